```python
import math
import jax
import jax.numpy as jnp
from jax import lax
import numpy as np

D_MODEL = 2048
BATCH = 8
SEQ = 2048
DEPTH = 1
DEC_BATCH = 128
DEC_SEQ = 8
PAST_LEN = 2048
PAGE_SIZE = 128

GLA_HEADS = 4
GLA_DK = 128
GLA_DV = 256
GLA_GATE_RANK = 16
GLA_TAU = 16.0
GLA_CHUNK = 64
GLA_WIDTH = GLA_HEADS * GLA_DV
NSA_HEADS = 16
NSA_KV = 4
NSA_GROUP = NSA_HEADS // NSA_KV
NSA_HD = 64
NSA_WIDTH = NSA_HEADS * NSA_HD
CMP_LEN = 32
CMP_STRIDE = 16
CMP_HIDDEN = 2 * NSA_HD
SEL_LEN = 64
SEL_TOPN = 16
WINDOW = 512
Q_BLOCK = 128
ROPE_THETA = 10000.0
MIX_WIDTH = GLA_WIDTH + NSA_WIDTH
D_FF = 5632
LN_EPS = 1e-5
ALPHA = (2.0 * DEPTH) ** 0.25
BETA = (8.0 * DEPTH) ** -0.25
NEG = -1e30

SPLIT_SIZES = (GLA_HEADS * GLA_DK, GLA_HEADS * GLA_DK, GLA_WIDTH, GLA_GATE_RANK, GLA_WIDTH,
               NSA_WIDTH, NSA_KV * NSA_HD, NSA_KV * NSA_HD, NSA_KV * NSA_HD, NSA_KV * NSA_HD,
               NSA_KV * NSA_HD, NSA_KV * NSA_HD, NSA_HEADS * 3)
SPLIT_IS_VALUE = (False, False, True, False, False, False, False, True, False, True, False, True, False)
IN_WIDTH = sum(SPLIT_SIZES)
SPLIT_POINTS = tuple(int(p) for p in np.cumsum(SPLIT_SIZES)[:-1])

kernel_name = 'gla_nsa_hybrid_decode_step'


def layer_norm(x, g, b):
    xf = x.astype(jnp.float32)
    mu = jnp.mean(xf, -1, keepdims=True)
    var = jnp.mean(jnp.square(xf - mu), -1, keepdims=True)
    return ((xf - mu) * lax.rsqrt(var + LN_EPS) * g + b).astype(x.dtype)


def head_rms_norm(o, g):
    of = o.astype(jnp.float32)
    return of * lax.rsqrt(jnp.mean(of * of, -1, keepdims=True) + LN_EPS) * g


def swiglu(x, w_in, w_out):
    gate, up = jnp.split(x @ w_in, 2, axis=-1)
    return (jax.nn.silu(gate) * up) @ w_out


def rope(x, pos):
    half = x.shape[-1] // 2
    freq = ROPE_THETA ** (-jnp.arange(half, dtype=jnp.float32) / half)
    ang = pos.astype(jnp.float32)[:, None] * freq[None, :]
    cos = jnp.cos(ang)[None, :, None, :]
    sin = jnp.sin(ang)[None, :, None, :]
    x1 = x[..., :half].astype(jnp.float32)
    x2 = x[..., half:].astype(jnp.float32)
    return jnp.concatenate([x1 * cos - x2 * sin, x2 * cos + x1 * sin], -1).astype(x.dtype)


def gla_chunked(q, k, v, log_a, s0):
    B, T, H, DK = q.shape
    DV = v.shape[-1]
    C = math.gcd(T, GLA_CHUNK)
    n = T // C
    f32 = jnp.float32

    def to_chunks(a):
        return a.astype(f32).reshape(B, n, C, H, a.shape[-1]).transpose(1, 0, 3, 2, 4)

    xs = (to_chunks(q * (DK ** -0.5)), to_chunks(k), to_chunks(v), to_chunks(log_a))
    causal = jnp.tril(jnp.ones((C, C), bool))[None, None, :, :, None]

    def step(S, inp):
        qi, ki, vi, ai = inp
        b = jnp.cumsum(ai, axis=2)
        diff = b[:, :, :, None, :] - b[:, :, None, :, :]
        decay = jnp.exp(jnp.where(causal, diff, -jnp.inf))
        attn = jnp.einsum('bhik,bhijk,bhjk->bhij', qi, decay, ki)
        o = jnp.einsum('bhij,bhjv->bhiv', attn, vi) + jnp.einsum('bhik,bhkv->bhiv', qi * jnp.exp(b), S)
        b_last = b[:, :, -1:, :]
        S_new = jnp.exp(b_last[:, :, 0, :])[..., None] * S + jnp.einsum('bhjk,bhjv->bhkv', ki * jnp.exp(b_last - b), vi)
        return S_new, o

    S_fin, o = lax.scan(step, s0.astype(f32), xs)
    o = o.transpose(1, 0, 3, 2, 4).reshape(B, T, H, DV)
    return o, S_fin.astype(s0.dtype)


def compress(rows, pos_emb, w1, w2):
    B, L, KV, HD = rows.shape
    n_cmp = (L - CMP_LEN) // CMP_STRIDE + 1
    idx = np.arange(n_cmp)[:, None] * CMP_STRIDE + np.arange(CMP_LEN)[None, :]
    blocks = rows[:, idx] + pos_emb[:, None, :].astype(rows.dtype)
    flat = blocks.transpose(0, 1, 3, 2, 4).reshape(B, n_cmp, KV, CMP_LEN * HD)
    return jax.nn.gelu(flat @ w1) @ w2


def cmp_attention(q, kc, vc, qpos):
    N = kc.shape[1]
    s = jnp.einsum('btkgd,bnkd->btkgn', q, kc).astype(jnp.float32) * (NSA_HD ** -0.5)
    end = jnp.asarray(np.arange(N) * CMP_STRIDE + CMP_LEN - 1)
    valid = (end[None, :] <= qpos[:, None])[None, :, None, None, :]
    p = jax.nn.softmax(jnp.where(valid, s, NEG), axis=-1) * valid
    o = jnp.einsum('btkgn,bnkd->btkgd', p.astype(vc.dtype), vc)
    return o, p


def select_blocks(p_cmp, qpos, n_sel):
    N = p_cmp.shape[-1]
    cs = np.arange(N) * CMP_STRIDE
    ss = np.arange(n_sel) * SEL_LEN
    overlap = ((cs[:, None] < ss[None, :] + SEL_LEN) & (cs[:, None] + CMP_LEN > ss[None, :])).astype(np.float32)
    imp = jnp.einsum('btkgn,ns->btks', p_cmp, jnp.asarray(overlap))
    ssj = jnp.asarray(ss)[None, :]
    cur = (qpos[:, None] // SEL_LEN) * SEL_LEN
    start_ok = ssj <= qpos[:, None]
    forced = (ssj == 0) | (ssj == cur) | (ssj == cur - SEL_LEN)
    score = jnp.where(start_ok[None, :, None, :], jnp.where(forced[None, :, None, :], 1e9, imp), -1e9)
    val, idx = lax.top_k(score, min(SEL_TOPN, n_sel))
    return idx, val > -1e8


def sel_attention_item(q, qpos, idx, ok, kblk, vblk):
    Tq = q.shape[0]
    kv_ar = jnp.arange(NSA_KV)[None, :, None]
    kg = kblk[kv_ar, idx]
    vg = vblk[kv_ar, idx]
    s = jnp.einsum('tkgd,tkjsd->tkgjs', q, kg).astype(jnp.float32) * (NSA_HD ** -0.5)
    kpos = idx[..., None] * SEL_LEN + jnp.arange(SEL_LEN)
    valid = (ok[..., None] & (kpos <= qpos[:, None, None, None]))[:, :, None]
    s = jnp.where(valid, s, NEG)
    p = jax.nn.softmax(s.reshape(Tq, NSA_KV, NSA_GROUP, -1), axis=-1).reshape(s.shape)
    return jnp.einsum('tkgjs,tkjsd->tkgd', p.astype(vg.dtype), vg)


def sel_attention(q, qpos, idx, ok, k_rows, v_rows, n_sel):
    B, T = q.shape[:2]
    L = k_rows.shape[1]
    pad = n_sel * SEL_LEN - L

    def blocks(r):
        r = jnp.pad(r, ((0, 0), (0, pad), (0, 0), (0, 0)))
        return r.reshape(B, n_sel, SEL_LEN, NSA_KV, NSA_HD).transpose(0, 3, 1, 2, 4)

    kb, vb = blocks(k_rows), blocks(v_rows)
    qb = math.gcd(T, Q_BLOCK)
    nb = T // qb

    def items(a):
        return a.reshape(B * nb, qb, *a.shape[2:])

    b_id = jnp.repeat(jnp.arange(B), nb)
    qpos_items = jnp.tile(qpos.reshape(nb, qb), (B, 1))

    def fn(args):
        qi, pi, ii, oi, bi = args
        return sel_attention_item(qi, pi, ii, oi, kb[bi], vb[bi])

    o = lax.map(fn, (items(q), qpos_items, items(idx), items(ok), b_id))
    return o.reshape(B, T, NSA_KV, NSA_GROUP, NSA_HD)


def window_attention(q, qpos, k_full, v_full, kpos_full):
    B, T = q.shape[:2]
    qb = math.gcd(T, Q_BLOCK)
    nb = T // qb
    band = np.arange(nb)[:, None] * qb + np.arange(WINDOW + qb)[None, :]
    kb, vb, kp = k_full[:, band], v_full[:, band], kpos_full[band]
    qq = q.reshape(B, nb, qb, NSA_KV, NSA_GROUP, NSA_HD)
    qp = qpos.reshape(nb, qb)
    s = jnp.einsum('bnqkgd,bnskd->bnqkgs', qq, kb).astype(jnp.float32) * (NSA_HD ** -0.5)
    dist = qp[:, :, None] - kp[:, None, :]
    valid = ((kp[:, None, :] >= 0) & (dist >= 0) & (dist < WINDOW))[None, :, :, None, None, :]
    p = jax.nn.softmax(jnp.where(valid, s, NEG), axis=-1)
    o = jnp.einsum('bnqkgs,bnskd->bnqkgd', p.astype(vb.dtype), vb)
    return o.reshape(B, T, NSA_KV, NSA_GROUP, NSA_HD)


def token_mixing(h, qpos, past_cmp, past_slc, win_prefix, win_prefix_pos, gla_s0, w_buf,
                 w_in, w_gla_gate2, b_gla_gate2, gla_norm_g, cmp_pos, cmp_w1, cmp_w2, w_out):
    B, T, _ = h.shape
    f32 = jnp.float32
    (gq, gk, gv, ga, gr, nq, kc, vc, ks, vs, kw, vw, ng) = jnp.split(h @ w_in, SPLIT_POINTS, axis=-1)

    def heads(a, n):
        return a.reshape(B, T, n, -1)

    log_a = jax.nn.log_sigmoid((ga @ w_gla_gate2 + b_gla_gate2).astype(f32)) / GLA_TAU
    o_g, gla_s = gla_chunked(heads(gq, GLA_HEADS), heads(gk, GLA_HEADS), heads(gv, GLA_HEADS),
                             heads(log_a, GLA_HEADS), gla_s0)
    o_g = head_rms_norm(o_g, gla_norm_g) * jax.nn.silu(heads(gr, GLA_HEADS).astype(f32))

    q = rope(heads(nq, NSA_HEADS), qpos).reshape(B, T, NSA_KV, NSA_GROUP, NSA_HD)
    cmp_rows = jnp.stack([rope(heads(kc, NSA_KV), qpos), heads(vc, NSA_KV)], axis=2)
    slc_rows = jnp.stack([rope(heads(ks, NSA_KV), qpos), heads(vs, NSA_KV)], axis=2)
    win_rows = jnp.stack([rope(heads(kw, NSA_KV), qpos), heads(vw, NSA_KV)], axis=2)
    cmp_all = cmp_rows if past_cmp is None else jnp.concatenate([past_cmp, cmp_rows], axis=1)
    slc_all = slc_rows if past_slc is None else jnp.concatenate([past_slc, slc_rows], axis=1)

    k_cmp = compress(cmp_all[:, :, 0], cmp_pos[0], cmp_w1[0], cmp_w2[0])
    v_cmp = compress(cmp_all[:, :, 1], cmp_pos[1], cmp_w1[1], cmp_w2[1])
    o_c, p_c = cmp_attention(q, k_cmp, v_cmp, qpos)

    n_sel = -(-slc_all.shape[1] // SEL_LEN)
    idx, ok = select_blocks(p_c, qpos, n_sel)
    o_s = sel_attention(q, qpos, idx, ok, slc_all[:, :, 0], slc_all[:, :, 1], n_sel)

    win_all = jnp.concatenate([win_prefix.astype(win_rows.dtype), win_rows], axis=1)
    win_pos = jnp.concatenate([win_prefix_pos, qpos])
    o_w = window_attention(q, qpos, win_all[:, :, 0], win_all[:, :, 1], win_pos)

    g = jax.nn.sigmoid(ng.reshape(B, T, NSA_KV, NSA_GROUP, 3).astype(f32))
    o_n = g[..., 0:1] * o_c + g[..., 1:2] * o_s + g[..., 2:3] * o_w

    mix = jnp.concatenate([o_g.reshape(B, T, GLA_WIDTH), o_n.reshape(B, T, NSA_WIDTH)], axis=-1).astype(h.dtype)
    return mix @ w_out, cmp_rows, slc_rows, win_all[:, -w_buf:], gla_s


def decoder_layer(x, qpos, past_cmp, past_slc, win_prefix, win_prefix_pos, gla_s0, w_buf,
                  w_ffn1_in, w_ffn1_out, w_in, w_gla_gate2, b_gla_gate2, gla_norm_g,
                  cmp_pos, cmp_w1, cmp_w2, w_out, w_ffn2_in, w_ffn2_out, ln_g, ln_b):
    x = layer_norm(ALPHA * x + 0.5 * swiglu(x, w_ffn1_in, w_ffn1_out), ln_g[0], ln_b[0])
    m, cmp_rows, slc_rows, win_buf, gla_s = token_mixing(
        x, qpos, past_cmp, past_slc, win_prefix, win_prefix_pos, gla_s0, w_buf,
        w_in, w_gla_gate2, b_gla_gate2, gla_norm_g, cmp_pos, cmp_w1, cmp_w2, w_out)
    x = layer_norm(ALPHA * x + m, ln_g[1], ln_b[1])
    x = layer_norm(ALPHA * x + 0.5 * swiglu(x, w_ffn2_in, w_ffn2_out), ln_g[2], ln_b[2])
    return x, cmp_rows, slc_rows, win_buf, gla_s


def gather_pages(pool, page_table):
    db, n_pages = page_table.shape
    return pool[page_table].reshape(db, n_pages * PAGE_SIZE, *pool.shape[2:])


def setup_inputs(seed: int = 0) -> dict:
    key = jax.random.key(seed)
    k = jax.random.split(key, 24)
    f32 = jnp.float32
    n_pages = PAST_LEN // PAGE_SIZE
    n_pool = (DEC_BATCH * n_pages * 5) // 4
    w_buf = min(WINDOW, PAST_LEN)

    def nrm(kk, shape, scale):
        return jax.random.normal(kk, shape, f32) * scale

    col_scale = jnp.asarray(np.concatenate(
        [np.full((n,), BETA if is_v else 1.0, np.float32) for n, is_v in zip(SPLIT_SIZES, SPLIT_IS_VALUE)]))
    page_table = jax.random.permutation(k[6], n_pool)[: DEC_BATCH * n_pages].reshape(DEC_BATCH, n_pages).astype(jnp.int32)
    return {
        'x_prompt': nrm(k[0], (BATCH, SEQ, D_MODEL), 1.0),
        'x_sample': nrm(k[1], (DEC_BATCH, DEC_SEQ, D_MODEL), 1.0),
        'cache_cmp_kv': nrm(k[2], (DEPTH, n_pool, PAGE_SIZE, 2, NSA_KV, NSA_HD), 1.0),
        'cache_slc_kv': nrm(k[3], (DEPTH, n_pool, PAGE_SIZE, 2, NSA_KV, NSA_HD), 1.0),
        'cache_win_kv': nrm(k[4], (DEPTH, DEC_BATCH, w_buf, 2, NSA_KV, NSA_HD), 1.0),
        'state_gla': nrm(k[5], (DEPTH, DEC_BATCH, GLA_HEADS, GLA_DK, GLA_DV), 0.5),
        'page_table': page_table,
        'w_ffn1_in': nrm(k[7], (DEPTH, D_MODEL, 2 * D_FF), BETA * D_MODEL ** -0.5),
        'w_ffn1_out': nrm(k[8], (DEPTH, D_FF, D_MODEL), BETA * D_FF ** -0.5),
        'w_in': nrm(k[9], (DEPTH, D_MODEL, IN_WIDTH), D_MODEL ** -0.5) * col_scale,
        'w_gla_gate2': nrm(k[10], (DEPTH, GLA_GATE_RANK, GLA_HEADS * GLA_DK), GLA_GATE_RANK ** -0.5),
        'b_gla_gate2': nrm(k[11], (DEPTH, GLA_HEADS * GLA_DK), 0.1),
        'gla_norm_g': 1.0 + nrm(k[12], (DEPTH, GLA_DV), 0.01),
        'cmp_pos': nrm(k[13], (DEPTH, 2, CMP_LEN, NSA_HD), 0.02),
        'cmp_w1': nrm(k[14], (DEPTH, 2, CMP_LEN * NSA_HD, CMP_HIDDEN), (CMP_LEN * NSA_HD) ** -0.5),
        'cmp_w2': nrm(k[15], (DEPTH, 2, CMP_HIDDEN, NSA_HD), CMP_HIDDEN ** -0.5),
        'w_out': nrm(k[16], (DEPTH, MIX_WIDTH, D_MODEL), BETA * MIX_WIDTH ** -0.5),
        'w_ffn2_in': nrm(k[17], (DEPTH, D_MODEL, 2 * D_FF), BETA * D_MODEL ** -0.5),
        'w_ffn2_out': nrm(k[18], (DEPTH, D_FF, D_MODEL), BETA * D_FF ** -0.5),
        'ln_g': 1.0 + nrm(k[19], (DEPTH, 3, D_MODEL), 0.01),
        'ln_b': nrm(k[20], (DEPTH, 3, D_MODEL), 0.01),
    }


def reference(x_prompt, x_sample, cache_cmp_kv, cache_slc_kv, cache_win_kv, state_gla, page_table,
              w_ffn1_in, w_ffn1_out, w_in, w_gla_gate2, b_gla_gate2, gla_norm_g,
              cmp_pos, cmp_w1, cmp_w2, w_out, w_ffn2_in, w_ffn2_out, ln_g, ln_b):
    B, T = x_prompt.shape[:2]
    DB, TS = x_sample.shape[:2]
    past_len = page_table.shape[1] * PAGE_SIZE
    w_buf = cache_win_kv.shape[2]
    qpos_p = jnp.arange(T, dtype=jnp.int32)
    qpos_s = past_len + jnp.arange(TS, dtype=jnp.int32)
    pref_pos_p = jnp.arange(-WINDOW, 0, dtype=jnp.int32)
    pref_pos_s = jnp.concatenate([jnp.full((WINDOW - w_buf,), -1, jnp.int32),
                                  past_len - w_buf + jnp.arange(w_buf, dtype=jnp.int32)])
    win_pref_p = jnp.zeros((B, WINDOW, 2, NSA_KV, NSA_HD), x_prompt.dtype)
    gla_zero = jnp.zeros((B, GLA_HEADS, GLA_DK, GLA_DV), x_prompt.dtype)

    hp, hs = x_prompt, x_sample
    cmp_p, slc_p, win_p, gla_p = [], [], [], []
    cmp_s, slc_s, win_s, gla_s = [], [], [], []
    for l in range(DEPTH):
        weights = (w_ffn1_in[l], w_ffn1_out[l], w_in[l], w_gla_gate2[l], b_gla_gate2[l], gla_norm_g[l],
                   cmp_pos[l], cmp_w1[l], cmp_w2[l], w_out[l], w_ffn2_in[l], w_ffn2_out[l], ln_g[l], ln_b[l])
        hp, c_r, s_r, w_b, g_s = decoder_layer(hp, qpos_p, None, None, win_pref_p, pref_pos_p, gla_zero, w_buf, *weights)
        cmp_p.append(c_r); slc_p.append(s_r); win_p.append(w_b); gla_p.append(g_s)
        past_cmp = gather_pages(cache_cmp_kv[l], page_table)
        past_slc = gather_pages(cache_slc_kv[l], page_table)
        win_pref_s = jnp.pad(cache_win_kv[l], ((0, 0), (WINDOW - w_buf, 0), (0, 0), (0, 0), (0, 0)))
        hs, c_r, s_r, w_b, g_s = decoder_layer(hs, qpos_s, past_cmp, past_slc, win_pref_s, pref_pos_s, state_gla[l], w_buf, *weights)
        cmp_s.append(c_r); slc_s.append(s_r); win_s.append(w_b); gla_s.append(g_s)

    return (hp, hs,
            jnp.stack(cmp_p), jnp.stack(slc_p), jnp.stack(win_p), jnp.stack(gla_p),
            jnp.stack(cmp_s), jnp.stack(slc_s), jnp.stack(win_s), jnp.stack(gla_s))
```

```python
import functools
import math

import numpy as np
import jax
import jax.numpy as jnp
from jax import lax
from jax.experimental import pallas as pl
from jax.experimental.pallas import tpu as pltpu

F32 = jnp.float32
BF16 = jnp.bfloat16

GLA_HEADS = 4
GLA_DK = 128
GLA_DV = 256
GLA_GATE_RANK = 16
GLA_TAU = 16.0
NSA_HEADS = 16
NSA_KV = 4
NSA_GROUP = NSA_HEADS // NSA_KV
NSA_HD = 64
CMP_LEN = 32
CMP_STRIDE = 16
SEL_LEN = 64
SEL_TOPN = 16
WINDOW = 512
PAGE_SIZE = 128
ROPE_THETA = 10000.0
LN_EPS = 1e-5
NEG = -1e30

LANES = 128
SUBLANES = 8
VMEM_LIMIT_BYTES = 56 * 1024 * 1024

GLA_KW = GLA_HEADS * GLA_DK
GLA_VW = GLA_HEADS * GLA_DV
NSA_QW = NSA_HEADS * NSA_HD
NSA_KVW = NSA_KV * NSA_HD
ROWW = 2 * NSA_KVW
C_GQ = 0
C_GK = C_GQ + GLA_KW
C_GV = C_GK + GLA_KW
C_GR = C_GV + GLA_VW
C_NQ = C_GR + GLA_VW
C_CMP = C_NQ + NSA_QW
C_SLC = C_CMP + ROWW
C_WIN = C_SLC + ROWW
C_TAIL = C_WIN + ROWW
PROJ_TN = 512
PROJ_W = C_TAIL + PROJ_TN
GATE_LANE0 = GLA_GATE_RANK

GLA_CHUNK = 128
GLA_SUB = 32
TQ = 128
KT = 128
NBLK = 128


def _cparams(sem):
    return pltpu.CompilerParams(dimension_semantics=sem, vmem_limit_bytes=VMEM_LIMIT_BYTES)


def _dot(a, b):
    return jnp.dot(a, b, preferred_element_type=F32)


def _dot_nt(a, b):
    return lax.dot_general(a, b, (((1,), (1,)), ((), ())), preferred_element_type=F32)


def _layer_norm(y, g, b):
    mu = jnp.mean(y, -1, keepdims=True)
    d = y - mu
    var = jnp.mean(d * d, -1, keepdims=True)
    return d * lax.rsqrt(var + LN_EPS) * g + b


def _split_bf16(x, n):
    parts = []
    for _ in range(n - 1):
        hi = x.astype(BF16)
        parts.append(hi)
        x = x - hi.astype(F32)
    parts.append(x.astype(BF16))
    return parts


def _ffn_ln_kernel(x_ref, wg_ref, wu_ref, wo_ref, g_ref, b_ref, y_ref, yb_ref, xb_ref, acc_ref, *, alpha):
    j = pl.program_id(1)

    @pl.when(j == 0)
    def _():
        xb_ref[...] = x_ref[...].astype(BF16)
        acc_ref[...] = jnp.zeros_like(acc_ref)

    xb = xb_ref[...]
    h = _dot(xb, wg_ref[...])
    u = _dot(xb, wu_ref[...])
    a = (h * jax.nn.sigmoid(h) * u).astype(BF16)
    acc_ref[...] += _dot(a, wo_ref[...])

    @pl.when(j == pl.num_programs(1) - 1)
    def _():
        y = _layer_norm(alpha * x_ref[...] + 0.5 * acc_ref[...], g_ref[...], b_ref[...])
        y_ref[...] = y
        yb_ref[...] = y.astype(BF16)


def _ffn_ln(x, w_in, w_out, g, b, alpha, tm, tf):
    n, d = x.shape
    dff = w_out.shape[0]
    nf = dff // tf
    assert n % tm == 0 and dff % tf == 0
    return pl.pallas_call(
        functools.partial(_ffn_ln_kernel, alpha=alpha),
        grid=(n // tm, nf),
        in_specs=[
            pl.BlockSpec((tm, d), lambda i, j: (i, 0)),
            pl.BlockSpec((d, tf), lambda i, j: (0, j)),
            pl.BlockSpec((d, tf), lambda i, j: (0, j + nf)),
            pl.BlockSpec((tf, d), lambda i, j: (j, 0)),
            pl.BlockSpec((1, d), lambda i, j: (0, 0)),
            pl.BlockSpec((1, d), lambda i, j: (0, 0)),
        ],
        out_specs=[pl.BlockSpec((tm, d), lambda i, j: (i, 0)),
                   pl.BlockSpec((tm, d), lambda i, j: (i, 0))],
        out_shape=[jax.ShapeDtypeStruct((n, d), F32), jax.ShapeDtypeStruct((n, d), BF16)],
        scratch_shapes=[pltpu.VMEM((tm, d), BF16), pltpu.VMEM((tm, d), F32)],
        compiler_params=_cparams(("parallel", "arbitrary")),
        name="ffn_ln",
    )(x, w_in, w_in, w_out, g, b)


def _rope_lanes(x, cos, sin_signed):
    lane = lax.broadcasted_iota(jnp.int32, x.shape, 1)
    first_half = (lane & (NSA_HD - 1)) < (NSA_HD // 2)
    partner = jnp.where(first_half, pltpu.roll(x, LANES - NSA_HD // 2, 1), pltpu.roll(x, NSA_HD // 2, 1))
    return x * cos + partner * sin_signed


def _in_proj_kernel(x_ref, w_ref, cos_ref, sin_ref, o_ref):
    j = pl.program_id(1)
    acc = _dot(x_ref[...], w_ref[...])
    j_q0, j_q1 = C_NQ // PROJ_TN, C_CMP // PROJ_TN
    j_k1 = C_TAIL // PROJ_TN
    n_slab = PROJ_TN // LANES

    def rope_first(n_rot):
        cos, sin = cos_ref[...], sin_ref[...]
        for c in range(n_slab):
            sl = slice(c * LANES, (c + 1) * LANES)
            o_ref[:, sl] = _rope_lanes(acc[:, sl], cos, sin) if c < n_rot else acc[:, sl]

    @pl.when((j < j_q0) | (j >= j_k1))
    def _():
        o_ref[...] = acc

    @pl.when((j >= j_q0) & (j < j_q1))
    def _():
        rope_first(n_slab)

    @pl.when((j >= j_q1) & (j < j_k1))
    def _():
        rope_first(NSA_KVW // LANES)


def _in_proj(xb, w, cos, sin, tm):
    n, d = xb.shape
    period = cos.shape[0]
    assert n % tm == 0 and period % tm == 0
    npb = period // tm
    return pl.pallas_call(
        _in_proj_kernel,
        grid=(n // tm, PROJ_W // PROJ_TN),
        in_specs=[
            pl.BlockSpec((tm, d), lambda i, j: (i, 0)),
            pl.BlockSpec((d, PROJ_TN), lambda i, j: (0, j)),
            pl.BlockSpec((tm, LANES), lambda i, j: (i % npb, 0)),
            pl.BlockSpec((tm, LANES), lambda i, j: (i % npb, 0)),
        ],
        out_specs=pl.BlockSpec((tm, PROJ_TN), lambda i, j: (i, j)),
        out_shape=jax.ShapeDtypeStruct((n, PROJ_W), F32),
        compiler_params=_cparams(("parallel", "arbitrary")),
        name="in_proj",
    )(xb, w, cos, sin)


def _gla_kernel(q_ref, k_ref, v_ref, r_ref, ga_ref, w2_ref, b2_ref, gn_ref, s0_ref,
                o_ref, sout_ref, st_ref, *, n_valid, n_chunks):
    t = pl.program_id(2)
    C, S = GLA_CHUNK, GLA_SUB

    @pl.when(t == 0)
    def _():
        st_ref[...] = s0_ref[0, 0].T

    row_k = lax.broadcasted_iota(jnp.int32, (C, GLA_DK), 0)
    row_c = lax.broadcasted_iota(jnp.int32, (C, C), 0)
    col_c = lax.broadcasted_iota(jnp.int32, (C, C), 1)
    causal = row_c >= col_c
    tri = causal.astype(BF16)

    def load(ref, r0):
        if n_valid >= C:
            return ref[pl.ds(r0, C), :]
        x = ref[...]
        return jnp.concatenate([x, jnp.zeros((C - n_valid, x.shape[1]), x.dtype)], axis=0)

    def chunk(c, carry):
        r0 = pl.multiple_of(c * C, C)
        q = load(q_ref, r0) * (GLA_DK ** -0.5)
        k = load(k_ref, r0)
        v = load(v_ref, r0)
        pre = _dot(load(ga_ref, r0).astype(BF16), w2_ref[0]) + b2_ref[0]
        la = (jnp.minimum(pre, 0.0) - jnp.log(1.0 + jnp.exp(-jnp.abs(pre)))) * (1.0 / GLA_TAU)
        if n_valid < C:
            la = jnp.where(row_k < n_valid, la, 0.0)
        la_hi, la_lo = _split_bf16(la, 2)
        b = _dot(tri, la_hi) + _dot(tri, la_lo)
        b_last = b[C - 1:C, :]

        strips = []
        for i in range(C // S):
            rows = slice(i * S, (i + 1) * S)
            anchor = jnp.zeros((1, GLA_DK), F32) if i == 0 else b[i * S - 1:i * S, :]
            qi = q[rows] * jnp.exp(b[rows] - anchor)
            ki = k * jnp.exp(jnp.where(row_k < (i + 1) * S, anchor - b, 0.0))
            strips.append(_dot_nt(qi.astype(BF16), ki.astype(BF16)))
        attn = jnp.where(causal, jnp.concatenate(strips, axis=0), 0.0)

        st = st_ref[...]
        o = _dot_nt((q * jnp.exp(b)).astype(BF16), st.astype(BF16)) + _dot(attn.astype(BF16), v.astype(BF16))
        ke = k * jnp.exp(b_last - b)
        st_ref[...] = st * jnp.exp(b_last) + _dot(v.T.astype(BF16), ke.astype(BF16))

        rms = lax.rsqrt(jnp.mean(o * o, -1, keepdims=True) + LN_EPS)
        r = load(r_ref, r0)
        out = (o * rms * gn_ref[...] * (r * jax.nn.sigmoid(r))).astype(o_ref.dtype)
        if n_valid >= C:
            o_ref[pl.ds(r0, C), :] = out
        else:
            o_ref[...] = out[:n_valid]
        return carry

    lax.fori_loop(0, n_chunks, chunk, 0)

    @pl.when(t == pl.num_programs(2) - 1)
    def _():
        sout_ref[0, 0] = st_ref[...].T


def _gla(proj, w2h, b2h, gn, s0, n_seq, t_seq):
    C = GLA_CHUNK
    if t_seq >= C:
        tb = math.gcd(t_seq, 1024)
        assert tb % C == 0
        n_valid, n_chunks = C, tb // C
    else:
        tb, n_valid, n_chunks = t_seq, t_seq, 1
        assert t_seq % SUBLANES == 0
    nt = t_seq // tb
    kcol, vcol = GLA_DK, GLA_DV
    rowblk = lambda b, h, t: b * nt + t
    return pl.pallas_call(
        functools.partial(_gla_kernel, n_valid=n_valid, n_chunks=n_chunks),
        grid=(n_seq, GLA_HEADS, nt),
        in_specs=[
            pl.BlockSpec((tb, kcol), lambda b, h, t: (rowblk(b, h, t), C_GQ // kcol + h)),
            pl.BlockSpec((tb, kcol), lambda b, h, t: (rowblk(b, h, t), C_GK // kcol + h)),
            pl.BlockSpec((tb, vcol), lambda b, h, t: (rowblk(b, h, t), C_GV // vcol + h)),
            pl.BlockSpec((tb, vcol), lambda b, h, t: (rowblk(b, h, t), C_GR // vcol + h)),
            pl.BlockSpec((tb, LANES), lambda b, h, t: (rowblk(b, h, t), C_TAIL // LANES)),
            pl.BlockSpec((1, LANES, GLA_DK), lambda b, h, t: (h, 0, 0)),
            pl.BlockSpec((1, 1, GLA_DK), lambda b, h, t: (h, 0, 0)),
            pl.BlockSpec((1, GLA_DV), lambda b, h, t: (0, 0)),
            pl.BlockSpec((1, 1, GLA_DK, GLA_DV), lambda b, h, t: (b, h, 0, 0)),
        ],
        out_specs=[
            pl.BlockSpec((tb, vcol), lambda b, h, t: (rowblk(b, h, t), h)),
            pl.BlockSpec((1, 1, GLA_DK, GLA_DV), lambda b, h, t: (b, h, 0, 0)),
        ],
        out_shape=[jax.ShapeDtypeStruct((n_seq * t_seq, GLA_VW), BF16 if t_seq >= C else F32),
                   jax.ShapeDtypeStruct((n_seq, GLA_HEADS, GLA_DK, GLA_DV), F32)],
        scratch_shapes=[pltpu.VMEM((GLA_DV, GLA_DK), F32)],
        compiler_params=_cparams(("parallel", "parallel", "arbitrary")),
        name="gla",
    )(proj, proj, proj, proj, proj, w2h, b2h, gn, s0)


def _gelu_tanh(x):
    return 0.5 * x * (1.0 + jnp.tanh(math.sqrt(2.0 / math.pi) * (x + 0.044715 * (x * x * x))))


def _compress_kernel(*refs, n_scalar, n_src, n_cmp):
    refs = refs[n_scalar:]
    src = refs[:n_src]
    pos_ref, w1_ref, w2_ref, kc_ref, vc_ref, slab_ref = refs[n_src:]
    r_src = src[0].shape[0]
    n_grp = n_src * r_src // CMP_STRIDE
    row = lax.broadcasted_iota(jnp.int32, (n_grp, LANES), 0)
    outs = (kc_ref, vc_ref)
    for part in range(2):
        for pair in range(NSA_KVW // LANES):
            c0 = part * NSA_KVW + pair * LANES
            for n, s in enumerate(src):
                slab_ref[n * r_src:(n + 1) * r_src, :] = s[:, c0:c0 + LANES]
            acc_a = jnp.zeros((n_grp, 2 * LANES), F32)
            acc_b = jnp.zeros((n_grp, 2 * LANES), F32)
            for l in range(CMP_STRIDE):
                x = slab_ref[pl.ds(l, n_grp, stride=CMP_STRIDE), :]
                acc_a += _dot((x + pos_ref[part, l:l + 1, :]).astype(BF16), w1_ref[part, l])
                acc_b += _dot((x + pos_ref[part, CMP_STRIDE + l:CMP_STRIDE + l + 1, :]).astype(BF16),
                              w1_ref[part, CMP_STRIDE + l])
            hidden = acc_a + pltpu.roll(acc_b, n_grp - 1, 0)
            out = _dot(_gelu_tanh(hidden).astype(BF16), w2_ref[part])
            outs[part][0, :, pair * LANES:(pair + 1) * LANES] = jnp.where(row < n_cmp, out, 0.0)


def _compress(srcs, src_specs, pos2, w1bd, w2bd, n_seq, n_grp, n_cmp, scalar_args=()):
    n_src = len(srcs)
    nsp = len(scalar_args)
    in_specs = list(src_specs) + [
        pl.BlockSpec(pos2.shape, lambda b, *_: (0, 0, 0)),
        pl.BlockSpec(w1bd.shape, lambda b, *_: (0, 0, 0, 0)),
        pl.BlockSpec(w2bd.shape, lambda b, *_: (0, 0, 0)),
    ]
    out_spec = pl.BlockSpec((1, n_grp, NSA_KVW), lambda b, *_: (b, 0, 0))
    return pl.pallas_call(
        functools.partial(_compress_kernel, n_scalar=nsp, n_src=n_src, n_cmp=n_cmp),
        grid_spec=pltpu.PrefetchScalarGridSpec(
            num_scalar_prefetch=nsp, grid=(n_seq,), in_specs=in_specs, out_specs=[out_spec, out_spec],
            scratch_shapes=[pltpu.VMEM((n_grp * CMP_STRIDE, LANES), F32)]),
        out_shape=[jax.ShapeDtypeStruct((n_seq, n_grp, NSA_KVW), F32)] * 2,
        compiler_params=_cparams(("parallel",)),
        name="compress",
    )(*scalar_args, *srcs, pos2, w1bd, w2bd)


def _build_q(q_ref, qs_ref, tq):
    lane = lax.broadcasted_iota(jnp.int32, (tq, LANES), 1)
    for j in range(NSA_HEADS):
        h = j // NSA_GROUP
        x = q_ref[:, (j // 2) * LANES:(j // 2 + 1) * LANES] * (NSA_HD ** -0.5)
        if (j % 2) != (h % 2):
            x = pltpu.roll(x, NSA_HD, 1)
        keep = (lane >= NSA_HD) if (h % 2) else (lane < NSA_HD)
        qs_ref[j * tq:(j + 1) * tq, :] = jnp.where(keep, x, 0.0).astype(qs_ref.dtype)


def _pair_slab(h):
    return slice((h // 2) * LANES, (h // 2 + 1) * LANES)


def _tile_rows(x, n):
    return jnp.concatenate([x] * n, axis=0)


def _cmp_attention(qh, kc, vc, qpos, n_cmp):
    tq = qpos.shape[0]
    blk = lax.broadcasted_iota(jnp.int32, (tq, NBLK), 1)
    valid = ((blk * CMP_STRIDE + (CMP_LEN - 1) <= qpos) & (blk < n_cmp)).astype(F32)
    valid4 = _tile_rows(valid, NSA_GROUP)
    s = jnp.where(valid4 > 0.5, _dot_nt(qh, kc.astype(BF16)), NEG)
    e = jnp.exp(s - jnp.max(s, -1, keepdims=True))
    p = e / jnp.sum(e, -1, keepdims=True) * valid4
    o = _dot(p.astype(BF16), vc.astype(BF16))
    psum = p[0:tq]
    for g in range(1, NSA_GROUP):
        psum = psum + p[g * tq:(g + 1) * tq]
    return o, psum


def _select_blocks(psum, ovl, qpos, n_sel):
    imp = sum(_dot(t, ovl) for t in _split_bf16(psum, 3))
    blk = lax.broadcasted_iota(jnp.int32, psum.shape, 1)
    ss = blk * SEL_LEN
    cur = qpos - (qpos & (SEL_LEN - 1))
    forced = (ss == 0) | (ss == cur) | (ss == cur - SEL_LEN)
    score = jnp.where(ss <= qpos, jnp.where(forced, 1e9, imp), -1e9)
    nb = -(-n_sel // SUBLANES) * SUBLANES
    st = score.T[0:nb]
    sub = lax.broadcasted_iota(jnp.int32, st.shape, 0)
    rank = jnp.zeros(st.shape, F32)
    for s in range(n_sel):
        other = st[s:s + 1, :]
        beats = (other > st) | ((other == st) & (sub > s))
        rank = rank + beats.astype(F32)
    sel_t = ((rank < SEL_TOPN) & (st > -1e8)).astype(F32)
    sel_t = jnp.concatenate([sel_t, jnp.zeros((NBLK - nb, st.shape[1]), F32)], axis=0)
    return sel_t.T


def _emit_branch(out_ref, o, gate, h, tq, first):
    lane = lax.broadcasted_iota(jnp.int32, (tq, LANES), 1)
    for gp in range(NSA_GROUP // 2):
        y0 = o[(2 * gp) * tq:(2 * gp + 1) * tq]
        y1 = o[(2 * gp + 1) * tq:(2 * gp + 2) * tq]
        if h % 2:
            y0 = pltpu.roll(y0, NSA_HD, 1)
        else:
            y1 = pltpu.roll(y1, NSA_HD, 1)
        m = 2 * h + gp
        sl = slice(m * LANES, (m + 1) * LANES)
        val = jnp.where(lane < NSA_HD, y0, y1) * gate[:, sl]
        if first:
            out_ref[:, sl] = val
        else:
            out_ref[:, sl] += val


def _expand_gates(tail, gexp_ref):
    g_hi, g_lo = _split_bf16(jax.nn.sigmoid(tail), 2)
    return [_dot(g_hi, gexp_ref[c]) + _dot(g_lo, gexp_ref[c]) for c in range(3)]


def _nsa_prompt_kernel(q_ref, slc_ref, win_ref, tail_ref, kc_ref, vc_ref, ovl_ref, bexp_ref, gexp_ref,
                       o_ref, qs_ref, mexp_ref, m_ref, l_ref, acc_ref, out_ref, *, n_cmp, n_sel):
    i = pl.program_id(1)
    q0 = i * TQ
    G = NSA_GROUP
    _build_q(q_ref, qs_ref, TQ)
    gates = _expand_gates(tail_ref[:, 0:LANES], gexp_ref)
    qpos = q0 + lax.broadcasted_iota(jnp.int32, (TQ, NBLK), 0)
    lane_k = lax.broadcasted_iota(jnp.int32, (TQ, KT), 1)
    n_kt = mexp_ref.shape[1]

    def flash(src_ref, h, qh, lo, hi, mask_fn):
        m_ref[...] = jnp.full(m_ref.shape, NEG, F32)
        l_ref[...] = jnp.zeros(l_ref.shape, F32)
        acc_ref[...] = jnp.zeros(acc_ref.shape, F32)

        def body(kt, carry):
            k0 = pl.multiple_of(kt * KT, KT)
            kk = src_ref[pl.ds(k0, KT), _pair_slab(h)].astype(BF16)
            vv = src_ref[pl.ds(k0, KT), NSA_KVW + (h // 2) * LANES:NSA_KVW + (h // 2 + 1) * LANES].astype(BF16)
            mask = _tile_rows(mask_fn(kt, k0 + lane_k), G)
            s = jnp.where(mask > 0.5, _dot_nt(qh, kk), NEG)
            m_prev = m_ref[...]
            m_new = jnp.maximum(m_prev, jnp.max(s, -1, keepdims=True))
            alpha = jnp.exp(m_prev - m_new)
            p = jnp.exp(s - m_new)
            l_ref[...] = alpha * l_ref[...] + jnp.sum(p, -1, keepdims=True)
            acc_ref[...] = alpha * acc_ref[...] + _dot(p.astype(BF16), vv)
            m_ref[...] = m_new
            return carry

        lax.fori_loop(lo, hi, body, 0)
        return acc_ref[...] / l_ref[...]

    for h in range(NSA_KV):
        qh = qs_ref[h * G * TQ:(h + 1) * G * TQ, :]
        o_c, psum = _cmp_attention(qh, kc_ref[0, :, _pair_slab(h)], vc_ref[0, :, _pair_slab(h)], qpos, n_cmp)
        _emit_branch(out_ref, o_c, gates[0], h, TQ, True)

        sel = _select_blocks(psum, ovl_ref[...], qpos, n_sel)
        mexp = _dot(sel.astype(BF16), bexp_ref[...])
        for kt in range(n_kt):
            mexp_ref[h, kt] = mexp[:, kt * KT:(kt + 1) * KT]

        def sel_mask(kt, kpos, h=h):
            return jnp.where((mexp_ref[h, kt] > 0.5) & (kpos <= qpos), 1.0, 0.0)

        o_s = flash(slc_ref, h, qh, 0, i + 1, sel_mask)
        _emit_branch(out_ref, o_s, gates[1], h, TQ, False)

        def win_mask(kt, kpos):
            dist = qpos - kpos
            return jnp.where((dist >= 0) & (dist < WINDOW), 1.0, 0.0)

        o_w = flash(win_ref, h, qh, jnp.maximum(i - WINDOW // KT, 0), i + 1, win_mask)
        _emit_branch(out_ref, o_w, gates[2], h, TQ, False)

    o_ref[...] = out_ref[...].astype(o_ref.dtype)


def _nsa_prompt(proj, kc, vc, ovl, bexp, gexp, n_seq, t_seq):
    assert t_seq % TQ == 0 and TQ == KT == LANES
    nq = t_seq // TQ
    n_cmp = (t_seq - CMP_LEN) // CMP_STRIDE + 1
    n_sel = -(-t_seq // SEL_LEN)
    G = NSA_GROUP
    return pl.pallas_call(
        functools.partial(_nsa_prompt_kernel, n_cmp=n_cmp, n_sel=n_sel),
        grid=(n_seq, nq),
        in_specs=[
            pl.BlockSpec((TQ, NSA_QW), lambda b, i: (b * nq + i, C_NQ // NSA_QW)),
            pl.BlockSpec((t_seq, ROWW), lambda b, i: (b, C_SLC // ROWW)),
            pl.BlockSpec((t_seq, ROWW), lambda b, i: (b, C_WIN // ROWW)),
            pl.BlockSpec((TQ, PROJ_TN), lambda b, i: (b * nq + i, C_TAIL // PROJ_TN)),
            pl.BlockSpec((1, NBLK, NSA_KVW), lambda b, i: (b, 0, 0)),
            pl.BlockSpec((1, NBLK, NSA_KVW), lambda b, i: (b, 0, 0)),
            pl.BlockSpec(ovl.shape, lambda b, i: (0, 0)),
            pl.BlockSpec(bexp.shape, lambda b, i: (0, 0)),
            pl.BlockSpec(gexp.shape, lambda b, i: (0, 0, 0)),
        ],
        out_specs=pl.BlockSpec((TQ, NSA_QW), lambda b, i: (b * nq + i, 0)),
        out_shape=jax.ShapeDtypeStruct((n_seq * t_seq, NSA_QW), BF16),
        scratch_shapes=[
            pltpu.VMEM((NSA_HEADS * TQ, LANES), BF16),
            pltpu.VMEM((NSA_KV, t_seq // KT, TQ, KT), F32),
            pltpu.VMEM((G * TQ, 1), F32),
            pltpu.VMEM((G * TQ, 1), F32),
            pltpu.VMEM((G * TQ, LANES), F32),
            pltpu.VMEM((TQ, NSA_QW), F32),
        ],
        compiler_params=_cparams(("parallel", "arbitrary")),
        name="nsa_prompt",
    )(proj, proj, proj, proj, kc, vc, ovl, bexp, gexp)


def _nsa_sample_kernel(*refs, n_pages, n_wt, t_new, past_len, n_cmp, n_sel):
    pt_ref = refs[0]
    q_ref, snew_ref, wnew_ref, tail_ref, kc_ref, vc_ref, wcache_ref = refs[1:8]
    pages = refs[8:8 + n_pages]
    ovl_ref, bexp_ref, gexp_ref, o_ref, qs_ref, out_ref = refs[8 + n_pages:]
    del pt_ref
    G = NSA_GROUP
    _build_q(q_ref, qs_ref, t_new)
    gates = _expand_gates(tail_ref[:, 0:LANES], gexp_ref)
    qpos_blk = past_len + lax.broadcasted_iota(jnp.int32, (TQ, NBLK), 0)
    qpos = qpos_blk[0:t_new]
    lane_k = lax.broadcasted_iota(jnp.int32, (t_new, KT), 1)
    pad_new = lambda x: jnp.concatenate([x, jnp.zeros((KT - t_new, x.shape[1]), x.dtype)], axis=0)

    def attend(qh, k_tiles, v_tiles, masks):
        s = jnp.concatenate(
            [jnp.where(_tile_rows(m, G) > 0.5, _dot_nt(qh, kk.astype(BF16)), NEG) for kk, m in zip(k_tiles, masks)],
            axis=1)
        e = jnp.exp(s - jnp.max(s, -1, keepdims=True))
        p = (e / jnp.sum(e, -1, keepdims=True)).astype(BF16)
        o = jnp.zeros((G * t_new, LANES), F32)
        for n, vv in enumerate(v_tiles):
            o = o + _dot(p[:, n * KT:(n + 1) * KT], vv.astype(BF16))
        return o

    for h in range(NSA_KV):
        ks, vs = _pair_slab(h), slice(NSA_KVW + (h // 2) * LANES, NSA_KVW + (h // 2 + 1) * LANES)
        qh = qs_ref[h * G * t_new:(h + 1) * G * t_new, :].astype(BF16)
        o_c, psum = _cmp_attention(qh, kc_ref[0, :, ks], vc_ref[0, :, ks], qpos, n_cmp)
        _emit_branch(out_ref, o_c, gates[0], h, t_new, True)

        psum_blk = jnp.concatenate([psum, jnp.zeros((TQ - t_new, NBLK), F32)], axis=0)
        sel = _select_blocks(psum_blk, ovl_ref[...], qpos_blk, n_sel)
        mexp = _dot(sel.astype(BF16), bexp_ref[...])[0:t_new]
        k_tiles = [pg[:, ks] for pg in pages] + [pad_new(snew_ref[:, ks])]
        v_tiles = [pg[:, vs] for pg in pages] + [pad_new(snew_ref[:, vs])]
        masks = [jnp.where((mexp[:, n * KT:(n + 1) * KT] > 0.5) & (n * KT + lane_k <= qpos[:, 0:KT]), 1.0, 0.0)
                 for n in range(n_pages + 1)]
        o_s = attend(qh, k_tiles, v_tiles, masks)
        _emit_branch(out_ref, o_s, gates[1], h, t_new, False)

        w0 = past_len - n_wt * KT
        k_tiles = [wcache_ref[n * KT:(n + 1) * KT, ks] for n in range(n_wt)] + [pad_new(wnew_ref[:, ks])]
        v_tiles = [wcache_ref[n * KT:(n + 1) * KT, vs] for n in range(n_wt)] + [pad_new(wnew_ref[:, vs])]
        masks = []
        for n in range(n_wt + 1):
            dist = qpos[:, 0:KT] - (w0 + n * KT + lane_k)
            masks.append(jnp.where((dist >= 0) & (dist < WINDOW), 1.0, 0.0))
        o_w = attend(qh, k_tiles, v_tiles, masks)
        _emit_branch(out_ref, o_w, gates[2], h, t_new, False)

    o_ref[...] = out_ref[...].astype(o_ref.dtype)


def _nsa_sample(proj, kc, vc, slc_pool, win_cache, page_table, ovl, bexp, gexp, n_seq, t_new, past_len):
    n_pages = page_table.shape[1]
    w_buf = win_cache.shape[0] // n_seq
    assert w_buf % KT == 0 and t_new % SUBLANES == 0 and t_new <= KT
    n_wt = w_buf // KT
    n_cmp = (past_len + t_new - CMP_LEN) // CMP_STRIDE + 1
    n_sel = -(-(past_len + t_new) // SEL_LEN)
    page_spec = lambda p: pl.BlockSpec((PAGE_SIZE, ROWW), lambda b, pt, p=p: (pt[b, p], 0))
    in_specs = [
        pl.BlockSpec((t_new, NSA_QW), lambda b, pt: (b, C_NQ // NSA_QW)),
        pl.BlockSpec((t_new, ROWW), lambda b, pt: (b, C_SLC // ROWW)),
        pl.BlockSpec((t_new, ROWW), lambda b, pt: (b, C_WIN // ROWW)),
        pl.BlockSpec((t_new, PROJ_TN), lambda b, pt: (b, C_TAIL // PROJ_TN)),
        pl.BlockSpec((1, NBLK, NSA_KVW), lambda b, pt: (b, 0, 0)),
        pl.BlockSpec((1, NBLK, NSA_KVW), lambda b, pt: (b, 0, 0)),
        pl.BlockSpec((w_buf, ROWW), lambda b, pt: (b, 0)),
    ] + [page_spec(p) for p in range(n_pages)] + [
        pl.BlockSpec(ovl.shape, lambda b, pt: (0, 0)),
        pl.BlockSpec(bexp.shape, lambda b, pt: (0, 0)),
        pl.BlockSpec(gexp.shape, lambda b, pt: (0, 0, 0)),
    ]
    return pl.pallas_call(
        functools.partial(_nsa_sample_kernel, n_pages=n_pages, n_wt=n_wt, t_new=t_new, past_len=past_len,
                          n_cmp=n_cmp, n_sel=n_sel),
        grid_spec=pltpu.PrefetchScalarGridSpec(
            num_scalar_prefetch=1, grid=(n_seq,), in_specs=in_specs,
            out_specs=pl.BlockSpec((t_new, NSA_QW), lambda b, pt: (b, 0)),
            scratch_shapes=[pltpu.VMEM((NSA_HEADS * t_new, LANES), F32), pltpu.VMEM((t_new, NSA_QW), F32)]),
        out_shape=jax.ShapeDtypeStruct((n_seq * t_new, NSA_QW), F32),
        compiler_params=_cparams(("parallel",)),
        name="nsa_sample",
    )(page_table, proj, proj, proj, proj, kc, vc, win_cache, *([slc_pool] * n_pages), ovl, bexp, gexp)


def _out_ln_kernel(og_ref, on_ref, wg_ref, wn_ref, x_ref, g_ref, b_ref, y_ref, yb_ref, *, alpha):
    m = _dot(og_ref[...].astype(BF16), wg_ref[...]) + _dot(on_ref[...].astype(BF16), wn_ref[...])
    y = _layer_norm(alpha * x_ref[...] + m, g_ref[...], b_ref[...])
    y_ref[...] = y
    yb_ref[...] = y.astype(BF16)


def _out_ln(og, on, wg, wn, x, g, b, alpha, tm):
    n, d = x.shape
    assert n % tm == 0
    return pl.pallas_call(
        functools.partial(_out_ln_kernel, alpha=alpha),
        grid=(n // tm,),
        in_specs=[
            pl.BlockSpec((tm, og.shape[1]), lambda i: (i, 0)),
            pl.BlockSpec((tm, on.shape[1]), lambda i: (i, 0)),
            pl.BlockSpec(wg.shape, lambda i: (0, 0)),
            pl.BlockSpec(wn.shape, lambda i: (0, 0)),
            pl.BlockSpec((tm, d), lambda i: (i, 0)),
            pl.BlockSpec((1, d), lambda i: (0, 0)),
            pl.BlockSpec((1, d), lambda i: (0, 0)),
        ],
        out_specs=[pl.BlockSpec((tm, d), lambda i: (i, 0)), pl.BlockSpec((tm, d), lambda i: (i, 0))],
        out_shape=[jax.ShapeDtypeStruct((n, d), F32), jax.ShapeDtypeStruct((n, d), BF16)],
        compiler_params=_cparams(("parallel",)),
        name="out_ln",
    )(og, on, wg, wn, x, g, b)


def _rope_tables(pos):
    half = NSA_HD // 2
    freq = ROPE_THETA ** (-jnp.arange(half, dtype=F32) / half)
    ang = pos.astype(F32)[:, None] * freq[None, :]
    cos, sin = jnp.cos(ang), jnp.sin(ang)
    cos_h = jnp.concatenate([cos, cos], -1)
    sin_h = jnp.concatenate([-sin, sin], -1)
    reps = LANES // NSA_HD
    return jnp.tile(cos_h, (1, reps)), jnp.tile(sin_h, (1, reps))


def _reorder_w_in(w):
    d = w.shape[0]
    sizes = (GLA_KW, GLA_KW, GLA_VW, GLA_GATE_RANK, GLA_VW, NSA_QW) + (NSA_KVW,) * 6 + (NSA_HEADS * 3,)
    offs = np.concatenate([[0], np.cumsum(sizes)])
    gq, gk, gv, ga, gr, nq, kc, vc, ks, vs, kw, vw, ng = [w[:, offs[i]:offs[i + 1]] for i in range(len(sizes))]
    pad = jnp.zeros((d, PROJ_TN - GLA_GATE_RANK - NSA_HEADS * 3), w.dtype)
    return jnp.concatenate([gq, gk, gv, gr, nq, kc, vc, ks, vs, kw, vw, ga, ng, pad], axis=1)


def _block_diag2(w):
    z = jnp.zeros_like(w)
    return jnp.concatenate([jnp.concatenate([w, z], -1), jnp.concatenate([z, w], -1)], -2)


def _tile_size(n, pref):
    t = math.gcd(n, pref)
    assert t % SUBLANES == 0
    return t


def _const_tables(n_keys):
    cs = np.arange(NBLK)[:, None] * CMP_STRIDE
    ss = np.arange(NBLK)[None, :] * SEL_LEN
    ovl = ((cs < ss + SEL_LEN) & (cs + CMP_LEN > ss)).astype(np.float32)
    bexp = (np.arange(NBLK)[:, None] == (np.arange(n_keys)[None, :] // SEL_LEN)).astype(np.float32)
    gexp = np.zeros((3, LANES, NSA_QW), np.float32)
    for c in range(3):
        for j in range(NSA_HEADS):
            gexp[c, GATE_LANE0 + 3 * j + c, j * NSA_HD:(j + 1) * NSA_HD] = 1.0
    return jnp.asarray(ovl, BF16), jnp.asarray(bexp, BF16), jnp.asarray(gexp, BF16)


def kernel(x_prompt, x_sample, cache_cmp_kv, cache_slc_kv, cache_win_kv, state_gla, page_table,
           w_ffn1_in, w_ffn1_out, w_in, w_gla_gate2, b_gla_gate2, gla_norm_g,
           cmp_pos, cmp_w1, cmp_w2, w_out, w_ffn2_in, w_ffn2_out, ln_g, ln_b):
    B, T, D = x_prompt.shape
    DB, TS, _ = x_sample.shape
    depth = w_in.shape[0]
    n_pages = page_table.shape[1]
    past_len = n_pages * PAGE_SIZE
    w_buf = cache_win_kv.shape[2]
    alpha = (2.0 * depth) ** 0.25
    assert w_buf == WINDOW and T % TQ == 0 and past_len % KT == 0

    cos_p, sin_p = _rope_tables(jnp.arange(T, dtype=jnp.int32))
    cos_s, sin_s = _rope_tables(past_len + jnp.arange(TS, dtype=jnp.int32))
    ovl, bexp_p, gexp = _const_tables(T)
    _, bexp_s, _ = _const_tables(past_len + KT)

    hp = x_prompt.reshape(B * T, D)
    hs = x_sample.reshape(DB * TS, D)
    tm_s = _tile_size(DB * TS, 1024)
    cos_s, sin_s = (jnp.tile(t, (tm_s // TS, 1)) for t in (cos_s, sin_s))
    outs = {k: [] for k in ("cmp_p", "slc_p", "win_p", "gla_p", "cmp_s", "slc_s", "win_s", "gla_s")}

    for l in range(depth):
        wf1i, wf1o = w_ffn1_in[l].astype(BF16), w_ffn1_out[l].astype(BF16)
        wf2i, wf2o = w_ffn2_in[l].astype(BF16), w_ffn2_out[l].astype(BF16)
        wi = _reorder_w_in(w_in[l]).astype(BF16)
        wo_g, wo_n = w_out[l][:GLA_VW].astype(BF16), w_out[l][GLA_VW:].astype(BF16)
        lg, lb = ln_g[l][:, None, :], ln_b[l][:, None, :]
        w2h = jnp.zeros((GLA_HEADS, LANES, GLA_DK), F32).at[:, :GLA_GATE_RANK, :].set(
            w_gla_gate2[l].reshape(GLA_GATE_RANK, GLA_HEADS, GLA_DK).transpose(1, 0, 2)).astype(BF16)
        b2h = b_gla_gate2[l].reshape(GLA_HEADS, 1, GLA_DK)
        gn = gla_norm_g[l][None, :]
        pos2 = jnp.tile(cmp_pos[l], (1, 1, LANES // NSA_HD))
        w1bd = _block_diag2(cmp_w1[l].reshape(2, CMP_LEN, NSA_HD, -1)).astype(BF16)
        w2bd = _block_diag2(cmp_w2[l]).astype(BF16)

        h1, h1b = _ffn_ln(hp, wf1i, wf1o, lg[0], lb[0], alpha, _tile_size(B * T, 512), 512)
        proj = _in_proj(h1b, wi, cos_p, sin_p, _tile_size(T, 1024))
        og, gla_state = _gla(proj, w2h, b2h, gn, jnp.zeros((B, GLA_HEADS, GLA_DK, GLA_DV), F32), B, T)
        n_grp = T // CMP_STRIDE
        n_cmp = (T - CMP_LEN) // CMP_STRIDE + 1
        kc, vc = _compress([proj], [pl.BlockSpec((T, ROWW), lambda b: (b, C_CMP // ROWW))],
                           pos2, w1bd, w2bd, B, n_grp, n_cmp)
        kc, vc = (jnp.pad(a, ((0, 0), (0, NBLK - n_grp), (0, 0))) for a in (kc, vc))
        on = _nsa_prompt(proj, kc, vc, ovl, bexp_p, gexp, B, T)
        h2, h2b = _out_ln(og, on, wo_g, wo_n, h1, lg[1], lb[1], alpha, _tile_size(B * T, 512))
        hp, _ = _ffn_ln(h2, wf2i, wf2o, lg[2], lb[2], alpha, _tile_size(B * T, 512), 512)
        row6 = lambda a, n, t: a.reshape(n, t, 2, NSA_KV, NSA_HD)
        outs["cmp_p"].append(row6(proj[:, C_CMP:C_CMP + ROWW], B, T))
        outs["slc_p"].append(row6(proj[:, C_SLC:C_SLC + ROWW], B, T))
        outs["win_p"].append(row6(proj[:, C_WIN:C_WIN + ROWW], B, T)[:, T - w_buf:])
        outs["gla_p"].append(gla_state)

        s1, s1b = _ffn_ln(hs, wf1i, wf1o, lg[0], lb[0], alpha, _tile_size(DB * TS, 512), 512)
        sproj = _in_proj(s1b, wi, cos_s, sin_s, tm_s)
        sog, sgla_state = _gla(sproj, w2h, b2h, gn, state_gla[l], DB, TS)
        cmp_pool = cache_cmp_kv[l].reshape(-1, ROWW)
        slc_pool = cache_slc_kv[l].reshape(-1, ROWW)
        win_cache = cache_win_kv[l].reshape(DB * w_buf, ROWW)
        n_grp_s = past_len // CMP_STRIDE
        n_cmp_s = (past_len + TS - CMP_LEN) // CMP_STRIDE + 1
        page_specs = [pl.BlockSpec((PAGE_SIZE, ROWW), lambda b, pt, p=p: (pt[b, p], 0)) for p in range(n_pages)]
        skc, svc = _compress([cmp_pool] * n_pages, page_specs, pos2, w1bd, w2bd, DB, n_grp_s, n_cmp_s,
                             scalar_args=(page_table,))
        skc, svc = (jnp.pad(a, ((0, 0), (0, NBLK - n_grp_s), (0, 0))) for a in (skc, svc))
        son = _nsa_sample(sproj, skc, svc, slc_pool, win_cache, page_table, ovl, bexp_s, gexp, DB, TS, past_len)
        s2, s2b = _out_ln(sog, son, wo_g, wo_n, s1, lg[1], lb[1], alpha, _tile_size(DB * TS, 512))
        hs, _ = _ffn_ln(s2, wf2i, wf2o, lg[2], lb[2], alpha, _tile_size(DB * TS, 512), 512)
        new_win = row6(sproj[:, C_WIN:C_WIN + ROWW], DB, TS)
        outs["cmp_s"].append(row6(sproj[:, C_CMP:C_CMP + ROWW], DB, TS))
        outs["slc_s"].append(row6(sproj[:, C_SLC:C_SLC + ROWW], DB, TS))
        outs["win_s"].append(jnp.concatenate([cache_win_kv[l], new_win], axis=1)[:, -w_buf:])
        outs["gla_s"].append(sgla_state)

    st = lambda k: jnp.stack(outs[k])
    return (hp.reshape(B, T, D), hs.reshape(DB, TS, D),
            st("cmp_p"), st("slc_p"), st("win_p"), st("gla_p"),
            st("cmp_s"), st("slc_s"), st("win_s"), st("gla_s"))
```

```python
import functools
import math

import numpy as np
import jax
import jax.numpy as jnp
from jax import lax
from jax.experimental import pallas as pl
from jax.experimental.pallas import tpu as pltpu

F32 = jnp.float32
BF16 = jnp.bfloat16

GLA_HEADS = 4
GLA_DK = 128
GLA_DV = 256
GLA_GATE_RANK = 16
GLA_TAU = 16.0
NSA_HEADS = 16
NSA_KV = 4
NSA_GROUP = NSA_HEADS // NSA_KV
NSA_HD = 64
CMP_LEN = 32
CMP_STRIDE = 16
SEL_LEN = 64
SEL_TOPN = 16
WINDOW = 512
PAGE_SIZE = 128
ROPE_THETA = 10000.0
LN_EPS = 1e-5
NEG = -1e30

LANES = 128
SUBLANES = 8
VMEM_LIMIT_BYTES = 56 * 1024 * 1024

GLA_KW = GLA_HEADS * GLA_DK
GLA_VW = GLA_HEADS * GLA_DV
NSA_QW = NSA_HEADS * NSA_HD
NSA_KVW = NSA_KV * NSA_HD
ROWW = 2 * NSA_KVW
C_GQ = 0
C_GK = C_GQ + GLA_KW
C_GV = C_GK + GLA_KW
C_GR = C_GV + GLA_VW
C_NQ = C_GR + GLA_VW
C_CMP = C_NQ + NSA_QW
C_SLC = C_CMP + ROWW
C_WIN = C_SLC + ROWW
C_TAIL = C_WIN + ROWW
PROJ_TN = 512
PROJ_W = C_TAIL + PROJ_TN
GATE_LANE0 = GLA_GATE_RANK

GLA_CHUNK = 128
GLA_SUB = 32
TQ = 128
KT = 128
FKT = 256
NBLK = 128


def _cparams(sem):
    return pltpu.CompilerParams(dimension_semantics=sem, vmem_limit_bytes=VMEM_LIMIT_BYTES)


def _dot(a, b):
    return jnp.dot(a, b, preferred_element_type=F32)


def _dot_nt(a, b):
    return lax.dot_general(a, b, (((1,), (1,)), ((), ())), preferred_element_type=F32)


def _layer_norm(y, g, b):
    mu = jnp.mean(y, -1, keepdims=True)
    d = y - mu
    var = jnp.mean(d * d, -1, keepdims=True)
    return d * lax.rsqrt(var + LN_EPS) * g + b


def _split_bf16(x, n):
    parts = []
    for _ in range(n - 1):
        hi = x.astype(BF16)
        parts.append(hi)
        x = x - hi.astype(F32)
    parts.append(x.astype(BF16))
    return parts


def _ffn_ln_kernel(x_ref, wg_ref, wu_ref, wo_ref, g_ref, b_ref, y_ref, yb_ref, xb_ref, acc_ref, *, alpha):
    j = pl.program_id(1)

    @pl.when(j == 0)
    def _():
        xb_ref[...] = x_ref[...].astype(BF16)
        acc_ref[...] = jnp.zeros_like(acc_ref)

    xb = xb_ref[...]
    h = _dot(xb, wg_ref[...])
    u = _dot(xb, wu_ref[...])
    a = (h * jax.nn.sigmoid(h) * u).astype(BF16)
    acc_ref[...] += _dot(a, wo_ref[...])

    @pl.when(j == pl.num_programs(1) - 1)
    def _():
        y = _layer_norm(alpha * x_ref[...] + 0.5 * acc_ref[...], g_ref[...], b_ref[...])
        y_ref[...] = y
        yb_ref[...] = y.astype(BF16)


def _ffn_ln(x, w_in, w_out, g, b, alpha, tm, tf):
    n, d = x.shape
    dff = w_out.shape[0]
    nf = dff // tf
    assert n % tm == 0 and dff % tf == 0
    return pl.pallas_call(
        functools.partial(_ffn_ln_kernel, alpha=alpha),
        grid=(n // tm, nf),
        in_specs=[
            pl.BlockSpec((tm, d), lambda i, j: (i, 0)),
            pl.BlockSpec((d, tf), lambda i, j: (0, j)),
            pl.BlockSpec((d, tf), lambda i, j: (0, j + nf)),
            pl.BlockSpec((tf, d), lambda i, j: (j, 0)),
            pl.BlockSpec((1, d), lambda i, j: (0, 0)),
            pl.BlockSpec((1, d), lambda i, j: (0, 0)),
        ],
        out_specs=[pl.BlockSpec((tm, d), lambda i, j: (i, 0)),
                   pl.BlockSpec((tm, d), lambda i, j: (i, 0))],
        out_shape=[jax.ShapeDtypeStruct((n, d), F32), jax.ShapeDtypeStruct((n, d), BF16)],
        scratch_shapes=[pltpu.VMEM((tm, d), BF16), pltpu.VMEM((tm, d), F32)],
        compiler_params=_cparams(("parallel", "arbitrary")),
        name="ffn_ln",
    )(x, w_in, w_in, w_out, g, b)


def _rope_lanes(x, cos, sin_signed):
    lane = lax.broadcasted_iota(jnp.int32, x.shape, 1)
    first_half = (lane & (NSA_HD - 1)) < (NSA_HD // 2)
    partner = jnp.where(first_half, pltpu.roll(x, LANES - NSA_HD // 2, 1), pltpu.roll(x, NSA_HD // 2, 1))
    return x * cos + partner * sin_signed


def _in_proj_kernel(x_ref, w_ref, cos_ref, sin_ref, o_ref):
    j = pl.program_id(1)
    acc = _dot(x_ref[...], w_ref[...])
    j_q0, j_q1 = C_NQ // PROJ_TN, C_CMP // PROJ_TN
    j_k1 = C_TAIL // PROJ_TN
    n_slab = PROJ_TN // LANES

    def rope_first(n_rot):
        cos, sin = cos_ref[...], sin_ref[...]
        for c in range(n_slab):
            sl = slice(c * LANES, (c + 1) * LANES)
            o_ref[:, sl] = _rope_lanes(acc[:, sl], cos, sin) if c < n_rot else acc[:, sl]

    @pl.when((j < j_q0) | (j >= j_k1))
    def _():
        o_ref[...] = acc

    @pl.when((j >= j_q0) & (j < j_q1))
    def _():
        rope_first(n_slab)

    @pl.when((j >= j_q1) & (j < j_k1))
    def _():
        rope_first(NSA_KVW // LANES)


def _in_proj(xb, w, cos, sin, tm):
    n, d = xb.shape
    period = cos.shape[0]
    assert n % tm == 0 and period % tm == 0
    npb = period // tm
    return pl.pallas_call(
        _in_proj_kernel,
        grid=(n // tm, PROJ_W // PROJ_TN),
        in_specs=[
            pl.BlockSpec((tm, d), lambda i, j: (i, 0)),
            pl.BlockSpec((d, PROJ_TN), lambda i, j: (0, j)),
            pl.BlockSpec((tm, LANES), lambda i, j: (i % npb, 0)),
            pl.BlockSpec((tm, LANES), lambda i, j: (i % npb, 0)),
        ],
        out_specs=pl.BlockSpec((tm, PROJ_TN), lambda i, j: (i, j)),
        out_shape=jax.ShapeDtypeStruct((n, PROJ_W), F32),
        compiler_params=_cparams(("parallel", "arbitrary")),
        name="in_proj",
    )(xb, w, cos, sin)


def _gla_kernel(q_ref, k_ref, v_ref, r_ref, ga_ref, w2_ref, b2_ref, gn_ref, s0_ref,
                o_ref, sout_ref, st_ref, *, n_valid, n_chunks, n_sb):
    t = pl.program_id(1)
    C, S, H = GLA_CHUNK, GLA_SUB, GLA_HEADS

    @pl.when(t == 0)
    def _():
        for sb in range(n_sb):
            for h in range(H):
                st_ref[sb * H + h] = s0_ref[sb, h].T

    row_k = lax.broadcasted_iota(jnp.int32, (C, GLA_DK), 0)
    row_a = lax.broadcasted_iota(jnp.int32, (C, GLA_KW), 0)
    row_c = lax.broadcasted_iota(jnp.int32, (C, C), 0)
    col_c = lax.broadcasted_iota(jnp.int32, (C, C), 1)
    causal = row_c >= col_c
    tri = causal.astype(BF16)

    def load(ref, r0, sb, c0, w):
        if n_valid >= C:
            return ref[pl.ds(r0, C), c0:c0 + w]
        x = ref[sb * n_valid:(sb + 1) * n_valid, c0:c0 + w]
        return jnp.concatenate([x, jnp.zeros((C - n_valid, w), x.dtype)], axis=0)

    def chunk(c, carry):
        r0 = pl.multiple_of(c * C, C)
        for sb in range(n_sb):
            pre = _dot(load(ga_ref, r0, sb, 0, LANES).astype(BF16), w2_ref[...]) + b2_ref[...]
            la = (jnp.minimum(pre, 0.0) - jnp.log(1.0 + jnp.exp(-jnp.abs(pre)))) * (1.0 / GLA_TAU)
            if n_valid < C:
                la = jnp.where(row_a < n_valid, la, 0.0)
            la_hi, la_lo = _split_bf16(la, 2)
            b_all = _dot(tri, la_hi) + _dot(tri, la_lo)
            for h in range(H):
                b = b_all[:, h * GLA_DK:(h + 1) * GLA_DK]
                q = load(q_ref, r0, sb, h * GLA_DK, GLA_DK) * (GLA_DK ** -0.5)
                k = load(k_ref, r0, sb, h * GLA_DK, GLA_DK)
                v = load(v_ref, r0, sb, h * GLA_DV, GLA_DV)
                b_last = b[C - 1:C, :]

                strips = []
                for i in range(C // S):
                    rows = slice(i * S, (i + 1) * S)
                    anchor = jnp.zeros((1, GLA_DK), F32) if i == 0 else b[i * S - 1:i * S, :]
                    qi = q[rows] * jnp.exp(b[rows] - anchor)
                    ki = k * jnp.exp(jnp.where(row_k < (i + 1) * S, anchor - b, 0.0))
                    strips.append(_dot_nt(qi.astype(BF16), ki.astype(BF16)))
                attn = jnp.where(causal, jnp.concatenate(strips, axis=0), 0.0)

                st = st_ref[sb * H + h]
                o = (_dot_nt((q * jnp.exp(b)).astype(BF16), st.astype(BF16))
                     + _dot(attn.astype(BF16), v.astype(BF16)))
                ke = k * jnp.exp(b_last - b)
                st_ref[sb * H + h] = st * jnp.exp(b_last) + _dot(v.T.astype(BF16), ke.astype(BF16))

                rms = lax.rsqrt(jnp.mean(o * o, -1, keepdims=True) + LN_EPS)
                r = load(r_ref, r0, sb, h * GLA_DV, GLA_DV)
                out = (o * rms * gn_ref[...] * (r * jax.nn.sigmoid(r))).astype(o_ref.dtype)
                cols = slice(h * GLA_DV, (h + 1) * GLA_DV)
                if n_valid >= C:
                    o_ref[pl.ds(r0, C), cols] = out
                else:
                    o_ref[sb * n_valid:(sb + 1) * n_valid, cols] = out[:n_valid]
        return carry

    lax.fori_loop(0, n_chunks, chunk, 0)

    @pl.when(t == pl.num_programs(1) - 1)
    def _():
        for sb in range(n_sb):
            for h in range(H):
                sout_ref[sb, h] = st_ref[sb * H + h].T


def _gla(proj, w2h, b2h, gn, s0, n_seq, t_seq):
    C, H = GLA_CHUNK, GLA_HEADS
    if t_seq >= C:
        tb = math.gcd(t_seq, 1024)
        assert tb % C == 0
        n_valid, n_chunks, n_sb = C, tb // C, 1
    else:
        tb, n_valid, n_chunks = t_seq, t_seq, 1
        n_sb = math.gcd(n_seq, 2)
        assert t_seq % SUBLANES == 0
    nt = t_seq // tb
    rows = n_sb * tb
    rowblk = lambda b, t: b * nt + t
    return pl.pallas_call(
        functools.partial(_gla_kernel, n_valid=n_valid, n_chunks=n_chunks, n_sb=n_sb),
        grid=(n_seq // n_sb, nt),
        in_specs=[
            pl.BlockSpec((rows, GLA_KW), lambda b, t: (rowblk(b, t), C_GQ // GLA_KW)),
            pl.BlockSpec((rows, GLA_KW), lambda b, t: (rowblk(b, t), C_GK // GLA_KW)),
            pl.BlockSpec((rows, GLA_VW), lambda b, t: (rowblk(b, t), C_GV // GLA_VW)),
            pl.BlockSpec((rows, GLA_VW), lambda b, t: (rowblk(b, t), C_GR // GLA_VW)),
            pl.BlockSpec((rows, LANES), lambda b, t: (rowblk(b, t), C_TAIL // LANES)),
            pl.BlockSpec((LANES, GLA_KW), lambda b, t: (0, 0)),
            pl.BlockSpec((1, GLA_KW), lambda b, t: (0, 0)),
            pl.BlockSpec((1, GLA_DV), lambda b, t: (0, 0)),
            pl.BlockSpec((n_sb, H, GLA_DK, GLA_DV), lambda b, t: (b, 0, 0, 0)),
        ],
        out_specs=[
            pl.BlockSpec((rows, GLA_VW), lambda b, t: (rowblk(b, t), 0)),
            pl.BlockSpec((n_sb, H, GLA_DK, GLA_DV), lambda b, t: (b, 0, 0, 0)),
        ],
        out_shape=[jax.ShapeDtypeStruct((n_seq * t_seq, GLA_VW), BF16 if t_seq >= C else F32),
                   jax.ShapeDtypeStruct((n_seq, H, GLA_DK, GLA_DV), F32)],
        scratch_shapes=[pltpu.VMEM((n_sb * H, GLA_DV, GLA_DK), F32)],
        compiler_params=_cparams(("parallel", "arbitrary")),
        name="gla",
    )(proj, proj, proj, proj, proj, w2h, b2h, gn, s0)


def _gelu_tanh(x):
    return 0.5 * x * (1.0 + jnp.tanh(math.sqrt(2.0 / math.pi) * (x + 0.044715 * (x * x * x))))


def _compress_kernel(*refs, n_scalar, n_src, n_cmp, feature_major):
    refs = refs[n_scalar:]
    src = refs[:n_src]
    pos_ref, w1_ref, w2_ref, kc_ref, vc_ref, slab_ref = refs[n_src:]
    r_src = src[0].shape[1] if feature_major else src[0].shape[0]
    n_grp = n_src * r_src // CMP_STRIDE
    row = lax.broadcasted_iota(jnp.int32, (n_grp, LANES), 0)
    outs = (kc_ref, vc_ref)
    for part in range(2):
        for pair in range(NSA_KVW // LANES):
            c0 = part * NSA_KVW + pair * LANES
            for n, s in enumerate(src):
                slab_ref[n * r_src:(n + 1) * r_src, :] = s[c0:c0 + LANES, :].T if feature_major else s[:, c0:c0 + LANES]
            acc_a = jnp.zeros((n_grp, 2 * LANES), F32)
            acc_b = jnp.zeros((n_grp, 2 * LANES), F32)
            for l in range(CMP_STRIDE):
                x = slab_ref[pl.ds(l, n_grp, stride=CMP_STRIDE), :]
                acc_a += _dot((x + pos_ref[part, l:l + 1, :]).astype(BF16), w1_ref[part, l])
                acc_b += _dot((x + pos_ref[part, CMP_STRIDE + l:CMP_STRIDE + l + 1, :]).astype(BF16),
                              w1_ref[part, CMP_STRIDE + l])
            hidden = acc_a + pltpu.roll(acc_b, n_grp - 1, 0)
            out = _dot(_gelu_tanh(hidden).astype(BF16), w2_ref[part])
            outs[part][0, :, pair * LANES:(pair + 1) * LANES] = jnp.where(row < n_cmp, out, 0.0)


def _compress(srcs, src_specs, pos2, w1bd, w2bd, n_seq, n_grp, n_cmp, scalar_args=(), feature_major=False):
    n_src = len(srcs)
    nsp = len(scalar_args)
    in_specs = list(src_specs) + [
        pl.BlockSpec(pos2.shape, lambda b, *_: (0, 0, 0)),
        pl.BlockSpec(w1bd.shape, lambda b, *_: (0, 0, 0, 0)),
        pl.BlockSpec(w2bd.shape, lambda b, *_: (0, 0, 0)),
    ]
    out_spec = pl.BlockSpec((1, n_grp, NSA_KVW), lambda b, *_: (b, 0, 0))
    return pl.pallas_call(
        functools.partial(_compress_kernel, n_scalar=nsp, n_src=n_src, n_cmp=n_cmp, feature_major=feature_major),
        grid_spec=pltpu.PrefetchScalarGridSpec(
            num_scalar_prefetch=nsp, grid=(n_seq,), in_specs=in_specs, out_specs=[out_spec, out_spec],
            scratch_shapes=[pltpu.VMEM((n_grp * CMP_STRIDE, LANES), F32)]),
        out_shape=[jax.ShapeDtypeStruct((n_seq, n_grp, NSA_KVW), F32)] * 2,
        compiler_params=_cparams(("parallel",)),
        name="compress",
    )(*scalar_args, *srcs, pos2, w1bd, w2bd)


def _build_q(q_ref, qs_ref, tq):
    lane = lax.broadcasted_iota(jnp.int32, (tq, LANES), 1)
    for j in range(NSA_HEADS):
        h = j // NSA_GROUP
        x = q_ref[:, (j // 2) * LANES:(j // 2 + 1) * LANES] * (NSA_HD ** -0.5)
        if (j % 2) != (h % 2):
            x = pltpu.roll(x, NSA_HD, 1)
        keep = (lane >= NSA_HD) if (h % 2) else (lane < NSA_HD)
        qs_ref[j * tq:(j + 1) * tq, :] = jnp.where(keep, x, 0.0).astype(qs_ref.dtype)


def _pair_slab(h):
    return slice((h // 2) * LANES, (h // 2 + 1) * LANES)


def _tile_rows(x, n):
    return jnp.concatenate([x] * n, axis=0)


def _cmp_attention(qh, kc, vc, qpos, n_cmp):
    tq = qpos.shape[0]
    blk = lax.broadcasted_iota(jnp.int32, (tq, NBLK), 1)
    valid = ((blk * CMP_STRIDE + (CMP_LEN - 1) <= qpos) & (blk < n_cmp)).astype(F32)
    valid4 = _tile_rows(valid, NSA_GROUP)
    s = jnp.where(valid4 > 0.5, _dot_nt(qh, kc.astype(BF16)), NEG)
    e = jnp.exp(s - jnp.max(s, -1, keepdims=True))
    p = e / jnp.sum(e, -1, keepdims=True) * valid4
    o = _dot(p.astype(BF16), vc.astype(BF16))
    psum = p[0:tq]
    for g in range(1, NSA_GROUP):
        psum = psum + p[g * tq:(g + 1) * tq]
    return o, psum


def _select_blocks(psum, ovl, qpos, n_sel):
    imp = sum(_dot(t, ovl) for t in _split_bf16(psum, 3))
    blk = lax.broadcasted_iota(jnp.int32, psum.shape, 1)
    ss = blk * SEL_LEN
    cur = qpos - (qpos & (SEL_LEN - 1))
    forced = (ss == 0) | (ss == cur) | (ss == cur - SEL_LEN)
    score = jnp.where(ss <= qpos, jnp.where(forced, 1e9, imp), -1e9)
    nb = -(-n_sel // SUBLANES) * SUBLANES
    st = score.T[0:nb]
    sub = lax.broadcasted_iota(jnp.int32, st.shape, 0)
    rank = jnp.zeros(st.shape, F32)
    for s in range(n_sel):
        other = st[s:s + 1, :]
        beats = (other > st) | ((other == st) & (sub > s))
        rank = rank + beats.astype(F32)
    sel_t = ((rank < SEL_TOPN) & (st > -1e8)).astype(F32)
    sel_t = jnp.concatenate([sel_t, jnp.zeros((NBLK - nb, st.shape[1]), F32)], axis=0)
    return sel_t.T


def _emit_branch(out_ref, o, gate, h, tq, first):
    lane = lax.broadcasted_iota(jnp.int32, (tq, LANES), 1)
    for gp in range(NSA_GROUP // 2):
        y0 = o[(2 * gp) * tq:(2 * gp + 1) * tq]
        y1 = o[(2 * gp + 1) * tq:(2 * gp + 2) * tq]
        if h % 2:
            y0 = pltpu.roll(y0, NSA_HD, 1)
        else:
            y1 = pltpu.roll(y1, NSA_HD, 1)
        m = 2 * h + gp
        sl = slice(m * LANES, (m + 1) * LANES)
        val = jnp.where(lane < NSA_HD, y0, y1) * gate[:, sl]
        if first:
            out_ref[:, sl] = val
        else:
            out_ref[:, sl] += val


def _expand_gates(tail, gexp_ref):
    g_hi, g_lo = _split_bf16(jax.nn.sigmoid(tail), 2)
    return [_dot(g_hi, gexp_ref[c]) + _dot(g_lo, gexp_ref[c]) for c in range(3)]


def _own_half(h, shape):
    lane = lax.broadcasted_iota(jnp.int32, shape, len(shape) - 1)
    return (lane >= NSA_HD) if (h % 2) else (lane < NSA_HD)


def _nsa_prompt_kernel(q_ref, slc_ref, win_ref, tail_ref, kc_ref, vc_ref, ovl_ref, bexp_ref, gexp_ref,
                       o_ref, qs_ref, mexp_ref, m_ref, acc_ref, out_ref, *, n_cmp, n_sel):
    i = pl.program_id(1)
    q0 = i * TQ
    G = NSA_GROUP
    R = G * TQ
    _build_q(q_ref, qs_ref, TQ)
    gates = _expand_gates(tail_ref[:, 0:LANES], gexp_ref)
    qpos = q0 + lax.broadcasted_iota(jnp.int32, (TQ, NBLK), 0)
    qpos_k = q0 + lax.broadcasted_iota(jnp.int32, (TQ, FKT), 0)
    lane_k = lax.broadcasted_iota(jnp.int32, (TQ, FKT), 1)
    n_kt = mexp_ref.shape[1]

    for h in range(NSA_KV):
        qh = qs_ref[h * R:(h + 1) * R, :]
        o_c, psum = _cmp_attention(qh, kc_ref[0, :, _pair_slab(h)], vc_ref[0, :, _pair_slab(h)], qpos, n_cmp)
        _emit_branch(out_ref, o_c, gates[0], h, TQ, True)
        sel = _select_blocks(psum, ovl_ref[...], qpos, n_sel)
        mexp = _dot(sel.astype(BF16), bexp_ref[...])
        for kt in range(n_kt):
            mexp_ref[h, kt] = mexp[:, kt * FKT:(kt + 1) * FKT]

    def flash(src_ref, lo, hi, bias_fn):
        m_ref[...] = jnp.full(m_ref.shape, NEG, F32)
        acc_ref[...] = jnp.zeros(acc_ref.shape, F32)

        def body(kt, carry):
            k0 = pl.multiple_of(kt * FKT, FKT)
            kpos = k0 + lane_k
            kslab = [src_ref[pl.ds(k0, FKT), m * LANES:(m + 1) * LANES].astype(BF16) for m in range(2)]
            vslab = [src_ref[pl.ds(k0, FKT), NSA_KVW + m * LANES:NSA_KVW + (m + 1) * LANES] for m in range(2)]
            for h in range(NSA_KV):
                vv = jnp.where(_own_half(h, (FKT, LANES)), vslab[h // 2], 1.0).astype(BF16)
                s = _dot_nt(qs_ref[h * R:(h + 1) * R, :], kslab[h // 2]) + _tile_rows(bias_fn(h, kt, kpos), G)
                m_prev = m_ref[h]
                smax = s[:, 0:LANES]
                for c in range(1, FKT // LANES):
                    smax = jnp.maximum(smax, s[:, c * LANES:(c + 1) * LANES])
                m_new = jnp.maximum(m_prev, jnp.max(smax, -1, keepdims=True))
                alpha = jnp.exp(m_prev - m_new)
                p = jnp.exp(s - jnp.concatenate([m_new] * (FKT // LANES), axis=1))
                acc_ref[h] = alpha * acc_ref[h] + _dot(p.astype(BF16), vv)
                m_ref[h] = m_new
            return carry

        lax.fori_loop(lo, hi, body, 0)
        outs = []
        for h in range(NSA_KV):
            acc = acc_ref[h]
            outs.append(acc / pltpu.roll(acc, NSA_HD, 1))
        return outs

    def sel_bias(h, kt, kpos):
        return jnp.where((mexp_ref[h, kt] > 0.5) & (kpos <= qpos_k), 0.0, NEG)

    def win_bias(h, kt, kpos):
        dist = qpos_k - kpos
        return jnp.where((dist >= 0) & (dist < WINDOW), 0.0, NEG)

    hi = (q0 + TQ + FKT - 1) // FKT
    for h, o_s in enumerate(flash(slc_ref, 0, hi, sel_bias)):
        _emit_branch(out_ref, o_s, gates[1], h, TQ, False)
    lo = jnp.maximum(q0 - WINDOW, 0) // FKT
    for h, o_w in enumerate(flash(win_ref, lo, hi, win_bias)):
        _emit_branch(out_ref, o_w, gates[2], h, TQ, False)
    o_ref[...] = out_ref[...].astype(o_ref.dtype)


def _nsa_prompt(proj, kc, vc, ovl, bexp, gexp, n_seq, t_seq):
    assert t_seq % FKT == 0 and TQ == NBLK == LANES
    nq = t_seq // TQ
    n_cmp = (t_seq - CMP_LEN) // CMP_STRIDE + 1
    n_sel = -(-t_seq // SEL_LEN)
    G = NSA_GROUP
    return pl.pallas_call(
        functools.partial(_nsa_prompt_kernel, n_cmp=n_cmp, n_sel=n_sel),
        grid=(n_seq, nq),
        in_specs=[
            pl.BlockSpec((TQ, NSA_QW), lambda b, i: (b * nq + i, C_NQ // NSA_QW)),
            pl.BlockSpec((t_seq, ROWW), lambda b, i: (b, C_SLC // ROWW)),
            pl.BlockSpec((t_seq, ROWW), lambda b, i: (b, C_WIN // ROWW)),
            pl.BlockSpec((TQ, PROJ_TN), lambda b, i: (b * nq + i, C_TAIL // PROJ_TN)),
            pl.BlockSpec((1, NBLK, NSA_KVW), lambda b, i: (b, 0, 0)),
            pl.BlockSpec((1, NBLK, NSA_KVW), lambda b, i: (b, 0, 0)),
            pl.BlockSpec(ovl.shape, lambda b, i: (0, 0)),
            pl.BlockSpec(bexp.shape, lambda b, i: (0, 0)),
            pl.BlockSpec(gexp.shape, lambda b, i: (0, 0, 0)),
        ],
        out_specs=pl.BlockSpec((TQ, NSA_QW), lambda b, i: (b * nq + i, 0)),
        out_shape=jax.ShapeDtypeStruct((n_seq * t_seq, NSA_QW), BF16),
        scratch_shapes=[
            pltpu.VMEM((NSA_HEADS * TQ, LANES), BF16),
            pltpu.VMEM((NSA_KV, t_seq // FKT, TQ, FKT), F32),
            pltpu.VMEM((NSA_KV, G * TQ, LANES), F32),
            pltpu.VMEM((NSA_KV, G * TQ, LANES), F32),
            pltpu.VMEM((TQ, NSA_QW), F32),
        ],
        compiler_params=_cparams(("parallel", "arbitrary")),
        name="nsa_prompt",
    )(proj, proj, proj, proj, kc, vc, ovl, bexp, gexp)


def _nsa_sample_kernel(*refs, n_pages, n_wt, t_new, past_len, n_cmp, n_sel):
    q_ref, snew_ref, wnew_ref, tail_ref, kc_ref, vc_ref, wcache_ref = refs[1:8]
    pages = refs[8:8 + n_pages]
    ovl_ref, bexp_ref, gexp_ref, o_ref, qs_ref, out_ref = refs[8 + n_pages:]
    G = NSA_GROUP
    R = NSA_HEADS * t_new
    half = R // 2
    _build_q(q_ref, qs_ref, t_new)
    gates = _expand_gates(tail_ref[:, 0:LANES], gexp_ref)
    zeros = jnp.zeros((half, LANES), F32)
    q_all = jnp.concatenate([jnp.concatenate([qs_ref[0:half, :], zeros], axis=1),
                             jnp.concatenate([zeros, qs_ref[half:R, :]], axis=1)], axis=0).astype(BF16)
    row_t = lax.broadcasted_iota(jnp.int32, (TQ, NBLK), 0) & (t_new - 1)
    qpos_blk = past_len + row_t
    pad_new = lambda x: jnp.concatenate([x, jnp.zeros((KT - t_new, x.shape[1]), x.dtype)], axis=0)

    def attend(k_t_tiles, v_t_tiles, k_new, v_new, bias):
        s = [_dot(q_all, kt.astype(BF16)) for kt in k_t_tiles] + [_dot_nt(q_all, pad_new(k_new).astype(BF16))]
        s = jnp.concatenate(s, axis=1) + bias
        e = jnp.exp(s - jnp.max(s, -1, keepdims=True))
        p = (e / jnp.sum(e, -1, keepdims=True)).astype(BF16)
        n = len(k_t_tiles)
        o = _dot(p[:, n * KT:(n + 1) * KT], pad_new(v_new).astype(BF16))
        for c, vt in enumerate(v_t_tiles):
            o = o + _dot_nt(p[:, c * KT:(c + 1) * KT], vt.astype(BF16))
        return o

    def emit(o_all, gate, first):
        for h in range(NSA_KV):
            _emit_branch(out_ref, o_all[h * G * t_new:(h + 1) * G * t_new, _pair_slab(h)], gate, h, t_new, first)

    qpos_r = qpos_blk[0:R]
    blk = lax.broadcasted_iota(jnp.int32, (R, NBLK), 1)
    valid = ((blk * CMP_STRIDE + (CMP_LEN - 1) <= qpos_r) & (blk < n_cmp)).astype(F32)
    s = jnp.where(valid > 0.5, _dot_nt(q_all, kc_ref[0].astype(BF16)), NEG)
    e = jnp.exp(s - jnp.max(s, -1, keepdims=True))
    p = e / jnp.sum(e, -1, keepdims=True) * valid
    emit(_dot(p.astype(BF16), vc_ref[0].astype(BF16)), gates[0], True)

    psum = []
    for h in range(NSA_KV):
        acc = p[(h * G) * t_new:(h * G + 1) * t_new]
        for g in range(1, G):
            acc = acc + p[(h * G + g) * t_new:(h * G + g + 1) * t_new]
        psum.append(acc)
    psum = jnp.concatenate(psum + [jnp.zeros((TQ - NSA_KV * t_new, NBLK), F32)], axis=0)
    sel = _select_blocks(psum, ovl_ref[...], qpos_blk, n_sel)
    mexp = _dot(sel.astype(BF16), bexp_ref[...])
    mexp = jnp.concatenate([_tile_rows(mexp[h * t_new:(h + 1) * t_new], G) for h in range(NSA_KV)], axis=0)
    n_keys = (n_pages + 1) * KT
    kpos = lax.broadcasted_iota(jnp.int32, (R, n_keys), 1)
    qpos_k = past_len + (lax.broadcasted_iota(jnp.int32, (R, n_keys), 0) & (t_new - 1))
    bias = jnp.where((mexp > 0.5) & (kpos <= qpos_k), 0.0, NEG)
    o_s = attend([pg[0:NSA_KVW, :] for pg in pages], [pg[NSA_KVW:ROWW, :] for pg in pages],
                 snew_ref[:, 0:NSA_KVW], snew_ref[:, NSA_KVW:ROWW], bias)
    emit(o_s, gates[1], False)

    n_keys = (n_wt + 1) * KT
    kpos = (past_len - n_wt * KT) + lax.broadcasted_iota(jnp.int32, (R, n_keys), 1)
    dist = past_len + (lax.broadcasted_iota(jnp.int32, (R, n_keys), 0) & (t_new - 1)) - kpos
    bias = jnp.where((dist >= 0) & (dist < WINDOW), 0.0, NEG)
    o_w = attend([wcache_ref[0:NSA_KVW, c * KT:(c + 1) * KT] for c in range(n_wt)],
                 [wcache_ref[NSA_KVW:ROWW, c * KT:(c + 1) * KT] for c in range(n_wt)],
                 wnew_ref[:, 0:NSA_KVW], wnew_ref[:, NSA_KVW:ROWW], bias)
    emit(o_w, gates[2], False)
    o_ref[...] = out_ref[...].astype(o_ref.dtype)


def _nsa_sample(proj, kc, vc, slc_pool_t, win_cache_t, page_table, ovl, bexp, gexp, n_seq, t_new, past_len):
    n_pages = page_table.shape[1]
    w_buf = win_cache_t.shape[1]
    assert w_buf % KT == 0 and t_new == SUBLANES and NSA_HEADS * t_new <= TQ and PAGE_SIZE == KT
    n_wt = w_buf // KT
    n_cmp = (past_len + t_new - CMP_LEN) // CMP_STRIDE + 1
    n_sel = -(-(past_len + t_new) // SEL_LEN)
    page_spec = lambda p: pl.BlockSpec((ROWW, PAGE_SIZE), lambda b, pt, p=p: (pt[b, p], 0))
    in_specs = [
        pl.BlockSpec((t_new, NSA_QW), lambda b, pt: (b, C_NQ // NSA_QW)),
        pl.BlockSpec((t_new, ROWW), lambda b, pt: (b, C_SLC // ROWW)),
        pl.BlockSpec((t_new, ROWW), lambda b, pt: (b, C_WIN // ROWW)),
        pl.BlockSpec((t_new, PROJ_TN), lambda b, pt: (b, C_TAIL // PROJ_TN)),
        pl.BlockSpec((1, NBLK, NSA_KVW), lambda b, pt: (b, 0, 0)),
        pl.BlockSpec((1, NBLK, NSA_KVW), lambda b, pt: (b, 0, 0)),
        pl.BlockSpec((ROWW, w_buf), lambda b, pt: (b, 0)),
    ] + [page_spec(p) for p in range(n_pages)] + [
        pl.BlockSpec(ovl.shape, lambda b, pt: (0, 0)),
        pl.BlockSpec(bexp.shape, lambda b, pt: (0, 0)),
        pl.BlockSpec(gexp.shape, lambda b, pt: (0, 0, 0)),
    ]
    return pl.pallas_call(
        functools.partial(_nsa_sample_kernel, n_pages=n_pages, n_wt=n_wt, t_new=t_new, past_len=past_len,
                          n_cmp=n_cmp, n_sel=n_sel),
        grid_spec=pltpu.PrefetchScalarGridSpec(
            num_scalar_prefetch=1, grid=(n_seq,), in_specs=in_specs,
            out_specs=pl.BlockSpec((t_new, NSA_QW), lambda b, pt: (b, 0)),
            scratch_shapes=[pltpu.VMEM((NSA_HEADS * t_new, LANES), F32), pltpu.VMEM((t_new, NSA_QW), F32)]),
        out_shape=jax.ShapeDtypeStruct((n_seq * t_new, NSA_QW), F32),
        compiler_params=_cparams(("parallel",)),
        name="nsa_sample",
    )(page_table, proj, proj, proj, proj, kc, vc, win_cache_t, *([slc_pool_t] * n_pages), ovl, bexp, gexp)


def _out_ln_kernel(og_ref, on_ref, wg_ref, wn_ref, x_ref, g_ref, b_ref, y_ref, yb_ref, *, alpha):
    m = _dot(og_ref[...].astype(BF16), wg_ref[...]) + _dot(on_ref[...].astype(BF16), wn_ref[...])
    y = _layer_norm(alpha * x_ref[...] + m, g_ref[...], b_ref[...])
    y_ref[...] = y
    yb_ref[...] = y.astype(BF16)


def _out_ln(og, on, wg, wn, x, g, b, alpha, tm):
    n, d = x.shape
    assert n % tm == 0
    return pl.pallas_call(
        functools.partial(_out_ln_kernel, alpha=alpha),
        grid=(n // tm,),
        in_specs=[
            pl.BlockSpec((tm, og.shape[1]), lambda i: (i, 0)),
            pl.BlockSpec((tm, on.shape[1]), lambda i: (i, 0)),
            pl.BlockSpec(wg.shape, lambda i: (0, 0)),
            pl.BlockSpec(wn.shape, lambda i: (0, 0)),
            pl.BlockSpec((tm, d), lambda i: (i, 0)),
            pl.BlockSpec((1, d), lambda i: (0, 0)),
            pl.BlockSpec((1, d), lambda i: (0, 0)),
        ],
        out_specs=[pl.BlockSpec((tm, d), lambda i: (i, 0)), pl.BlockSpec((tm, d), lambda i: (i, 0))],
        out_shape=[jax.ShapeDtypeStruct((n, d), F32), jax.ShapeDtypeStruct((n, d), BF16)],
        compiler_params=_cparams(("parallel",)),
        name="out_ln",
    )(og, on, wg, wn, x, g, b)


def _rope_tables(pos):
    half = NSA_HD // 2
    freq = ROPE_THETA ** (-jnp.arange(half, dtype=F32) / half)
    ang = pos.astype(F32)[:, None] * freq[None, :]
    cos, sin = jnp.cos(ang), jnp.sin(ang)
    cos_h = jnp.concatenate([cos, cos], -1)
    sin_h = jnp.concatenate([-sin, sin], -1)
    reps = LANES // NSA_HD
    return jnp.tile(cos_h, (1, reps)), jnp.tile(sin_h, (1, reps))


def _reorder_w_in(w):
    d = w.shape[0]
    sizes = (GLA_KW, GLA_KW, GLA_VW, GLA_GATE_RANK, GLA_VW, NSA_QW) + (NSA_KVW,) * 6 + (NSA_HEADS * 3,)
    offs = np.concatenate([[0], np.cumsum(sizes)])
    gq, gk, gv, ga, gr, nq, kc, vc, ks, vs, kw, vw, ng = [w[:, offs[i]:offs[i + 1]] for i in range(len(sizes))]
    pad = jnp.zeros((d, PROJ_TN - GLA_GATE_RANK - NSA_HEADS * 3), w.dtype)
    return jnp.concatenate([gq, gk, gv, gr, nq, kc, vc, ks, vs, kw, vw, ga, ng, pad], axis=1)


def _block_diag2(w):
    z = jnp.zeros_like(w)
    return jnp.concatenate([jnp.concatenate([w, z], -1), jnp.concatenate([z, w], -1)], -2)


def _tile_size(n, pref):
    t = math.gcd(n, pref)
    assert t % SUBLANES == 0
    return t


def _const_tables(n_keys):
    cs = np.arange(NBLK)[:, None] * CMP_STRIDE
    ss = np.arange(NBLK)[None, :] * SEL_LEN
    ovl = ((cs < ss + SEL_LEN) & (cs + CMP_LEN > ss)).astype(np.float32)
    bexp = (np.arange(NBLK)[:, None] == (np.arange(n_keys)[None, :] // SEL_LEN)).astype(np.float32)
    gexp = np.zeros((3, LANES, NSA_QW), np.float32)
    for c in range(3):
        for j in range(NSA_HEADS):
            gexp[c, GATE_LANE0 + 3 * j + c, j * NSA_HD:(j + 1) * NSA_HD] = 1.0
    return jnp.asarray(ovl, BF16), jnp.asarray(bexp, BF16), jnp.asarray(gexp, BF16)


def kernel(x_prompt, x_sample, cache_cmp_kv, cache_slc_kv, cache_win_kv, state_gla, page_table,
           w_ffn1_in, w_ffn1_out, w_in, w_gla_gate2, b_gla_gate2, gla_norm_g,
           cmp_pos, cmp_w1, cmp_w2, w_out, w_ffn2_in, w_ffn2_out, ln_g, ln_b):
    B, T, D = x_prompt.shape
    DB, TS, _ = x_sample.shape
    depth = w_in.shape[0]
    n_pages = page_table.shape[1]
    past_len = n_pages * PAGE_SIZE
    w_buf = cache_win_kv.shape[2]
    alpha = (2.0 * depth) ** 0.25
    assert w_buf == WINDOW and T % TQ == 0 and past_len % KT == 0

    cos_p, sin_p = _rope_tables(jnp.arange(T, dtype=jnp.int32))
    cos_s, sin_s = _rope_tables(past_len + jnp.arange(TS, dtype=jnp.int32))
    ovl, bexp_p, gexp = _const_tables(T)
    _, bexp_s, _ = _const_tables(past_len + KT)

    hp = x_prompt.reshape(B * T, D)
    hs = x_sample.reshape(DB * TS, D)
    tm_s = _tile_size(DB * TS, 1024)
    cos_s, sin_s = (jnp.tile(t, (tm_s // TS, 1)) for t in (cos_s, sin_s))
    outs = {k: [] for k in ("cmp_p", "slc_p", "win_p", "gla_p", "cmp_s", "slc_s", "win_s", "gla_s")}

    for l in range(depth):
        wf1i, wf1o = w_ffn1_in[l].astype(BF16), w_ffn1_out[l].astype(BF16)
        wf2i, wf2o = w_ffn2_in[l].astype(BF16), w_ffn2_out[l].astype(BF16)
        wi = _reorder_w_in(w_in[l]).astype(BF16)
        wo_g, wo_n = w_out[l][:GLA_VW].astype(BF16), w_out[l][GLA_VW:].astype(BF16)
        lg, lb = ln_g[l][:, None, :], ln_b[l][:, None, :]
        w2h = jnp.zeros((LANES, GLA_KW), F32).at[:GLA_GATE_RANK].set(w_gla_gate2[l]).astype(BF16)
        b2h = b_gla_gate2[l][None, :]
        gn = gla_norm_g[l][None, :]
        pos2 = jnp.tile(cmp_pos[l], (1, 1, LANES // NSA_HD))
        w1bd = _block_diag2(cmp_w1[l].reshape(2, CMP_LEN, NSA_HD, -1)).astype(BF16)
        w2bd = _block_diag2(cmp_w2[l]).astype(BF16)

        h1, h1b = _ffn_ln(hp, wf1i, wf1o, lg[0], lb[0], alpha, _tile_size(B * T, 512), 512)
        proj = _in_proj(h1b, wi, cos_p, sin_p, _tile_size(T, 1024))
        og, gla_state = _gla(proj, w2h, b2h, gn, jnp.zeros((B, GLA_HEADS, GLA_DK, GLA_DV), F32), B, T)
        n_grp = T // CMP_STRIDE
        n_cmp = (T - CMP_LEN) // CMP_STRIDE + 1
        kc, vc = _compress([proj], [pl.BlockSpec((T, ROWW), lambda b: (b, C_CMP // ROWW))],
                           pos2, w1bd, w2bd, B, n_grp, n_cmp)
        kc, vc = (jnp.pad(a, ((0, 0), (0, NBLK - n_grp), (0, 0))) for a in (kc, vc))
        on = _nsa_prompt(proj, kc, vc, ovl, bexp_p, gexp, B, T)
        h2, h2b = _out_ln(og, on, wo_g, wo_n, h1, lg[1], lb[1], alpha, _tile_size(B * T, 512))
        hp, _ = _ffn_ln(h2, wf2i, wf2o, lg[2], lb[2], alpha, _tile_size(B * T, 512), 512)
        row6 = lambda a, n, t: a.reshape(n, t, 2, NSA_KV, NSA_HD)
        outs["cmp_p"].append(row6(proj[:, C_CMP:C_CMP + ROWW], B, T))
        outs["slc_p"].append(row6(proj[:, C_SLC:C_SLC + ROWW], B, T))
        outs["win_p"].append(row6(proj[:, C_WIN:C_WIN + ROWW], B, T)[:, T - w_buf:])
        outs["gla_p"].append(gla_state)

        s1, s1b = _ffn_ln(hs, wf1i, wf1o, lg[0], lb[0], alpha, _tile_size(DB * TS, 512), 512)
        sproj = _in_proj(s1b, wi, cos_s, sin_s, tm_s)
        sog, sgla_state = _gla(sproj, w2h, b2h, gn, state_gla[l], DB, TS)
        feature_major = lambda c: c.transpose(0, 2, 3, 4, 1).reshape(c.shape[0] * ROWW, c.shape[1])
        cmp_pool_t = feature_major(cache_cmp_kv[l])
        slc_pool_t = feature_major(cache_slc_kv[l])
        win_cache_t = feature_major(cache_win_kv[l])
        n_grp_s = past_len // CMP_STRIDE
        n_cmp_s = (past_len + TS - CMP_LEN) // CMP_STRIDE + 1
        page_specs = [pl.BlockSpec((ROWW, PAGE_SIZE), lambda b, pt, p=p: (pt[b, p], 0)) for p in range(n_pages)]
        skc, svc = _compress([cmp_pool_t] * n_pages, page_specs, pos2, w1bd, w2bd, DB, n_grp_s, n_cmp_s,
                             scalar_args=(page_table,), feature_major=True)
        skc, svc = (jnp.pad(a, ((0, 0), (0, NBLK - n_grp_s), (0, 0))) for a in (skc, svc))
        son = _nsa_sample(sproj, skc, svc, slc_pool_t, win_cache_t, page_table, ovl, bexp_s, gexp, DB, TS, past_len)
        s2, s2b = _out_ln(sog, son, wo_g, wo_n, s1, lg[1], lb[1], alpha, _tile_size(DB * TS, 512))
        hs, _ = _ffn_ln(s2, wf2i, wf2o, lg[2], lb[2], alpha, _tile_size(DB * TS, 512), 512)
        new_win = row6(sproj[:, C_WIN:C_WIN + ROWW], DB, TS)
        outs["cmp_s"].append(row6(sproj[:, C_CMP:C_CMP + ROWW], DB, TS))
        outs["slc_s"].append(row6(sproj[:, C_SLC:C_SLC + ROWW], DB, TS))
        outs["win_s"].append(jnp.concatenate([cache_win_kv[l], new_win], axis=1)[:, -w_buf:])
        outs["gla_s"].append(sgla_state)

    st = lambda k: jnp.stack(outs[k])
    return (hp.reshape(B, T, D), hs.reshape(DB, TS, D),
            st("cmp_p"), st("slc_p"), st("win_p"), st("gla_p"),
            st("cmp_s"), st("slc_s"), st("win_s"), st("gla_s"))
```

```python
import functools
import math

import numpy as np
import jax
import jax.numpy as jnp
from jax import lax
from jax.experimental import pallas as pl
from jax.experimental.pallas import tpu as pltpu

F32 = jnp.float32
BF16 = jnp.bfloat16

GLA_HEADS = 4
GLA_DK = 128
GLA_DV = 256
GLA_GATE_RANK = 16
GLA_TAU = 16.0
NSA_HEADS = 16
NSA_KV = 4
NSA_GROUP = NSA_HEADS // NSA_KV
NSA_HD = 64
CMP_LEN = 32
CMP_STRIDE = 16
SEL_LEN = 64
SEL_TOPN = 16
WINDOW = 512
PAGE_SIZE = 128
ROPE_THETA = 10000.0
LN_EPS = 1e-5
NEG = -1e30

LANES = 128
SUBLANES = 8
VMEM_LIMIT_BYTES = 56 * 1024 * 1024

GLA_KW = GLA_HEADS * GLA_DK
GLA_VW = GLA_HEADS * GLA_DV
NSA_QW = NSA_HEADS * NSA_HD
NSA_KVW = NSA_KV * NSA_HD
ROWW = 2 * NSA_KVW
C_GQ = 0
C_GK = C_GQ + GLA_KW
C_GV = C_GK + GLA_KW
C_GR = C_GV + GLA_VW
C_NQ = C_GR + GLA_VW
C_CMP = C_NQ + NSA_QW
C_SLC = C_CMP + ROWW
C_WIN = C_SLC + ROWW
C_TAIL = C_WIN + ROWW
PROJ_TN = 512
PROJ_W = C_TAIL + PROJ_TN
GATE_LANE0 = GLA_GATE_RANK

GLA_CHUNK = 128
GLA_SUB = 32
TQ = 128
KT = 128
FKT = 256
NBLK = 128


def _cparams(sem):
    return pltpu.CompilerParams(dimension_semantics=sem, vmem_limit_bytes=VMEM_LIMIT_BYTES)


def _dot(a, b):
    return jnp.dot(a, b, preferred_element_type=F32)


def _dot_nt(a, b):
    return lax.dot_general(a, b, (((1,), (1,)), ((), ())), preferred_element_type=F32)


def _layer_norm(y, g, b):
    mu = jnp.mean(y, -1, keepdims=True)
    d = y - mu
    var = jnp.mean(d * d, -1, keepdims=True)
    return d * lax.rsqrt(var + LN_EPS) * g + b


def _split_bf16(x, n):
    parts = []
    for _ in range(n - 1):
        hi = x.astype(BF16)
        parts.append(hi)
        x = x - hi.astype(F32)
    parts.append(x.astype(BF16))
    return parts


def _ffn_ln_kernel(x_ref, wg_ref, wu_ref, wo_ref, g_ref, b_ref, y_ref, yb_ref, xb_ref, acc_ref, *, alpha):
    j = pl.program_id(1)

    @pl.when(j == 0)
    def _():
        xb_ref[...] = x_ref[...].astype(BF16)
        acc_ref[...] = jnp.zeros_like(acc_ref)

    xb = xb_ref[...]
    h = _dot(xb, wg_ref[...])
    u = _dot(xb, wu_ref[...])
    a = (h * jax.nn.sigmoid(h) * u).astype(BF16)
    acc_ref[...] += _dot(a, wo_ref[...])

    @pl.when(j == pl.num_programs(1) - 1)
    def _():
        y = _layer_norm(alpha * x_ref[...] + 0.5 * acc_ref[...], g_ref[...], b_ref[...])
        y_ref[...] = y
        yb_ref[...] = y.astype(BF16)


def _ffn_ln(x, w_in, w_out, g, b, alpha, tm, tf):
    n, d = x.shape
    dff = w_out.shape[0]
    nf = dff // tf
    assert n % tm == 0 and dff % tf == 0
    return pl.pallas_call(
        functools.partial(_ffn_ln_kernel, alpha=alpha),
        grid=(n // tm, nf),
        in_specs=[
            pl.BlockSpec((tm, d), lambda i, j: (i, 0)),
            pl.BlockSpec((d, tf), lambda i, j: (0, j)),
            pl.BlockSpec((d, tf), lambda i, j: (0, j + nf)),
            pl.BlockSpec((tf, d), lambda i, j: (j, 0)),
            pl.BlockSpec((1, d), lambda i, j: (0, 0)),
            pl.BlockSpec((1, d), lambda i, j: (0, 0)),
        ],
        out_specs=[pl.BlockSpec((tm, d), lambda i, j: (i, 0)),
                   pl.BlockSpec((tm, d), lambda i, j: (i, 0))],
        out_shape=[jax.ShapeDtypeStruct((n, d), F32), jax.ShapeDtypeStruct((n, d), BF16)],
        scratch_shapes=[pltpu.VMEM((tm, d), BF16), pltpu.VMEM((tm, d), F32)],
        compiler_params=_cparams(("parallel", "arbitrary")),
        name="ffn_ln",
    )(x, w_in, w_in, w_out, g, b)


def _rope_lanes(x, cos, sin_signed):
    lane = lax.broadcasted_iota(jnp.int32, x.shape, 1)
    first_half = (lane & (NSA_HD - 1)) < (NSA_HD // 2)
    partner = jnp.where(first_half, pltpu.roll(x, LANES - NSA_HD // 2, 1), pltpu.roll(x, NSA_HD // 2, 1))
    return x * cos + partner * sin_signed


def _is_rotary_col(col):
    if C_NQ <= col < C_CMP:
        return True
    return C_CMP <= col < C_TAIL and (col - C_CMP) % ROWW < NSA_KVW


def _in_proj_kernel(x_ref, w_ref, cos_ref, sin_ref, o_ref, *t_refs, tn):
    j = pl.program_id(1)
    chunk = 2 * LANES

    for jj in range(PROJ_W // tn):
        @pl.when(j == jj)
        def _(jj=jj):
            for c0 in range(0, tn, chunk):
                acc = _dot(x_ref[...], w_ref[:, c0:c0 + chunk])
                for s0 in range(0, chunk, LANES):
                    col = jj * tn + c0 + s0
                    val = acc[:, s0:s0 + LANES]
                    if _is_rotary_col(col):
                        val = _rope_lanes(val, cos_ref[...], sin_ref[...])
                    o_ref[:, c0 + s0:c0 + s0 + LANES] = val
                    group, off = divmod(col - C_CMP, ROWW)
                    if t_refs and 0 <= group < len(t_refs):
                        t_refs[group][0, off:off + LANES, :] = val.T


def _in_proj(xb, w, cos, sin, tm, tn, emit_t):
    n, d = xb.shape
    period = cos.shape[0]
    assert n % tm == 0 and period % tm == 0 and PROJ_W % tn == 0 and tn % ROWW == 0
    npb = period // tm
    out_specs = [pl.BlockSpec((tm, tn), lambda i, j: (i, j))]
    out_shape = [jax.ShapeDtypeStruct((n, PROJ_W), F32)]
    if emit_t:
        out_specs += [pl.BlockSpec((1, ROWW, tm), lambda i, j: (i // npb, 0, i % npb))] * 3
        out_shape += [jax.ShapeDtypeStruct((n // period, ROWW, period), F32)] * 3
    return pl.pallas_call(
        functools.partial(_in_proj_kernel, tn=tn),
        grid=(n // tm, PROJ_W // tn),
        in_specs=[
            pl.BlockSpec((tm, d), lambda i, j: (i, 0)),
            pl.BlockSpec((d, tn), lambda i, j: (0, j)),
            pl.BlockSpec((tm, LANES), lambda i, j: (i % npb, 0)),
            pl.BlockSpec((tm, LANES), lambda i, j: (i % npb, 0)),
        ],
        out_specs=out_specs,
        out_shape=out_shape,
        compiler_params=_cparams(("parallel", "arbitrary")),
        name="in_proj",
    )(xb, w, cos, sin)


def _gla_kernel(q_ref, k_ref, v_ref, r_ref, ga_ref, w2_ref, b2_ref, gn_ref, s0_ref,
                o_ref, sout_ref, st_ref, *, n_valid, n_chunks, n_sb):
    t = pl.program_id(1)
    C, S, H = GLA_CHUNK, GLA_SUB, GLA_HEADS

    @pl.when(t == 0)
    def _():
        for sb in range(n_sb):
            for h in range(H):
                st_ref[sb * H + h] = s0_ref[sb, h].T

    row_k = lax.broadcasted_iota(jnp.int32, (C, GLA_DK), 0)
    row_a = lax.broadcasted_iota(jnp.int32, (C, GLA_KW), 0)
    row_c = lax.broadcasted_iota(jnp.int32, (C, C), 0)
    col_c = lax.broadcasted_iota(jnp.int32, (C, C), 1)
    causal = row_c >= col_c
    tri = causal.astype(BF16)

    def load(ref, r0, sb, c0, w):
        if n_valid >= C:
            return ref[pl.ds(r0, C), c0:c0 + w]
        x = ref[sb * n_valid:(sb + 1) * n_valid, c0:c0 + w]
        return jnp.concatenate([x, jnp.zeros((C - n_valid, w), x.dtype)], axis=0)

    def chunk(c, carry):
        r0 = pl.multiple_of(c * C, C)
        for sb in range(n_sb):
            pre = _dot(load(ga_ref, r0, sb, 0, LANES).astype(BF16), w2_ref[...]) + b2_ref[...]
            la = (jnp.minimum(pre, 0.0) - jnp.log(1.0 + jnp.exp(-jnp.abs(pre)))) * (1.0 / GLA_TAU)
            if n_valid < C:
                la = jnp.where(row_a < n_valid, la, 0.0)
            la_hi, la_lo = _split_bf16(la, 2)
            b_all = _dot(tri, la_hi) + _dot(tri, la_lo)
            for h in range(H):
                b = b_all[:, h * GLA_DK:(h + 1) * GLA_DK]
                q = load(q_ref, r0, sb, h * GLA_DK, GLA_DK) * (GLA_DK ** -0.5)
                k = load(k_ref, r0, sb, h * GLA_DK, GLA_DK)
                v = load(v_ref, r0, sb, h * GLA_DV, GLA_DV)
                b_last = b[C - 1:C, :]

                strips = []
                for i in range(C // S):
                    rows = slice(i * S, (i + 1) * S)
                    anchor = jnp.zeros((1, GLA_DK), F32) if i == 0 else b[i * S - 1:i * S, :]
                    qi = q[rows] * jnp.exp(b[rows] - anchor)
                    ki = k * jnp.exp(jnp.where(row_k < (i + 1) * S, anchor - b, 0.0))
                    strips.append(_dot_nt(qi.astype(BF16), ki.astype(BF16)))
                attn = jnp.where(causal, jnp.concatenate(strips, axis=0), 0.0)

                st = st_ref[sb * H + h]
                o = (_dot_nt((q * jnp.exp(b)).astype(BF16), st.astype(BF16))
                     + _dot(attn.astype(BF16), v.astype(BF16)))
                ke = k * jnp.exp(b_last - b)
                st_ref[sb * H + h] = st * jnp.exp(b_last) + _dot(v.T.astype(BF16), ke.astype(BF16))

                rms = lax.rsqrt(jnp.mean(o * o, -1, keepdims=True) + LN_EPS)
                r = load(r_ref, r0, sb, h * GLA_DV, GLA_DV)
                out = (o * rms * gn_ref[...] * (r * jax.nn.sigmoid(r))).astype(o_ref.dtype)
                cols = slice(h * GLA_DV, (h + 1) * GLA_DV)
                if n_valid >= C:
                    o_ref[pl.ds(r0, C), cols] = out
                else:
                    o_ref[sb * n_valid:(sb + 1) * n_valid, cols] = out[:n_valid]
        return carry

    lax.fori_loop(0, n_chunks, chunk, 0)

    @pl.when(t == pl.num_programs(1) - 1)
    def _():
        for sb in range(n_sb):
            for h in range(H):
                sout_ref[sb, h] = st_ref[sb * H + h].T


def _gla(proj, w2h, b2h, gn, s0, n_seq, t_seq):
    C, H = GLA_CHUNK, GLA_HEADS
    if t_seq >= C:
        tb = math.gcd(t_seq, 1024)
        assert tb % C == 0
        n_valid, n_chunks, n_sb = C, tb // C, 1
    else:
        tb, n_valid, n_chunks = t_seq, t_seq, 1
        n_sb = math.gcd(n_seq, 2)
        assert t_seq % SUBLANES == 0
    nt = t_seq // tb
    rows = n_sb * tb
    rowblk = lambda b, t: b * nt + t
    return pl.pallas_call(
        functools.partial(_gla_kernel, n_valid=n_valid, n_chunks=n_chunks, n_sb=n_sb),
        grid=(n_seq // n_sb, nt),
        in_specs=[
            pl.BlockSpec((rows, GLA_KW), lambda b, t: (rowblk(b, t), C_GQ // GLA_KW)),
            pl.BlockSpec((rows, GLA_KW), lambda b, t: (rowblk(b, t), C_GK // GLA_KW)),
            pl.BlockSpec((rows, GLA_VW), lambda b, t: (rowblk(b, t), C_GV // GLA_VW)),
            pl.BlockSpec((rows, GLA_VW), lambda b, t: (rowblk(b, t), C_GR // GLA_VW)),
            pl.BlockSpec((rows, LANES), lambda b, t: (rowblk(b, t), C_TAIL // LANES)),
            pl.BlockSpec((LANES, GLA_KW), lambda b, t: (0, 0)),
            pl.BlockSpec((1, GLA_KW), lambda b, t: (0, 0)),
            pl.BlockSpec((1, GLA_DV), lambda b, t: (0, 0)),
            pl.BlockSpec((n_sb, H, GLA_DK, GLA_DV), lambda b, t: (b, 0, 0, 0)),
        ],
        out_specs=[
            pl.BlockSpec((rows, GLA_VW), lambda b, t: (rowblk(b, t), 0)),
            pl.BlockSpec((n_sb, H, GLA_DK, GLA_DV), lambda b, t: (b, 0, 0, 0)),
        ],
        out_shape=[jax.ShapeDtypeStruct((n_seq * t_seq, GLA_VW), BF16 if t_seq >= C else F32),
                   jax.ShapeDtypeStruct((n_seq, H, GLA_DK, GLA_DV), F32)],
        scratch_shapes=[pltpu.VMEM((n_sb * H, GLA_DV, GLA_DK), F32)],
        compiler_params=_cparams(("parallel", "arbitrary")),
        name="gla",
    )(proj, proj, proj, proj, proj, w2h, b2h, gn, s0)


def _gelu_tanh(x):
    return 0.5 * x * (1.0 + jnp.tanh(math.sqrt(2.0 / math.pi) * (x + 0.044715 * (x * x * x))))


def _compress_kernel(*refs, n_scalar, n_src, n_cmp, feature_major):
    refs = refs[n_scalar:]
    src = refs[:n_src]
    pos_ref, w1_ref, w2_ref, kc_ref, vc_ref, slab_ref, u_ref = refs[n_src:]
    r_src = src[0].shape[1] if feature_major else src[0].shape[0]
    n_grp = n_src * r_src // CMP_STRIDE
    row = lax.broadcasted_iota(jnp.int32, (n_grp, LANES), 0)
    outs = (kc_ref, vc_ref)
    hw = 2 * LANES
    for part in range(2):
        pos_bias = _dot(pos_ref[part], w1_ref[part])
        for pair in range(NSA_KVW // LANES):
            c0 = part * NSA_KVW + pair * LANES
            for n, s in enumerate(src):
                slab_ref[n * r_src:(n + 1) * r_src, :] = s[c0:c0 + LANES, :].T if feature_major else s[:, c0:c0 + LANES]
            for l in range(CMP_STRIDE):
                u_ref[:, l * LANES:(l + 1) * LANES] = slab_ref[pl.ds(l, n_grp, stride=CMP_STRIDE), :].astype(BF16)
            ab = _dot(u_ref[...], w1_ref[part])
            hidden = []
            for hh in range(2):
                a = ab[:, hh * hw:hh * hw + LANES] + pos_bias[0:1, hh * hw:hh * hw + LANES]
                b = ab[:, hh * hw + LANES:(hh + 1) * hw] + pos_bias[SUBLANES:SUBLANES + 1, hh * hw + LANES:(hh + 1) * hw]
                hidden.append(a + pltpu.roll(b, n_grp - 1, 0))
            out = _dot(_gelu_tanh(jnp.concatenate(hidden, axis=1)).astype(BF16), w2_ref[part])
            outs[part][0, :, pair * LANES:(pair + 1) * LANES] = jnp.where(row < n_cmp, out, 0.0)


def _compress(srcs, src_specs, pos2, w1bd, w2bd, n_seq, n_grp, n_cmp, scalar_args=(), feature_major=False):
    n_src = len(srcs)
    nsp = len(scalar_args)
    in_specs = list(src_specs) + [
        pl.BlockSpec(pos2.shape, lambda b, *_: (0, 0, 0)),
        pl.BlockSpec(w1bd.shape, lambda b, *_: (0, 0, 0)),
        pl.BlockSpec(w2bd.shape, lambda b, *_: (0, 0, 0)),
    ]
    out_spec = pl.BlockSpec((1, n_grp, NSA_KVW), lambda b, *_: (b, 0, 0))
    return pl.pallas_call(
        functools.partial(_compress_kernel, n_scalar=nsp, n_src=n_src, n_cmp=n_cmp, feature_major=feature_major),
        grid_spec=pltpu.PrefetchScalarGridSpec(
            num_scalar_prefetch=nsp, grid=(n_seq,), in_specs=in_specs, out_specs=[out_spec, out_spec],
            scratch_shapes=[pltpu.VMEM((n_grp * CMP_STRIDE, LANES), F32),
                            pltpu.VMEM((n_grp, CMP_STRIDE * LANES), BF16)]),
        out_shape=[jax.ShapeDtypeStruct((n_seq, n_grp, NSA_KVW), F32)] * 2,
        compiler_params=_cparams(("parallel",)),
        name="compress",
    )(*scalar_args, *srcs, pos2, w1bd, w2bd)


def _build_q(q_ref, qs_ref, tq):
    lane = lax.broadcasted_iota(jnp.int32, (tq, LANES), 1)
    for j in range(NSA_HEADS):
        h = j // NSA_GROUP
        x = q_ref[:, (j // 2) * LANES:(j // 2 + 1) * LANES] * (NSA_HD ** -0.5)
        if (j % 2) != (h % 2):
            x = pltpu.roll(x, NSA_HD, 1)
        keep = (lane >= NSA_HD) if (h % 2) else (lane < NSA_HD)
        qs_ref[j * tq:(j + 1) * tq, :] = jnp.where(keep, x, 0.0).astype(qs_ref.dtype)


def _pair_slab(h):
    return slice((h // 2) * LANES, (h // 2 + 1) * LANES)


def _tile_rows(x, n):
    return jnp.concatenate([x] * n, axis=0)


def _cmp_attention(qh, kc, vc, qpos, n_cmp):
    tq = qpos.shape[0]
    blk = lax.broadcasted_iota(jnp.int32, (tq, NBLK), 1)
    valid = ((blk * CMP_STRIDE + (CMP_LEN - 1) <= qpos) & (blk < n_cmp)).astype(F32)
    valid4 = _tile_rows(valid, NSA_GROUP)
    s = jnp.where(valid4 > 0.5, _dot_nt(qh, kc.astype(BF16)), NEG)
    e = jnp.exp(s - jnp.max(s, -1, keepdims=True))
    p = e / jnp.sum(e, -1, keepdims=True) * valid4
    o = _dot(p.astype(BF16), vc.astype(BF16))
    psum = p[0:tq]
    for g in range(1, NSA_GROUP):
        psum = psum + p[g * tq:(g + 1) * tq]
    return o, psum


def _select_blocks(psum, ovl, qpos, n_sel):
    imp = sum(_dot(t, ovl) for t in _split_bf16(psum, 3))
    blk = lax.broadcasted_iota(jnp.int32, psum.shape, 1)
    ss = blk * SEL_LEN
    cur = qpos - (qpos & (SEL_LEN - 1))
    forced = (ss == 0) | (ss == cur) | (ss == cur - SEL_LEN)
    score = jnp.where(ss <= qpos, jnp.where(forced, 1e9, imp), -1e9)
    nb = -(-n_sel // SUBLANES) * SUBLANES
    st = score.T[0:nb]
    sub = lax.broadcasted_iota(jnp.int32, st.shape, 0)
    rank = jnp.zeros(st.shape, F32)
    for s in range(n_sel):
        other = st[s:s + 1, :]
        beats = (other > st) | ((other == st) & (sub > s))
        rank = rank + beats.astype(F32)
    sel_t = ((rank < SEL_TOPN) & (st > -1e8)).astype(F32)
    sel_t = jnp.concatenate([sel_t, jnp.zeros((NBLK - nb, st.shape[1]), F32)], axis=0)
    return sel_t.T


def _emit_branch(out_ref, o, gate, h, tq, first):
    lane = lax.broadcasted_iota(jnp.int32, (tq, LANES), 1)
    for gp in range(NSA_GROUP // 2):
        y0 = o[(2 * gp) * tq:(2 * gp + 1) * tq]
        y1 = o[(2 * gp + 1) * tq:(2 * gp + 2) * tq]
        if h % 2:
            y0 = pltpu.roll(y0, NSA_HD, 1)
        else:
            y1 = pltpu.roll(y1, NSA_HD, 1)
        m = 2 * h + gp
        sl = slice(m * LANES, (m + 1) * LANES)
        val = jnp.where(lane < NSA_HD, y0, y1) * gate[:, sl]
        if first:
            out_ref[:, sl] = val
        else:
            out_ref[:, sl] += val


def _expand_gates(tail, gexp_ref):
    g_hi, g_lo = _split_bf16(jax.nn.sigmoid(tail), 2)
    return [_dot(g_hi, gexp_ref[c]) + _dot(g_lo, gexp_ref[c]) for c in range(3)]


def _own_half(h, shape):
    lane = lax.broadcasted_iota(jnp.int32, shape, len(shape) - 1)
    return (lane >= NSA_HD) if (h % 2) else (lane < NSA_HD)


def _nsa_prompt_kernel(q_ref, slc_ref, win_ref, tail_ref, kc_ref, vc_ref, ovl_ref, bexp_ref, gexp_ref,
                       o_ref, qs_ref, mexp_ref, m_ref, acc_ref, out_ref, *, n_cmp, n_sel):
    i = pl.program_id(1)
    q0 = i * TQ
    G = NSA_GROUP
    R = G * TQ
    _build_q(q_ref, qs_ref, TQ)
    gates = _expand_gates(tail_ref[:, 0:LANES], gexp_ref)
    qpos = q0 + lax.broadcasted_iota(jnp.int32, (TQ, NBLK), 0)
    qpos_k = q0 + lax.broadcasted_iota(jnp.int32, (TQ, FKT), 0)
    lane_k = lax.broadcasted_iota(jnp.int32, (TQ, FKT), 1)
    n_kt = mexp_ref.shape[1]

    for h in range(NSA_KV):
        qh = qs_ref[h * R:(h + 1) * R, :]
        o_c, psum = _cmp_attention(qh, kc_ref[0, :, _pair_slab(h)], vc_ref[0, :, _pair_slab(h)], qpos, n_cmp)
        _emit_branch(out_ref, o_c, gates[0], h, TQ, True)
        sel = _select_blocks(psum, ovl_ref[...], qpos, n_sel)
        mexp = _dot(sel.astype(BF16), bexp_ref[...])
        for kt in range(n_kt):
            mexp_ref[h, kt] = mexp[:, kt * FKT:(kt + 1) * FKT]

    def flash(src_ref, lo, hi, bias_fn):
        m_ref[...] = jnp.full(m_ref.shape, NEG, F32)
        acc_ref[...] = jnp.zeros(acc_ref.shape, F32)

        def body(kt, carry):
            k0 = pl.multiple_of(kt * FKT, FKT)
            kpos = k0 + lane_k
            kslab = [src_ref[pl.ds(k0, FKT), m * LANES:(m + 1) * LANES].astype(BF16) for m in range(2)]
            vslab = [src_ref[pl.ds(k0, FKT), NSA_KVW + m * LANES:NSA_KVW + (m + 1) * LANES] for m in range(2)]
            for h in range(NSA_KV):
                vv = jnp.where(_own_half(h, (FKT, LANES)), vslab[h // 2], 1.0).astype(BF16)
                s = _dot_nt(qs_ref[h * R:(h + 1) * R, :], kslab[h // 2]) + _tile_rows(bias_fn(h, kt, kpos), G)
                m_prev = m_ref[h]
                smax = s[:, 0:LANES]
                for c in range(1, FKT // LANES):
                    smax = jnp.maximum(smax, s[:, c * LANES:(c + 1) * LANES])
                m_new = jnp.maximum(m_prev, jnp.max(smax, -1, keepdims=True))
                alpha = jnp.exp(m_prev - m_new)
                p = jnp.exp(s - jnp.concatenate([m_new] * (FKT // LANES), axis=1))
                acc_ref[h] = alpha * acc_ref[h] + _dot(p.astype(BF16), vv)
                m_ref[h] = m_new
            return carry

        lax.fori_loop(lo, hi, body, 0)
        outs = []
        for h in range(NSA_KV):
            acc = acc_ref[h]
            outs.append(acc / pltpu.roll(acc, NSA_HD, 1))
        return outs

    def sel_bias(h, kt, kpos):
        return jnp.where((mexp_ref[h, kt] > 0.5) & (kpos <= qpos_k), 0.0, NEG)

    def win_bias(h, kt, kpos):
        dist = qpos_k - kpos
        return jnp.where((dist >= 0) & (dist < WINDOW), 0.0, NEG)

    hi = (q0 + TQ + FKT - 1) // FKT
    for h, o_s in enumerate(flash(slc_ref, 0, hi, sel_bias)):
        _emit_branch(out_ref, o_s, gates[1], h, TQ, False)
    lo = jnp.maximum(q0 - WINDOW, 0) // FKT
    for h, o_w in enumerate(flash(win_ref, lo, hi, win_bias)):
        _emit_branch(out_ref, o_w, gates[2], h, TQ, False)
    o_ref[...] = out_ref[...].astype(o_ref.dtype)


def _nsa_prompt(proj, kc, vc, ovl, bexp, gexp, n_seq, t_seq):
    assert t_seq % FKT == 0 and TQ == NBLK == LANES
    nq = t_seq // TQ
    n_cmp = (t_seq - CMP_LEN) // CMP_STRIDE + 1
    n_sel = -(-t_seq // SEL_LEN)
    G = NSA_GROUP
    return pl.pallas_call(
        functools.partial(_nsa_prompt_kernel, n_cmp=n_cmp, n_sel=n_sel),
        grid=(n_seq, nq),
        in_specs=[
            pl.BlockSpec((TQ, NSA_QW), lambda b, i: (b * nq + i, C_NQ // NSA_QW)),
            pl.BlockSpec((t_seq, ROWW), lambda b, i: (b, C_SLC // ROWW)),
            pl.BlockSpec((t_seq, ROWW), lambda b, i: (b, C_WIN // ROWW)),
            pl.BlockSpec((TQ, PROJ_TN), lambda b, i: (b * nq + i, C_TAIL // PROJ_TN)),
            pl.BlockSpec((1, NBLK, NSA_KVW), lambda b, i: (b, 0, 0)),
            pl.BlockSpec((1, NBLK, NSA_KVW), lambda b, i: (b, 0, 0)),
            pl.BlockSpec(ovl.shape, lambda b, i: (0, 0)),
            pl.BlockSpec(bexp.shape, lambda b, i: (0, 0)),
            pl.BlockSpec(gexp.shape, lambda b, i: (0, 0, 0)),
        ],
        out_specs=pl.BlockSpec((TQ, NSA_QW), lambda b, i: (b * nq + i, 0)),
        out_shape=jax.ShapeDtypeStruct((n_seq * t_seq, NSA_QW), BF16),
        scratch_shapes=[
            pltpu.VMEM((NSA_HEADS * TQ, LANES), BF16),
            pltpu.VMEM((NSA_KV, t_seq // FKT, TQ, FKT), F32),
            pltpu.VMEM((NSA_KV, G * TQ, LANES), F32),
            pltpu.VMEM((NSA_KV, G * TQ, LANES), F32),
            pltpu.VMEM((TQ, NSA_QW), F32),
        ],
        compiler_params=_cparams(("parallel", "arbitrary")),
        name="nsa_prompt",
    )(proj, proj, proj, proj, kc, vc, ovl, bexp, gexp)


def _nsa_sample_kernel(*refs, n_pages, n_wt, t_new, past_len, n_cmp, n_sel):
    q_ref, snew_ref, wnew_ref, tail_ref, kc_ref, vc_ref, wcache_ref = refs[1:8]
    pages = refs[8:8 + n_pages]
    ovl_ref, bexp_ref, gexp_ref, o_ref, qs_ref, out_ref = refs[8 + n_pages:]
    G = NSA_GROUP
    R = NSA_HEADS * t_new
    half = R // 2
    _build_q(q_ref, qs_ref, t_new)
    gates = _expand_gates(tail_ref[:, 0:LANES], gexp_ref)
    zeros = jnp.zeros((half, LANES), F32)
    q_all = jnp.concatenate([jnp.concatenate([qs_ref[0:half, :], zeros], axis=1),
                             jnp.concatenate([zeros, qs_ref[half:R, :]], axis=1)], axis=0).astype(BF16)
    row_t = lax.broadcasted_iota(jnp.int32, (TQ, NBLK), 0) & (t_new - 1)
    qpos_blk = past_len + row_t
    pad_new = lambda x: jnp.concatenate([x, jnp.zeros((KT - t_new, x.shape[1]), x.dtype)], axis=0)

    def attend(k_t_tiles, v_t_tiles, k_new, v_new, bias):
        s = [_dot(q_all, kt.astype(BF16)) for kt in k_t_tiles] + [_dot_nt(q_all, pad_new(k_new).astype(BF16))]
        s = jnp.concatenate(s, axis=1) + bias
        e = jnp.exp(s - jnp.max(s, -1, keepdims=True))
        p = (e / jnp.sum(e, -1, keepdims=True)).astype(BF16)
        n = len(k_t_tiles)
        o = _dot(p[:, n * KT:(n + 1) * KT], pad_new(v_new).astype(BF16))
        for c, vt in enumerate(v_t_tiles):
            o = o + _dot_nt(p[:, c * KT:(c + 1) * KT], vt.astype(BF16))
        return o

    def emit(o_all, gate, first):
        for h in range(NSA_KV):
            _emit_branch(out_ref, o_all[h * G * t_new:(h + 1) * G * t_new, _pair_slab(h)], gate, h, t_new, first)

    qpos_r = qpos_blk[0:R]
    blk = lax.broadcasted_iota(jnp.int32, (R, NBLK), 1)
    valid = ((blk * CMP_STRIDE + (CMP_LEN - 1) <= qpos_r) & (blk < n_cmp)).astype(F32)
    s = jnp.where(valid > 0.5, _dot_nt(q_all, kc_ref[0].astype(BF16)), NEG)
    e = jnp.exp(s - jnp.max(s, -1, keepdims=True))
    p = e / jnp.sum(e, -1, keepdims=True) * valid
    emit(_dot(p.astype(BF16), vc_ref[0].astype(BF16)), gates[0], True)

    psum = []
    for h in range(NSA_KV):
        acc = p[(h * G) * t_new:(h * G + 1) * t_new]
        for g in range(1, G):
            acc = acc + p[(h * G + g) * t_new:(h * G + g + 1) * t_new]
        psum.append(acc)
    psum = jnp.concatenate(psum + [jnp.zeros((TQ - NSA_KV * t_new, NBLK), F32)], axis=0)
    sel = _select_blocks(psum, ovl_ref[...], qpos_blk, n_sel)
    mexp = _dot(sel.astype(BF16), bexp_ref[...])
    mexp = jnp.concatenate([_tile_rows(mexp[h * t_new:(h + 1) * t_new], G) for h in range(NSA_KV)], axis=0)
    n_keys = (n_pages + 1) * KT
    kpos = lax.broadcasted_iota(jnp.int32, (R, n_keys), 1)
    qpos_k = past_len + (lax.broadcasted_iota(jnp.int32, (R, n_keys), 0) & (t_new - 1))
    bias = jnp.where((mexp > 0.5) & (kpos <= qpos_k), 0.0, NEG)
    o_s = attend([pg[0:NSA_KVW, :] for pg in pages], [pg[NSA_KVW:ROWW, :] for pg in pages],
                 snew_ref[:, 0:NSA_KVW], snew_ref[:, NSA_KVW:ROWW], bias)
    emit(o_s, gates[1], False)

    n_keys = (n_wt + 1) * KT
    kpos = (past_len - n_wt * KT) + lax.broadcasted_iota(jnp.int32, (R, n_keys), 1)
    dist = past_len + (lax.broadcasted_iota(jnp.int32, (R, n_keys), 0) & (t_new - 1)) - kpos
    bias = jnp.where((dist >= 0) & (dist < WINDOW), 0.0, NEG)
    o_w = attend([wcache_ref[0:NSA_KVW, c * KT:(c + 1) * KT] for c in range(n_wt)],
                 [wcache_ref[NSA_KVW:ROWW, c * KT:(c + 1) * KT] for c in range(n_wt)],
                 wnew_ref[:, 0:NSA_KVW], wnew_ref[:, NSA_KVW:ROWW], bias)
    emit(o_w, gates[2], False)
    o_ref[...] = out_ref[...].astype(o_ref.dtype)


def _nsa_sample(proj, kc, vc, slc_pool_t, win_cache_t, page_table, ovl, bexp, gexp, n_seq, t_new, past_len):
    n_pages = page_table.shape[1]
    w_buf = win_cache_t.shape[1]
    assert w_buf % KT == 0 and t_new == SUBLANES and NSA_HEADS * t_new <= TQ and PAGE_SIZE == KT
    n_wt = w_buf // KT
    n_cmp = (past_len + t_new - CMP_LEN) // CMP_STRIDE + 1
    n_sel = -(-(past_len + t_new) // SEL_LEN)
    page_spec = lambda p: pl.BlockSpec((ROWW, PAGE_SIZE), lambda b, pt, p=p: (pt[b, p], 0))
    in_specs = [
        pl.BlockSpec((t_new, NSA_QW), lambda b, pt: (b, C_NQ // NSA_QW)),
        pl.BlockSpec((t_new, ROWW), lambda b, pt: (b, C_SLC // ROWW)),
        pl.BlockSpec((t_new, ROWW), lambda b, pt: (b, C_WIN // ROWW)),
        pl.BlockSpec((t_new, PROJ_TN), lambda b, pt: (b, C_TAIL // PROJ_TN)),
        pl.BlockSpec((1, NBLK, NSA_KVW), lambda b, pt: (b, 0, 0)),
        pl.BlockSpec((1, NBLK, NSA_KVW), lambda b, pt: (b, 0, 0)),
        pl.BlockSpec((ROWW, w_buf), lambda b, pt: (b, 0)),
    ] + [page_spec(p) for p in range(n_pages)] + [
        pl.BlockSpec(ovl.shape, lambda b, pt: (0, 0)),
        pl.BlockSpec(bexp.shape, lambda b, pt: (0, 0)),
        pl.BlockSpec(gexp.shape, lambda b, pt: (0, 0, 0)),
    ]
    return pl.pallas_call(
        functools.partial(_nsa_sample_kernel, n_pages=n_pages, n_wt=n_wt, t_new=t_new, past_len=past_len,
                          n_cmp=n_cmp, n_sel=n_sel),
        grid_spec=pltpu.PrefetchScalarGridSpec(
            num_scalar_prefetch=1, grid=(n_seq,), in_specs=in_specs,
            out_specs=pl.BlockSpec((t_new, NSA_QW), lambda b, pt: (b, 0)),
            scratch_shapes=[pltpu.VMEM((NSA_HEADS * t_new, LANES), F32), pltpu.VMEM((t_new, NSA_QW), F32)]),
        out_shape=jax.ShapeDtypeStruct((n_seq * t_new, NSA_QW), F32),
        compiler_params=_cparams(("parallel",)),
        name="nsa_sample",
    )(page_table, proj, proj, proj, proj, kc, vc, win_cache_t, *([slc_pool_t] * n_pages), ovl, bexp, gexp)


def _out_ln_kernel(og_ref, on_ref, wg_ref, wn_ref, x_ref, g_ref, b_ref, y_ref, yb_ref, *, alpha):
    tm = x_ref.shape[0]
    rows = math.gcd(tm, 256)
    for r0 in range(0, tm, rows):
        sl = slice(r0, r0 + rows)
        m = _dot(og_ref[sl, :].astype(BF16), wg_ref[...]) + _dot(on_ref[sl, :].astype(BF16), wn_ref[...])
        y = _layer_norm(alpha * x_ref[sl, :] + m, g_ref[...], b_ref[...])
        y_ref[sl, :] = y
        yb_ref[sl, :] = y.astype(BF16)


def _out_ln(og, on, wg, wn, x, g, b, alpha, tm):
    n, d = x.shape
    assert n % tm == 0
    return pl.pallas_call(
        functools.partial(_out_ln_kernel, alpha=alpha),
        grid=(n // tm,),
        in_specs=[
            pl.BlockSpec((tm, og.shape[1]), lambda i: (i, 0)),
            pl.BlockSpec((tm, on.shape[1]), lambda i: (i, 0)),
            pl.BlockSpec(wg.shape, lambda i: (0, 0)),
            pl.BlockSpec(wn.shape, lambda i: (0, 0)),
            pl.BlockSpec((tm, d), lambda i: (i, 0)),
            pl.BlockSpec((1, d), lambda i: (0, 0)),
            pl.BlockSpec((1, d), lambda i: (0, 0)),
        ],
        out_specs=[pl.BlockSpec((tm, d), lambda i: (i, 0)), pl.BlockSpec((tm, d), lambda i: (i, 0))],
        out_shape=[jax.ShapeDtypeStruct((n, d), F32), jax.ShapeDtypeStruct((n, d), BF16)],
        compiler_params=_cparams(("parallel",)),
        name="out_ln",
    )(og, on, wg, wn, x, g, b)


def _rope_tables(pos):
    half = NSA_HD // 2
    freq = ROPE_THETA ** (-jnp.arange(half, dtype=F32) / half)
    ang = pos.astype(F32)[:, None] * freq[None, :]
    cos, sin = jnp.cos(ang), jnp.sin(ang)
    cos_h = jnp.concatenate([cos, cos], -1)
    sin_h = jnp.concatenate([-sin, sin], -1)
    reps = LANES // NSA_HD
    return jnp.tile(cos_h, (1, reps)), jnp.tile(sin_h, (1, reps))


def _reorder_w_in(w):
    d = w.shape[0]
    sizes = (GLA_KW, GLA_KW, GLA_VW, GLA_GATE_RANK, GLA_VW, NSA_QW) + (NSA_KVW,) * 6 + (NSA_HEADS * 3,)
    offs = np.concatenate([[0], np.cumsum(sizes)])
    gq, gk, gv, ga, gr, nq, kc, vc, ks, vs, kw, vw, ng = [w[:, offs[i]:offs[i + 1]] for i in range(len(sizes))]
    pad = jnp.zeros((d, PROJ_TN - GLA_GATE_RANK - NSA_HEADS * 3), w.dtype)
    return jnp.concatenate([gq, gk, gv, gr, nq, kc, vc, ks, vs, kw, vw, ga, ng, pad], axis=1)


def _block_diag2(w):
    z = jnp.zeros_like(w)
    return jnp.concatenate([jnp.concatenate([w, z], -1), jnp.concatenate([z, w], -1)], -2)


def _compress_params(pos, w1):
    half = CMP_LEN // 2
    hidden = w1.shape[-1]
    w1r = w1.reshape(2, 2, half, NSA_HD, hidden)
    eye = jnp.eye(2, dtype=w1.dtype)
    w1p = jnp.einsum("hg,palDn->plhDgan", eye, w1r).reshape(2, half * 2 * NSA_HD, 2 * 2 * hidden)
    posr = jnp.tile(pos.reshape(2, 2, half, 1, NSA_HD), (1, 1, 1, 2, 1)).reshape(2, 2, 1, half * 2 * NSA_HD)
    pos16 = jnp.tile(posr, (1, 1, SUBLANES, 1)).reshape(2, 2 * SUBLANES, half * 2 * NSA_HD)
    return pos16.astype(BF16), w1p.astype(BF16)


def _tile_size(n, pref):
    t = math.gcd(n, pref)
    assert t % SUBLANES == 0
    return t


def _const_tables(n_keys):
    cs = np.arange(NBLK)[:, None] * CMP_STRIDE
    ss = np.arange(NBLK)[None, :] * SEL_LEN
    ovl = ((cs < ss + SEL_LEN) & (cs + CMP_LEN > ss)).astype(np.float32)
    bexp = (np.arange(NBLK)[:, None] == (np.arange(n_keys)[None, :] // SEL_LEN)).astype(np.float32)
    gexp = np.zeros((3, LANES, NSA_QW), np.float32)
    for c in range(3):
        for j in range(NSA_HEADS):
            gexp[c, GATE_LANE0 + 3 * j + c, j * NSA_HD:(j + 1) * NSA_HD] = 1.0
    return jnp.asarray(ovl, BF16), jnp.asarray(bexp, BF16), jnp.asarray(gexp, BF16)


def kernel(x_prompt, x_sample, cache_cmp_kv, cache_slc_kv, cache_win_kv, state_gla, page_table,
           w_ffn1_in, w_ffn1_out, w_in, w_gla_gate2, b_gla_gate2, gla_norm_g,
           cmp_pos, cmp_w1, cmp_w2, w_out, w_ffn2_in, w_ffn2_out, ln_g, ln_b):
    B, T, D = x_prompt.shape
    DB, TS, _ = x_sample.shape
    depth = w_in.shape[0]
    n_pages = page_table.shape[1]
    past_len = n_pages * PAGE_SIZE
    w_buf = cache_win_kv.shape[2]
    alpha = (2.0 * depth) ** 0.25
    assert w_buf == WINDOW and T % TQ == 0 and past_len % KT == 0

    cos_p, sin_p = _rope_tables(jnp.arange(T, dtype=jnp.int32))
    cos_s, sin_s = _rope_tables(past_len + jnp.arange(TS, dtype=jnp.int32))
    ovl, bexp_p, gexp = _const_tables(T)
    _, bexp_s, _ = _const_tables(past_len + KT)

    hp = x_prompt.reshape(B * T, D)
    hs = x_sample.reshape(DB * TS, D)
    tm_s = _tile_size(DB * TS, 1024)
    cos_s, sin_s = (jnp.tile(t, (tm_s // TS, 1)) for t in (cos_s, sin_s))
    outs = {k: [] for k in ("cmp_p", "slc_p", "win_p", "gla_p", "cmp_s", "slc_s", "win_s", "gla_s")}

    for l in range(depth):
        wf1i, wf1o = w_ffn1_in[l].astype(BF16), w_ffn1_out[l].astype(BF16)
        wf2i, wf2o = w_ffn2_in[l].astype(BF16), w_ffn2_out[l].astype(BF16)
        wi = _reorder_w_in(w_in[l]).astype(BF16)
        wo_g, wo_n = w_out[l][:GLA_VW].astype(BF16), w_out[l][GLA_VW:].astype(BF16)
        lg, lb = ln_g[l][:, None, :], ln_b[l][:, None, :]
        w2h = jnp.zeros((LANES, GLA_KW), F32).at[:GLA_GATE_RANK].set(w_gla_gate2[l]).astype(BF16)
        b2h = b_gla_gate2[l][None, :]
        gn = gla_norm_g[l][None, :]
        pos2, w1bd = _compress_params(cmp_pos[l], cmp_w1[l])
        w2bd = _block_diag2(cmp_w2[l]).astype(BF16)

        h1, h1b = _ffn_ln(hp, wf1i, wf1o, lg[0], lb[0], alpha, _tile_size(B * T, 512), 512)
        proj, cmp_t, slc_t, win_t = _in_proj(h1b, wi, cos_p, sin_p, _tile_size(T, 1024), 1024, True)
        og, gla_state = _gla(proj, w2h, b2h, gn, jnp.zeros((B, GLA_HEADS, GLA_DK, GLA_DV), F32), B, T)
        n_grp = T // CMP_STRIDE
        n_cmp = (T - CMP_LEN) // CMP_STRIDE + 1
        kc, vc = _compress([proj], [pl.BlockSpec((T, ROWW), lambda b: (b, C_CMP // ROWW))],
                           pos2, w1bd, w2bd, B, n_grp, n_cmp)
        kc, vc = (jnp.pad(a, ((0, 0), (0, NBLK - n_grp), (0, 0))) for a in (kc, vc))
        on = _nsa_prompt(proj, kc, vc, ovl, bexp_p, gexp, B, T)
        h2, h2b = _out_ln(og, on, wo_g, wo_n, h1, lg[1], lb[1], alpha, _tile_size(B * T, 512))
        hp, _ = _ffn_ln(h2, wf2i, wf2o, lg[2], lb[2], alpha, _tile_size(B * T, 512), 512)
        row6 = lambda a, n, t: a.reshape(n, t, 2, NSA_KV, NSA_HD)
        row6_t = lambda a: a.reshape(a.shape[0], 2, NSA_KV, NSA_HD, a.shape[2]).transpose(0, 4, 1, 2, 3)
        outs["cmp_p"].append(row6_t(cmp_t))
        outs["slc_p"].append(row6_t(slc_t))
        outs["win_p"].append(row6_t(win_t[:, :, T - w_buf:]))
        outs["gla_p"].append(gla_state)

        s1, s1b = _ffn_ln(hs, wf1i, wf1o, lg[0], lb[0], alpha, _tile_size(DB * TS, 512), 512)
        (sproj,) = _in_proj(s1b, wi, cos_s, sin_s, tm_s, 1024, False)
        sog, sgla_state = _gla(sproj, w2h, b2h, gn, state_gla[l], DB, TS)
        feature_major = lambda c: c.transpose(0, 2, 3, 4, 1).reshape(c.shape[0] * ROWW, c.shape[1])
        cmp_pool_t = feature_major(cache_cmp_kv[l])
        slc_pool_t = feature_major(cache_slc_kv[l])
        win_cache_t = feature_major(cache_win_kv[l])
        n_grp_s = past_len // CMP_STRIDE
        n_cmp_s = (past_len + TS - CMP_LEN) // CMP_STRIDE + 1
        page_specs = [pl.BlockSpec((ROWW, PAGE_SIZE), lambda b, pt, p=p: (pt[b, p], 0)) for p in range(n_pages)]
        skc, svc = _compress([cmp_pool_t] * n_pages, page_specs, pos2, w1bd, w2bd, DB, n_grp_s, n_cmp_s,
                             scalar_args=(page_table,), feature_major=True)
        skc, svc = (jnp.pad(a, ((0, 0), (0, NBLK - n_grp_s), (0, 0))) for a in (skc, svc))
        son = _nsa_sample(sproj, skc, svc, slc_pool_t, win_cache_t, page_table, ovl, bexp_s, gexp, DB, TS, past_len)
        s2, s2b = _out_ln(sog, son, wo_g, wo_n, s1, lg[1], lb[1], alpha, _tile_size(DB * TS, 512))
        hs, _ = _ffn_ln(s2, wf2i, wf2o, lg[2], lb[2], alpha, _tile_size(DB * TS, 512), 512)
        new_win = row6(sproj[:, C_WIN:C_WIN + ROWW], DB, TS)
        outs["cmp_s"].append(row6(sproj[:, C_CMP:C_CMP + ROWW], DB, TS))
        outs["slc_s"].append(row6(sproj[:, C_SLC:C_SLC + ROWW], DB, TS))
        outs["win_s"].append(jnp.concatenate([cache_win_kv[l], new_win], axis=1)[:, -w_buf:])
        outs["gla_s"].append(sgla_state)

    st = lambda k: jnp.stack(outs[k])
    return (hp.reshape(B, T, D), hs.reshape(DB, TS, D),
            st("cmp_p"), st("slc_p"), st("win_p"), st("gla_p"),
            st("cmp_s"), st("slc_s"), st("win_s"), st("gla_s"))
```

```python
import functools
import math

import numpy as np
import jax
import jax.numpy as jnp
from jax import lax
from jax.experimental import pallas as pl
from jax.experimental.pallas import tpu as pltpu

F32 = jnp.float32
BF16 = jnp.bfloat16

GLA_HEADS = 4
GLA_DK = 128
GLA_DV = 256
GLA_GATE_RANK = 16
GLA_TAU = 16.0
NSA_HEADS = 16
NSA_KV = 4
NSA_GROUP = NSA_HEADS // NSA_KV
NSA_HD = 64
CMP_LEN = 32
CMP_STRIDE = 16
SEL_LEN = 64
SEL_TOPN = 16
WINDOW = 512
PAGE_SIZE = 128
ROPE_THETA = 10000.0
LN_EPS = 1e-5
NEG = -1e30

LANES = 128
SUBLANES = 8
VMEM_LIMIT_BYTES = 56 * 1024 * 1024

GLA_KW = GLA_HEADS * GLA_DK
GLA_VW = GLA_HEADS * GLA_DV
NSA_QW = NSA_HEADS * NSA_HD
NSA_KVW = NSA_KV * NSA_HD
ROWW = 2 * NSA_KVW
C_GQ = 0
C_GK = C_GQ + GLA_KW
C_GV = C_GK + GLA_KW
C_GR = C_GV + GLA_VW
C_NQ = C_GR + GLA_VW
C_CMP = C_NQ + NSA_QW
C_SLC = C_CMP + ROWW
C_WIN = C_SLC + ROWW
C_TAIL = C_WIN + ROWW
PROJ_TN = 512
PROJ_W = C_TAIL + PROJ_TN
GATE_LANE0 = GLA_GATE_RANK

GLA_CHUNK = 128
GLA_SUB = 32
TQ = 256
SROWS = 128
KT = 128
FKT = 256
NBLK = 128


def _cparams(sem):
    return pltpu.CompilerParams(dimension_semantics=sem, vmem_limit_bytes=VMEM_LIMIT_BYTES)


def _dot(a, b):
    return jnp.dot(a, b, preferred_element_type=F32)


def _dot_nt(a, b):
    return lax.dot_general(a, b, (((1,), (1,)), ((), ())), preferred_element_type=F32)


def _layer_norm(y, g, b):
    mu = jnp.mean(y, -1, keepdims=True)
    d = y - mu
    var = jnp.mean(d * d, -1, keepdims=True)
    return d * lax.rsqrt(var + LN_EPS) * g + b


def _split_bf16(x, n):
    parts = []
    for _ in range(n - 1):
        hi = x.astype(BF16)
        parts.append(hi)
        x = x - hi.astype(F32)
    parts.append(x.astype(BF16))
    return parts


def _ffn_ln_kernel(x_ref, wg_ref, wu_ref, wo_ref, g_ref, b_ref, y_ref, yb_ref, xb_ref, acc_ref, *, alpha):
    j = pl.program_id(1)

    @pl.when(j == 0)
    def _():
        xb_ref[...] = x_ref[...].astype(BF16)
        acc_ref[...] = jnp.zeros_like(acc_ref)

    xb = xb_ref[...]
    h = _dot(xb, wg_ref[...])
    u = _dot(xb, wu_ref[...])
    a = (h * jax.nn.sigmoid(h) * u).astype(BF16)
    acc_ref[...] += _dot(a, wo_ref[...])

    @pl.when(j == pl.num_programs(1) - 1)
    def _():
        y = _layer_norm(alpha * x_ref[...] + 0.5 * acc_ref[...], g_ref[...], b_ref[...])
        y_ref[...] = y
        yb_ref[...] = y.astype(BF16)


def _ffn_ln(x, w_in, w_out, g, b, alpha, tm, tf):
    n, d = x.shape
    dff = w_out.shape[0]
    nf = dff // tf
    assert n % tm == 0 and dff % tf == 0
    return pl.pallas_call(
        functools.partial(_ffn_ln_kernel, alpha=alpha),
        grid=(n // tm, nf),
        in_specs=[
            pl.BlockSpec((tm, d), lambda i, j: (i, 0)),
            pl.BlockSpec((d, tf), lambda i, j: (0, j)),
            pl.BlockSpec((d, tf), lambda i, j: (0, j + nf)),
            pl.BlockSpec((tf, d), lambda i, j: (j, 0)),
            pl.BlockSpec((1, d), lambda i, j: (0, 0)),
            pl.BlockSpec((1, d), lambda i, j: (0, 0)),
        ],
        out_specs=[pl.BlockSpec((tm, d), lambda i, j: (i, 0)),
                   pl.BlockSpec((tm, d), lambda i, j: (i, 0))],
        out_shape=[jax.ShapeDtypeStruct((n, d), F32), jax.ShapeDtypeStruct((n, d), BF16)],
        scratch_shapes=[pltpu.VMEM((tm, d), BF16), pltpu.VMEM((tm, d), F32)],
        compiler_params=_cparams(("parallel", "arbitrary")),
        name="ffn_ln",
    )(x, w_in, w_in, w_out, g, b)


def _rope_lanes(x, cos, sin_signed):
    lane = lax.broadcasted_iota(jnp.int32, x.shape, 1)
    first_half = (lane & (NSA_HD - 1)) < (NSA_HD // 2)
    partner = jnp.where(first_half, pltpu.roll(x, LANES - NSA_HD // 2, 1), pltpu.roll(x, NSA_HD // 2, 1))
    return x * cos + partner * sin_signed


def _is_rotary_col(col):
    if C_NQ <= col < C_CMP:
        return True
    return C_CMP <= col < C_TAIL and (col - C_CMP) % ROWW < NSA_KVW


def _in_proj_kernel(x_ref, w_ref, cos_ref, sin_ref, o_ref, *t_refs, tn):
    j = pl.program_id(1)
    chunk = 2 * LANES

    for jj in range(PROJ_W // tn):
        @pl.when(j == jj)
        def _(jj=jj):
            for c0 in range(0, tn, chunk):
                acc = _dot(x_ref[...], w_ref[:, c0:c0 + chunk])
                for s0 in range(0, chunk, LANES):
                    col = jj * tn + c0 + s0
                    val = acc[:, s0:s0 + LANES]
                    if _is_rotary_col(col):
                        val = _rope_lanes(val, cos_ref[...], sin_ref[...])
                    o_ref[:, c0 + s0:c0 + s0 + LANES] = val
                    group, off = divmod(col - C_CMP, ROWW)
                    if t_refs and 0 <= group < len(t_refs):
                        t_refs[group][0, off:off + LANES, :] = val.T


def _in_proj(xb, w, cos, sin, tm, tn, emit_t):
    n, d = xb.shape
    period = cos.shape[0]
    assert n % tm == 0 and period % tm == 0 and PROJ_W % tn == 0 and tn % ROWW == 0
    npb = period // tm
    out_specs = [pl.BlockSpec((tm, tn), lambda i, j: (i, j))]
    out_shape = [jax.ShapeDtypeStruct((n, PROJ_W), F32)]
    if emit_t:
        out_specs += [pl.BlockSpec((1, ROWW, tm), lambda i, j: (i // npb, 0, i % npb))] * 3
        out_shape += [jax.ShapeDtypeStruct((n // period, ROWW, period), F32)] * 3
    return pl.pallas_call(
        functools.partial(_in_proj_kernel, tn=tn),
        grid=(n // tm, PROJ_W // tn),
        in_specs=[
            pl.BlockSpec((tm, d), lambda i, j: (i, 0)),
            pl.BlockSpec((d, tn), lambda i, j: (0, j)),
            pl.BlockSpec((tm, LANES), lambda i, j: (i % npb, 0)),
            pl.BlockSpec((tm, LANES), lambda i, j: (i % npb, 0)),
        ],
        out_specs=out_specs,
        out_shape=out_shape,
        compiler_params=_cparams(("parallel", "arbitrary")),
        name="in_proj",
    )(xb, w, cos, sin)


def _gla_kernel(q_ref, k_ref, v_ref, r_ref, ga_ref, w2_ref, b2_ref, gn_ref, s0_ref,
                o_ref, sout_ref, st_ref, *, n_valid, n_chunks, n_sb):
    t = pl.program_id(1)
    C, S, H = GLA_CHUNK, GLA_SUB, GLA_HEADS

    @pl.when(t == 0)
    def _():
        for sb in range(n_sb):
            for h in range(H):
                st_ref[sb * H + h] = s0_ref[sb, h].T

    row_k = lax.broadcasted_iota(jnp.int32, (C, GLA_DK), 0)
    row_a = lax.broadcasted_iota(jnp.int32, (C, GLA_KW), 0)
    row_c = lax.broadcasted_iota(jnp.int32, (C, C), 0)
    col_c = lax.broadcasted_iota(jnp.int32, (C, C), 1)
    causal = row_c >= col_c
    tri = causal.astype(BF16)

    def load(ref, r0, sb, c0, w):
        if n_valid >= C:
            return ref[pl.ds(r0, C), c0:c0 + w]
        x = ref[sb * n_valid:(sb + 1) * n_valid, c0:c0 + w]
        return jnp.concatenate([x, jnp.zeros((C - n_valid, w), x.dtype)], axis=0)

    def chunk(c, carry):
        r0 = pl.multiple_of(c * C, C)
        for sb in range(n_sb):
            pre = _dot(load(ga_ref, r0, sb, 0, LANES).astype(BF16), w2_ref[...]) + b2_ref[...]
            la = (jnp.minimum(pre, 0.0) - jnp.log(1.0 + jnp.exp(-jnp.abs(pre)))) * (1.0 / GLA_TAU)
            if n_valid < C:
                la = jnp.where(row_a < n_valid, la, 0.0)
            la_hi, la_lo = _split_bf16(la, 2)
            b_all = _dot(tri, la_hi) + _dot(tri, la_lo)
            for h in range(H):
                b = b_all[:, h * GLA_DK:(h + 1) * GLA_DK]
                q = load(q_ref, r0, sb, h * GLA_DK, GLA_DK) * (GLA_DK ** -0.5)
                k = load(k_ref, r0, sb, h * GLA_DK, GLA_DK)
                v = load(v_ref, r0, sb, h * GLA_DV, GLA_DV)
                b_last = b[C - 1:C, :]

                strips = []
                for i in range(C // S):
                    rows = slice(i * S, (i + 1) * S)
                    anchor = jnp.zeros((1, GLA_DK), F32) if i == 0 else b[i * S - 1:i * S, :]
                    qi = q[rows] * jnp.exp(b[rows] - anchor)
                    ki = k * jnp.exp(jnp.where(row_k < (i + 1) * S, anchor - b, 0.0))
                    strips.append(_dot_nt(qi.astype(BF16), ki.astype(BF16)))
                attn = jnp.where(causal, jnp.concatenate(strips, axis=0), 0.0)

                st = st_ref[sb * H + h]
                o = (_dot_nt((q * jnp.exp(b)).astype(BF16), st.astype(BF16))
                     + _dot(attn.astype(BF16), v.astype(BF16)))
                ke = k * jnp.exp(b_last - b)
                st_ref[sb * H + h] = st * jnp.exp(b_last) + _dot(v.T.astype(BF16), ke.astype(BF16))

                rms = lax.rsqrt(jnp.mean(o * o, -1, keepdims=True) + LN_EPS)
                r = load(r_ref, r0, sb, h * GLA_DV, GLA_DV)
                out = (o * rms * gn_ref[...] * (r * jax.nn.sigmoid(r))).astype(o_ref.dtype)
                cols = slice(h * GLA_DV, (h + 1) * GLA_DV)
                if n_valid >= C:
                    o_ref[pl.ds(r0, C), cols] = out
                else:
                    o_ref[sb * n_valid:(sb + 1) * n_valid, cols] = out[:n_valid]
        return carry

    lax.fori_loop(0, n_chunks, chunk, 0)

    @pl.when(t == pl.num_programs(1) - 1)
    def _():
        for sb in range(n_sb):
            for h in range(H):
                sout_ref[sb, h] = st_ref[sb * H + h].T


def _gla(proj, w2h, b2h, gn, s0, n_seq, t_seq):
    C, H = GLA_CHUNK, GLA_HEADS
    if t_seq >= C:
        tb = math.gcd(t_seq, 1024)
        assert tb % C == 0
        n_valid, n_chunks, n_sb = C, tb // C, 1
    else:
        tb, n_valid, n_chunks = t_seq, t_seq, 1
        n_sb = math.gcd(n_seq, 2)
        assert t_seq % SUBLANES == 0
    nt = t_seq // tb
    rows = n_sb * tb
    rowblk = lambda b, t: b * nt + t
    return pl.pallas_call(
        functools.partial(_gla_kernel, n_valid=n_valid, n_chunks=n_chunks, n_sb=n_sb),
        grid=(n_seq // n_sb, nt),
        in_specs=[
            pl.BlockSpec((rows, GLA_KW), lambda b, t: (rowblk(b, t), C_GQ // GLA_KW)),
            pl.BlockSpec((rows, GLA_KW), lambda b, t: (rowblk(b, t), C_GK // GLA_KW)),
            pl.BlockSpec((rows, GLA_VW), lambda b, t: (rowblk(b, t), C_GV // GLA_VW)),
            pl.BlockSpec((rows, GLA_VW), lambda b, t: (rowblk(b, t), C_GR // GLA_VW)),
            pl.BlockSpec((rows, LANES), lambda b, t: (rowblk(b, t), C_TAIL // LANES)),
            pl.BlockSpec((LANES, GLA_KW), lambda b, t: (0, 0)),
            pl.BlockSpec((1, GLA_KW), lambda b, t: (0, 0)),
            pl.BlockSpec((1, GLA_DV), lambda b, t: (0, 0)),
            pl.BlockSpec((n_sb, H, GLA_DK, GLA_DV), lambda b, t: (b, 0, 0, 0)),
        ],
        out_specs=[
            pl.BlockSpec((rows, GLA_VW), lambda b, t: (rowblk(b, t), 0)),
            pl.BlockSpec((n_sb, H, GLA_DK, GLA_DV), lambda b, t: (b, 0, 0, 0)),
        ],
        out_shape=[jax.ShapeDtypeStruct((n_seq * t_seq, GLA_VW), BF16 if t_seq >= C else F32),
                   jax.ShapeDtypeStruct((n_seq, H, GLA_DK, GLA_DV), F32)],
        scratch_shapes=[pltpu.VMEM((n_sb * H, GLA_DV, GLA_DK), F32)],
        compiler_params=_cparams(("parallel", "arbitrary")),
        name="gla",
    )(proj, proj, proj, proj, proj, w2h, b2h, gn, s0)


def _gelu_tanh(x):
    return 0.5 * x * (1.0 + jnp.tanh(math.sqrt(2.0 / math.pi) * (x + 0.044715 * (x * x * x))))


def _compress_kernel(*refs, n_scalar, n_src, n_cmp, feature_major):
    refs = refs[n_scalar:]
    src = refs[:n_src]
    pos_ref, w1_ref, w2_ref, kc_ref, vc_ref, slab_ref, u_ref = refs[n_src:]
    r_src = src[0].shape[1] if feature_major else src[0].shape[0]
    n_grp = n_src * r_src // CMP_STRIDE
    row = lax.broadcasted_iota(jnp.int32, (n_grp, LANES), 0)
    outs = (kc_ref, vc_ref)
    hw = 2 * LANES
    for part in range(2):
        pos_bias = _dot(pos_ref[part], w1_ref[part])
        for pair in range(NSA_KVW // LANES):
            c0 = part * NSA_KVW + pair * LANES
            for n, s in enumerate(src):
                slab_ref[n * r_src:(n + 1) * r_src, :] = s[c0:c0 + LANES, :].T if feature_major else s[:, c0:c0 + LANES]
            for l in range(CMP_STRIDE):
                u_ref[:, l * LANES:(l + 1) * LANES] = slab_ref[pl.ds(l, n_grp, stride=CMP_STRIDE), :].astype(BF16)
            ab = _dot(u_ref[...], w1_ref[part])
            hidden = []
            for hh in range(2):
                a = ab[:, hh * hw:hh * hw + LANES] + pos_bias[0:1, hh * hw:hh * hw + LANES]
                b = ab[:, hh * hw + LANES:(hh + 1) * hw] + pos_bias[SUBLANES:SUBLANES + 1, hh * hw + LANES:(hh + 1) * hw]
                hidden.append(a + pltpu.roll(b, n_grp - 1, 0))
            out = _dot(_gelu_tanh(jnp.concatenate(hidden, axis=1)).astype(BF16), w2_ref[part])
            outs[part][0, :, pair * LANES:(pair + 1) * LANES] = jnp.where(row < n_cmp, out, 0.0)


def _compress(srcs, src_specs, pos2, w1bd, w2bd, n_seq, n_grp, n_cmp, scalar_args=(), feature_major=False):
    n_src = len(srcs)
    nsp = len(scalar_args)
    in_specs = list(src_specs) + [
        pl.BlockSpec(pos2.shape, lambda b, *_: (0, 0, 0)),
        pl.BlockSpec(w1bd.shape, lambda b, *_: (0, 0, 0)),
        pl.BlockSpec(w2bd.shape, lambda b, *_: (0, 0, 0)),
    ]
    out_spec = pl.BlockSpec((1, n_grp, NSA_KVW), lambda b, *_: (b, 0, 0))
    return pl.pallas_call(
        functools.partial(_compress_kernel, n_scalar=nsp, n_src=n_src, n_cmp=n_cmp, feature_major=feature_major),
        grid_spec=pltpu.PrefetchScalarGridSpec(
            num_scalar_prefetch=nsp, grid=(n_seq,), in_specs=in_specs, out_specs=[out_spec, out_spec],
            scratch_shapes=[pltpu.VMEM((n_grp * CMP_STRIDE, LANES), F32),
                            pltpu.VMEM((n_grp, CMP_STRIDE * LANES), BF16)]),
        out_shape=[jax.ShapeDtypeStruct((n_seq, n_grp, NSA_KVW), F32)] * 2,
        compiler_params=_cparams(("parallel",)),
        name="compress",
    )(*scalar_args, *srcs, pos2, w1bd, w2bd)


def _build_q(q_ref, qs_ref, tq):
    lane = lax.broadcasted_iota(jnp.int32, (tq, LANES), 1)
    for j in range(NSA_HEADS):
        h = j // NSA_GROUP
        x = q_ref[:, (j // 2) * LANES:(j // 2 + 1) * LANES] * (NSA_HD ** -0.5)
        if (j % 2) != (h % 2):
            x = pltpu.roll(x, NSA_HD, 1)
        keep = (lane >= NSA_HD) if (h % 2) else (lane < NSA_HD)
        qs_ref[j * tq:(j + 1) * tq, :] = jnp.where(keep, x, 0.0).astype(qs_ref.dtype)


def _pair_slab(h):
    return slice((h // 2) * LANES, (h // 2 + 1) * LANES)


def _tile_rows(x, n):
    return jnp.concatenate([x] * n, axis=0)


def _cmp_attention(qh, kc, vc, qpos, n_cmp):
    tq = qpos.shape[0]
    blk = lax.broadcasted_iota(jnp.int32, (tq, NBLK), 1)
    valid = ((blk * CMP_STRIDE + (CMP_LEN - 1) <= qpos) & (blk < n_cmp)).astype(F32)
    valid4 = _tile_rows(valid, NSA_GROUP)
    s = jnp.where(valid4 > 0.5, _dot_nt(qh, kc.astype(BF16)), NEG)
    e = jnp.exp(s - jnp.max(s, -1, keepdims=True))
    p = e * (valid4 * (1.0 / jnp.sum(e, -1, keepdims=True)))
    o = _dot(p.astype(BF16), vc.astype(BF16))
    psum = p[0:tq]
    for g in range(1, NSA_GROUP):
        psum = psum + p[g * tq:(g + 1) * tq]
    return o, psum


def _select_blocks(psum, ovl, qpos, n_sel):
    imp = sum(_dot(t, ovl) for t in _split_bf16(psum, 3))
    blk = lax.broadcasted_iota(jnp.int32, psum.shape, 1)
    ss = blk * SEL_LEN
    cur = qpos - (qpos & (SEL_LEN - 1))
    forced = (ss == 0) | (ss == cur) | (ss == cur - SEL_LEN)
    score = jnp.where(ss <= qpos, jnp.where(forced, 1e9, imp), -1e9)
    nb = -(-n_sel // SUBLANES) * SUBLANES
    st = score.T[0:nb]
    sub = lax.broadcasted_iota(jnp.int32, st.shape, 0)
    rank = jnp.zeros(st.shape, F32)
    for s in range(n_sel):
        other = st[s:s + 1, :]
        beats = (other > st) | ((other == st) & (sub > s))
        rank = rank + beats.astype(F32)
    sel_t = ((rank < SEL_TOPN) & (st > -1e8)).astype(F32)
    sel_t = jnp.concatenate([sel_t, jnp.zeros((NBLK - nb, st.shape[1]), F32)], axis=0)
    return sel_t.T


def _emit_branch(out_ref, o, gate, h, tq, first, denom_in_other_half=False):
    lane = lax.broadcasted_iota(jnp.int32, (tq, LANES), 1)
    for gp in range(NSA_GROUP // 2):
        y0 = o[(2 * gp) * tq:(2 * gp + 1) * tq]
        y1 = o[(2 * gp + 1) * tq:(2 * gp + 2) * tq]
        if h % 2:
            y0 = pltpu.roll(y0, NSA_HD, 1)
        else:
            y1 = pltpu.roll(y1, NSA_HD, 1)
        m = 2 * h + gp
        sl = slice(m * LANES, (m + 1) * LANES)
        val = jnp.where(lane < NSA_HD, y0, y1)
        if denom_in_other_half:
            val = val / pltpu.roll(jnp.where(lane < NSA_HD, y1, y0), NSA_HD, 1)
        val = val * gate[:, sl]
        if first:
            out_ref[:, sl] = val
        else:
            out_ref[:, sl] += val


def _expand_gates(tail, gexp_ref):
    g_hi, g_lo = _split_bf16(jax.nn.sigmoid(tail), 2)
    return [_dot(g_hi, gexp_ref[c]) + _dot(g_lo, gexp_ref[c]) for c in range(3)]


def _own_half(h, shape):
    lane = lax.broadcasted_iota(jnp.int32, shape, len(shape) - 1)
    return (lane >= NSA_HD) if (h % 2) else (lane < NSA_HD)


def _nsa_prompt_kernel(q_ref, slc_ref, win_ref, tail_ref, kc_ref, vc_ref, ovl_ref, bexp_ref, gexp_ref,
                       o_ref, qs_ref, mexp_ref, m_ref, acc_ref, out_ref, *, n_cmp, n_sel):
    i = pl.program_id(1)
    q0 = i * TQ
    G = NSA_GROUP
    R = G * TQ
    _build_q(q_ref, qs_ref, TQ)
    gates = _expand_gates(tail_ref[:, 0:LANES], gexp_ref)
    qpos = q0 + lax.broadcasted_iota(jnp.int32, (TQ, NBLK), 0)
    qpos_k = q0 + lax.broadcasted_iota(jnp.int32, (TQ, FKT), 0)
    lane_k = lax.broadcasted_iota(jnp.int32, (TQ, FKT), 1)
    n_kt = mexp_ref.shape[1]

    for h in range(NSA_KV):
        qh = qs_ref[h * R:(h + 1) * R, :]
        o_c, psum = _cmp_attention(qh, kc_ref[0, :, _pair_slab(h)], vc_ref[0, :, _pair_slab(h)], qpos, n_cmp)
        _emit_branch(out_ref, o_c, gates[0], h, TQ, True)
        sel = _select_blocks(psum, ovl_ref[...], qpos, n_sel)
        mexp = _dot(sel.astype(BF16), bexp_ref[...])
        for kt in range(n_kt):
            mexp_ref[h, kt] = mexp[:, kt * FKT:(kt + 1) * FKT].astype(BF16)

    def flash(src_ref, lo, hi, bias_fn):
        m_ref[...] = jnp.full(m_ref.shape, NEG, F32)
        acc_ref[...] = jnp.zeros(acc_ref.shape, F32)

        def body(kt, carry):
            k0 = pl.multiple_of(kt * FKT, FKT)
            kpos = k0 + lane_k
            kslab = [src_ref[pl.ds(k0, FKT), m * LANES:(m + 1) * LANES].astype(BF16) for m in range(2)]
            vslab = [src_ref[pl.ds(k0, FKT), NSA_KVW + m * LANES:NSA_KVW + (m + 1) * LANES] for m in range(2)]
            for h in range(NSA_KV):
                vv = jnp.where(_own_half(h, (FKT, LANES)), vslab[h // 2], 1.0).astype(BF16)
                s = _dot_nt(qs_ref[h * R:(h + 1) * R, :], kslab[h // 2]) + _tile_rows(bias_fn(h, kt, kpos), G)
                m_prev = m_ref[h]
                smax = s[:, 0:LANES]
                for c in range(1, FKT // LANES):
                    smax = jnp.maximum(smax, s[:, c * LANES:(c + 1) * LANES])
                m_new = jnp.maximum(m_prev, jnp.max(smax, -1, keepdims=True))
                alpha = jnp.exp(m_prev - m_new)
                p = jnp.exp(s - jnp.concatenate([m_new] * (FKT // LANES), axis=1))
                acc_ref[h] = alpha * acc_ref[h] + _dot(p.astype(BF16), vv)
                m_ref[h] = m_new
            return carry

        lax.fori_loop(lo, hi, body, 0)

    def sel_bias(h, kt, kpos):
        return jnp.where((mexp_ref[h, kt].astype(F32) > 0.5) & (kpos <= qpos_k), 0.0, NEG)

    def win_bias(h, kt, kpos):
        dist = qpos_k - kpos
        return jnp.where((dist >= 0) & (dist < WINDOW), 0.0, NEG)

    hi = (q0 + TQ + FKT - 1) // FKT
    flash(slc_ref, 0, hi, sel_bias)
    for h in range(NSA_KV):
        _emit_branch(out_ref, acc_ref[h], gates[1], h, TQ, False, denom_in_other_half=True)
    flash(win_ref, jnp.maximum(q0 - WINDOW, 0) // FKT, hi, win_bias)
    for h in range(NSA_KV):
        _emit_branch(out_ref, acc_ref[h], gates[2], h, TQ, False, denom_in_other_half=True)
    o_ref[...] = out_ref[...].astype(o_ref.dtype)


def _nsa_prompt(proj, kc, vc, ovl, bexp, gexp, n_seq, t_seq):
    assert t_seq % FKT == 0 and t_seq % TQ == 0 and TQ % LANES == 0 and NBLK == LANES
    nq = t_seq // TQ
    n_cmp = (t_seq - CMP_LEN) // CMP_STRIDE + 1
    n_sel = -(-t_seq // SEL_LEN)
    G = NSA_GROUP
    return pl.pallas_call(
        functools.partial(_nsa_prompt_kernel, n_cmp=n_cmp, n_sel=n_sel),
        grid=(n_seq, nq),
        in_specs=[
            pl.BlockSpec((TQ, NSA_QW), lambda b, i: (b * nq + i, C_NQ // NSA_QW)),
            pl.BlockSpec((t_seq, ROWW), lambda b, i: (b, C_SLC // ROWW)),
            pl.BlockSpec((t_seq, ROWW), lambda b, i: (b, C_WIN // ROWW)),
            pl.BlockSpec((TQ, PROJ_TN), lambda b, i: (b * nq + i, C_TAIL // PROJ_TN)),
            pl.BlockSpec((1, NBLK, NSA_KVW), lambda b, i: (b, 0, 0)),
            pl.BlockSpec((1, NBLK, NSA_KVW), lambda b, i: (b, 0, 0)),
            pl.BlockSpec(ovl.shape, lambda b, i: (0, 0)),
            pl.BlockSpec(bexp.shape, lambda b, i: (0, 0)),
            pl.BlockSpec(gexp.shape, lambda b, i: (0, 0, 0)),
        ],
        out_specs=pl.BlockSpec((TQ, NSA_QW), lambda b, i: (b * nq + i, 0)),
        out_shape=jax.ShapeDtypeStruct((n_seq * t_seq, NSA_QW), BF16),
        scratch_shapes=[
            pltpu.VMEM((NSA_HEADS * TQ, LANES), BF16),
            pltpu.VMEM((NSA_KV, t_seq // FKT, TQ, FKT), BF16),
            pltpu.VMEM((NSA_KV, G * TQ, LANES), F32),
            pltpu.VMEM((NSA_KV, G * TQ, LANES), F32),
            pltpu.VMEM((TQ, NSA_QW), F32),
        ],
        compiler_params=_cparams(("parallel", "arbitrary")),
        name="nsa_prompt",
    )(proj, proj, proj, proj, kc, vc, ovl, bexp, gexp)


def _nsa_sample_kernel(*refs, n_sb, n_pages, t_new, **kw):
    q_ref, snew_ref, wnew_ref, tail_ref, kc_ref, vc_ref, wcache_ref = refs[1:8]
    pages = refs[8:8 + n_sb * n_pages]
    ovl_ref, bexp_ref, gexp_ref, o_ref, qs_ref, out_ref = refs[8 + n_sb * n_pages:]
    for sb in range(n_sb):
        rows = pl.ds(sb * t_new, t_new)
        _nsa_sample_one(q_ref.at[rows], snew_ref.at[rows], wnew_ref.at[rows], tail_ref.at[rows],
                        kc_ref.at[pl.ds(sb, 1)], vc_ref.at[pl.ds(sb, 1)], wcache_ref.at[pl.ds(sb * ROWW, ROWW)],
                        pages[sb * n_pages:(sb + 1) * n_pages], ovl_ref, bexp_ref, gexp_ref, o_ref.at[rows],
                        qs_ref.at[pl.ds(sb * NSA_HEADS * t_new, NSA_HEADS * t_new)], out_ref.at[rows],
                        t_new=t_new, **kw)


def _nsa_sample_one(q_ref, snew_ref, wnew_ref, tail_ref, kc_ref, vc_ref, wcache_ref, pages, ovl_ref, bexp_ref,
                    gexp_ref, o_ref, qs_ref, out_ref, *, n_wt, t_new, past_len, n_cmp, n_sel):
    n_pages = len(pages)
    G = NSA_GROUP
    R = NSA_HEADS * t_new
    half = R // 2
    _build_q(q_ref, qs_ref, t_new)
    gates = _expand_gates(tail_ref[:, 0:LANES], gexp_ref)
    zeros = jnp.zeros((half, LANES), F32)
    q_all = jnp.concatenate([jnp.concatenate([qs_ref[0:half, :], zeros], axis=1),
                             jnp.concatenate([zeros, qs_ref[half:R, :]], axis=1)], axis=0).astype(BF16)
    row_t = lax.broadcasted_iota(jnp.int32, (SROWS, NBLK), 0) & (t_new - 1)
    qpos_blk = past_len + row_t
    pad_new = lambda x: jnp.concatenate([x, jnp.zeros((KT - t_new, x.shape[1]), x.dtype)], axis=0)

    def attend(k_t_tiles, v_t_tiles, k_new, v_new, bias):
        s = [_dot(q_all, kt.astype(BF16)) for kt in k_t_tiles] + [_dot_nt(q_all, pad_new(k_new).astype(BF16))]
        s = jnp.concatenate(s, axis=1) + bias
        e = jnp.exp(s - jnp.max(s, -1, keepdims=True))
        p = (e / jnp.sum(e, -1, keepdims=True)).astype(BF16)
        n = len(k_t_tiles)
        o = _dot(p[:, n * KT:(n + 1) * KT], pad_new(v_new).astype(BF16))
        for c, vt in enumerate(v_t_tiles):
            o = o + _dot_nt(p[:, c * KT:(c + 1) * KT], vt.astype(BF16))
        return o

    def emit(o_all, gate, first):
        for h in range(NSA_KV):
            _emit_branch(out_ref, o_all[h * G * t_new:(h + 1) * G * t_new, _pair_slab(h)], gate, h, t_new, first)

    qpos_r = qpos_blk[0:R]
    blk = lax.broadcasted_iota(jnp.int32, (R, NBLK), 1)
    valid = ((blk * CMP_STRIDE + (CMP_LEN - 1) <= qpos_r) & (blk < n_cmp)).astype(F32)
    s = jnp.where(valid > 0.5, _dot_nt(q_all, kc_ref[0].astype(BF16)), NEG)
    e = jnp.exp(s - jnp.max(s, -1, keepdims=True))
    p = e / jnp.sum(e, -1, keepdims=True) * valid
    emit(_dot(p.astype(BF16), vc_ref[0].astype(BF16)), gates[0], True)

    psum = []
    for h in range(NSA_KV):
        acc = p[(h * G) * t_new:(h * G + 1) * t_new]
        for g in range(1, G):
            acc = acc + p[(h * G + g) * t_new:(h * G + g + 1) * t_new]
        psum.append(acc)
    psum = jnp.concatenate(psum + [jnp.zeros((SROWS - NSA_KV * t_new, NBLK), F32)], axis=0)
    sel = _select_blocks(psum, ovl_ref[...], qpos_blk, n_sel)
    mexp = _dot(sel.astype(BF16), bexp_ref[...])
    mexp = jnp.concatenate([_tile_rows(mexp[h * t_new:(h + 1) * t_new], G) for h in range(NSA_KV)], axis=0)
    n_keys = (n_pages + 1) * KT
    kpos = lax.broadcasted_iota(jnp.int32, (R, n_keys), 1)
    qpos_k = past_len + (lax.broadcasted_iota(jnp.int32, (R, n_keys), 0) & (t_new - 1))
    bias = jnp.where((mexp > 0.5) & (kpos <= qpos_k), 0.0, NEG)
    o_s = attend([pg[0:NSA_KVW, :] for pg in pages], [pg[NSA_KVW:ROWW, :] for pg in pages],
                 snew_ref[:, 0:NSA_KVW], snew_ref[:, NSA_KVW:ROWW], bias)
    emit(o_s, gates[1], False)

    n_keys = (n_wt + 1) * KT
    kpos = (past_len - n_wt * KT) + lax.broadcasted_iota(jnp.int32, (R, n_keys), 1)
    dist = past_len + (lax.broadcasted_iota(jnp.int32, (R, n_keys), 0) & (t_new - 1)) - kpos
    bias = jnp.where((dist >= 0) & (dist < WINDOW), 0.0, NEG)
    o_w = attend([wcache_ref[0:NSA_KVW, c * KT:(c + 1) * KT] for c in range(n_wt)],
                 [wcache_ref[NSA_KVW:ROWW, c * KT:(c + 1) * KT] for c in range(n_wt)],
                 wnew_ref[:, 0:NSA_KVW], wnew_ref[:, NSA_KVW:ROWW], bias)
    emit(o_w, gates[2], False)
    o_ref[...] = out_ref[...].astype(o_ref.dtype)


def _nsa_sample(proj, kc, vc, slc_pool_t, win_cache_t, page_table, ovl, bexp, gexp, n_seq, t_new, past_len):
    n_pages = page_table.shape[1]
    w_buf = win_cache_t.shape[1]
    assert w_buf % KT == 0 and t_new == SUBLANES and NSA_HEADS * t_new <= SROWS and PAGE_SIZE == KT
    n_wt = w_buf // KT
    n_cmp = (past_len + t_new - CMP_LEN) // CMP_STRIDE + 1
    n_sel = -(-(past_len + t_new) // SEL_LEN)
    n_sb = math.gcd(n_seq, 2)
    rows = n_sb * t_new
    page_spec = lambda sb, p: pl.BlockSpec((ROWW, PAGE_SIZE), lambda b, pt: (pt[n_sb * b + sb, p], 0))
    in_specs = [
        pl.BlockSpec((rows, NSA_QW), lambda b, pt: (b, C_NQ // NSA_QW)),
        pl.BlockSpec((rows, ROWW), lambda b, pt: (b, C_SLC // ROWW)),
        pl.BlockSpec((rows, ROWW), lambda b, pt: (b, C_WIN // ROWW)),
        pl.BlockSpec((rows, PROJ_TN), lambda b, pt: (b, C_TAIL // PROJ_TN)),
        pl.BlockSpec((n_sb, NBLK, NSA_KVW), lambda b, pt: (b, 0, 0)),
        pl.BlockSpec((n_sb, NBLK, NSA_KVW), lambda b, pt: (b, 0, 0)),
        pl.BlockSpec((n_sb * ROWW, w_buf), lambda b, pt: (b, 0)),
    ] + [page_spec(sb, p) for sb in range(n_sb) for p in range(n_pages)] + [
        pl.BlockSpec(ovl.shape, lambda b, pt: (0, 0)),
        pl.BlockSpec(bexp.shape, lambda b, pt: (0, 0)),
        pl.BlockSpec(gexp.shape, lambda b, pt: (0, 0, 0)),
    ]
    return pl.pallas_call(
        functools.partial(_nsa_sample_kernel, n_sb=n_sb, n_pages=n_pages, n_wt=n_wt, t_new=t_new,
                          past_len=past_len, n_cmp=n_cmp, n_sel=n_sel),
        grid_spec=pltpu.PrefetchScalarGridSpec(
            num_scalar_prefetch=1, grid=(n_seq // n_sb,), in_specs=in_specs,
            out_specs=pl.BlockSpec((rows, NSA_QW), lambda b, pt: (b, 0)),
            scratch_shapes=[pltpu.VMEM((n_sb * NSA_HEADS * t_new, LANES), F32),
                            pltpu.VMEM((rows, NSA_QW), F32)]),
        out_shape=jax.ShapeDtypeStruct((n_seq * t_new, NSA_QW), F32),
        compiler_params=_cparams(("parallel",)),
        name="nsa_sample",
    )(page_table, proj, proj, proj, proj, kc, vc, win_cache_t, *([slc_pool_t] * (n_sb * n_pages)), ovl, bexp, gexp)


def _out_ln_kernel(og_ref, on_ref, wg_ref, wn_ref, x_ref, g_ref, b_ref, y_ref, yb_ref, *, alpha):
    tm = x_ref.shape[0]
    rows = math.gcd(tm, 256)
    for r0 in range(0, tm, rows):
        sl = slice(r0, r0 + rows)
        m = _dot(og_ref[sl, :].astype(BF16), wg_ref[...]) + _dot(on_ref[sl, :].astype(BF16), wn_ref[...])
        y = _layer_norm(alpha * x_ref[sl, :] + m, g_ref[...], b_ref[...])
        y_ref[sl, :] = y
        yb_ref[sl, :] = y.astype(BF16)


def _out_ln(og, on, wg, wn, x, g, b, alpha, tm):
    n, d = x.shape
    assert n % tm == 0
    return pl.pallas_call(
        functools.partial(_out_ln_kernel, alpha=alpha),
        grid=(n // tm,),
        in_specs=[
            pl.BlockSpec((tm, og.shape[1]), lambda i: (i, 0)),
            pl.BlockSpec((tm, on.shape[1]), lambda i: (i, 0)),
            pl.BlockSpec(wg.shape, lambda i: (0, 0)),
            pl.BlockSpec(wn.shape, lambda i: (0, 0)),
            pl.BlockSpec((tm, d), lambda i: (i, 0)),
            pl.BlockSpec((1, d), lambda i: (0, 0)),
            pl.BlockSpec((1, d), lambda i: (0, 0)),
        ],
        out_specs=[pl.BlockSpec((tm, d), lambda i: (i, 0)), pl.BlockSpec((tm, d), lambda i: (i, 0))],
        out_shape=[jax.ShapeDtypeStruct((n, d), F32), jax.ShapeDtypeStruct((n, d), BF16)],
        compiler_params=_cparams(("parallel",)),
        name="out_ln",
    )(og, on, wg, wn, x, g, b)


def _rope_tables(pos):
    half = NSA_HD // 2
    freq = ROPE_THETA ** (-jnp.arange(half, dtype=F32) / half)
    ang = pos.astype(F32)[:, None] * freq[None, :]
    cos, sin = jnp.cos(ang), jnp.sin(ang)
    cos_h = jnp.concatenate([cos, cos], -1)
    sin_h = jnp.concatenate([-sin, sin], -1)
    reps = LANES // NSA_HD
    return jnp.tile(cos_h, (1, reps)), jnp.tile(sin_h, (1, reps))


def _reorder_w_in(w):
    d = w.shape[0]
    sizes = (GLA_KW, GLA_KW, GLA_VW, GLA_GATE_RANK, GLA_VW, NSA_QW) + (NSA_KVW,) * 6 + (NSA_HEADS * 3,)
    offs = np.concatenate([[0], np.cumsum(sizes)])
    gq, gk, gv, ga, gr, nq, kc, vc, ks, vs, kw, vw, ng = [w[:, offs[i]:offs[i + 1]] for i in range(len(sizes))]
    pad = jnp.zeros((d, PROJ_TN - GLA_GATE_RANK - NSA_HEADS * 3), w.dtype)
    return jnp.concatenate([gq, gk, gv, gr, nq, kc, vc, ks, vs, kw, vw, ga, ng, pad], axis=1)


def _block_diag2(w):
    z = jnp.zeros_like(w)
    return jnp.concatenate([jnp.concatenate([w, z], -1), jnp.concatenate([z, w], -1)], -2)


def _compress_params(pos, w1):
    half = CMP_LEN // 2
    hidden = w1.shape[-1]
    w1r = w1.reshape(2, 2, half, NSA_HD, hidden)
    eye = jnp.eye(2, dtype=w1.dtype)
    w1p = jnp.einsum("hg,palDn->plhDgan", eye, w1r).reshape(2, half * 2 * NSA_HD, 2 * 2 * hidden)
    posr = jnp.tile(pos.reshape(2, 2, half, 1, NSA_HD), (1, 1, 1, 2, 1)).reshape(2, 2, 1, half * 2 * NSA_HD)
    pos16 = jnp.tile(posr, (1, 1, SUBLANES, 1)).reshape(2, 2 * SUBLANES, half * 2 * NSA_HD)
    return pos16.astype(BF16), w1p.astype(BF16)


def _tile_size(n, pref):
    t = math.gcd(n, pref)
    assert t % SUBLANES == 0
    return t


def _const_tables(n_keys):
    cs = np.arange(NBLK)[:, None] * CMP_STRIDE
    ss = np.arange(NBLK)[None, :] * SEL_LEN
    ovl = ((cs < ss + SEL_LEN) & (cs + CMP_LEN > ss)).astype(np.float32)
    bexp = (np.arange(NBLK)[:, None] == (np.arange(n_keys)[None, :] // SEL_LEN)).astype(np.float32)
    gexp = np.zeros((3, LANES, NSA_QW), np.float32)
    for c in range(3):
        for j in range(NSA_HEADS):
            gexp[c, GATE_LANE0 + 3 * j + c, j * NSA_HD:(j + 1) * NSA_HD] = 1.0
    return jnp.asarray(ovl, BF16), jnp.asarray(bexp, BF16), jnp.asarray(gexp, BF16)


def kernel(x_prompt, x_sample, cache_cmp_kv, cache_slc_kv, cache_win_kv, state_gla, page_table,
           w_ffn1_in, w_ffn1_out, w_in, w_gla_gate2, b_gla_gate2, gla_norm_g,
           cmp_pos, cmp_w1, cmp_w2, w_out, w_ffn2_in, w_ffn2_out, ln_g, ln_b):
    B, T, D = x_prompt.shape
    DB, TS, _ = x_sample.shape
    depth = w_in.shape[0]
    n_pages = page_table.shape[1]
    past_len = n_pages * PAGE_SIZE
    w_buf = cache_win_kv.shape[2]
    alpha = (2.0 * depth) ** 0.25
    assert w_buf == WINDOW and T % TQ == 0 and past_len % KT == 0

    cos_p, sin_p = _rope_tables(jnp.arange(T, dtype=jnp.int32))
    cos_s, sin_s = _rope_tables(past_len + jnp.arange(TS, dtype=jnp.int32))
    ovl, bexp_p, gexp = _const_tables(T)
    _, bexp_s, _ = _const_tables(past_len + KT)

    hp = x_prompt.reshape(B * T, D)
    hs = x_sample.reshape(DB * TS, D)
    tm_s = _tile_size(DB * TS, 1024)
    cos_s, sin_s = (jnp.tile(t, (tm_s // TS, 1)) for t in (cos_s, sin_s))
    outs = {k: [] for k in ("cmp_p", "slc_p", "win_p", "gla_p", "cmp_s", "slc_s", "win_s", "gla_s")}

    for l in range(depth):
        wf1i, wf1o = w_ffn1_in[l].astype(BF16), w_ffn1_out[l].astype(BF16)
        wf2i, wf2o = w_ffn2_in[l].astype(BF16), w_ffn2_out[l].astype(BF16)
        wi = _reorder_w_in(w_in[l]).astype(BF16)
        wo_g, wo_n = w_out[l][:GLA_VW].astype(BF16), w_out[l][GLA_VW:].astype(BF16)
        lg, lb = ln_g[l][:, None, :], ln_b[l][:, None, :]
        w2h = jnp.zeros((LANES, GLA_KW), F32).at[:GLA_GATE_RANK].set(w_gla_gate2[l]).astype(BF16)
        b2h = b_gla_gate2[l][None, :]
        gn = gla_norm_g[l][None, :]
        pos2, w1bd = _compress_params(cmp_pos[l], cmp_w1[l])
        w2bd = _block_diag2(cmp_w2[l]).astype(BF16)

        h1, h1b = _ffn_ln(hp, wf1i, wf1o, lg[0], lb[0], alpha, _tile_size(B * T, 512), 512)
        proj, cmp_t, slc_t, win_t = _in_proj(h1b, wi, cos_p, sin_p, _tile_size(T, 1024), 1024, True)
        og, gla_state = _gla(proj, w2h, b2h, gn, jnp.zeros((B, GLA_HEADS, GLA_DK, GLA_DV), F32), B, T)
        n_grp = T // CMP_STRIDE
        n_cmp = (T - CMP_LEN) // CMP_STRIDE + 1
        kc, vc = _compress([proj], [pl.BlockSpec((T, ROWW), lambda b: (b, C_CMP // ROWW))],
                           pos2, w1bd, w2bd, B, n_grp, n_cmp)
        kc, vc = (jnp.pad(a, ((0, 0), (0, NBLK - n_grp), (0, 0))) for a in (kc, vc))
        on = _nsa_prompt(proj, kc, vc, ovl, bexp_p, gexp, B, T)
        h2, h2b = _out_ln(og, on, wo_g, wo_n, h1, lg[1], lb[1], alpha, _tile_size(B * T, 512))
        hp, _ = _ffn_ln(h2, wf2i, wf2o, lg[2], lb[2], alpha, _tile_size(B * T, 512), 512)
        row6 = lambda a, n, t: a.reshape(n, t, 2, NSA_KV, NSA_HD)
        row6_t = lambda a: a.reshape(a.shape[0], 2, NSA_KV, NSA_HD, a.shape[2]).transpose(0, 4, 1, 2, 3)
        outs["cmp_p"].append(row6_t(cmp_t))
        outs["slc_p"].append(row6_t(slc_t))
        outs["win_p"].append(row6_t(win_t[:, :, T - w_buf:]))
        outs["gla_p"].append(gla_state)

        s1, s1b = _ffn_ln(hs, wf1i, wf1o, lg[0], lb[0], alpha, _tile_size(DB * TS, 512), 512)
        (sproj,) = _in_proj(s1b, wi, cos_s, sin_s, tm_s, 1024, False)
        sog, sgla_state = _gla(sproj, w2h, b2h, gn, state_gla[l], DB, TS)
        feature_major = lambda c: c.transpose(0, 2, 3, 4, 1).reshape(c.shape[0] * ROWW, c.shape[1])
        cmp_pool_t = feature_major(cache_cmp_kv[l])
        slc_pool_t = feature_major(cache_slc_kv[l])
        win_cache_t = feature_major(cache_win_kv[l])
        n_grp_s = past_len // CMP_STRIDE
        n_cmp_s = (past_len + TS - CMP_LEN) // CMP_STRIDE + 1
        page_specs = [pl.BlockSpec((ROWW, PAGE_SIZE), lambda b, pt, p=p: (pt[b, p], 0)) for p in range(n_pages)]
        skc, svc = _compress([cmp_pool_t] * n_pages, page_specs, pos2, w1bd, w2bd, DB, n_grp_s, n_cmp_s,
                             scalar_args=(page_table,), feature_major=True)
        skc, svc = (jnp.pad(a, ((0, 0), (0, NBLK - n_grp_s), (0, 0))) for a in (skc, svc))
        son = _nsa_sample(sproj, skc, svc, slc_pool_t, win_cache_t, page_table, ovl, bexp_s, gexp, DB, TS, past_len)
        s2, s2b = _out_ln(sog, son, wo_g, wo_n, s1, lg[1], lb[1], alpha, _tile_size(DB * TS, 512))
        hs, _ = _ffn_ln(s2, wf2i, wf2o, lg[2], lb[2], alpha, _tile_size(DB * TS, 512), 512)
        new_win = row6(sproj[:, C_WIN:C_WIN + ROWW], DB, TS)
        outs["cmp_s"].append(row6(sproj[:, C_CMP:C_CMP + ROWW], DB, TS))
        outs["slc_s"].append(row6(sproj[:, C_SLC:C_SLC + ROWW], DB, TS))
        outs["win_s"].append(jnp.concatenate([cache_win_kv[l], new_win], axis=1)[:, -w_buf:])
        outs["gla_s"].append(sgla_state)

    st = lambda k: jnp.stack(outs[k])
    return (hp.reshape(B, T, D), hs.reshape(DB, TS, D),
            st("cmp_p"), st("slc_p"), st("win_p"), st("gla_p"),
            st("cmp_s"), st("slc_s"), st("win_s"), st("gla_s"))
```

```python
import functools
import math

import numpy as np
import jax
import jax.numpy as jnp
from jax import lax
from jax.experimental import pallas as pl
from jax.experimental.pallas import tpu as pltpu

F32 = jnp.float32
BF16 = jnp.bfloat16

GLA_HEADS = 4
GLA_DK = 128
GLA_DV = 256
GLA_GATE_RANK = 16
GLA_TAU = 16.0
NSA_HEADS = 16
NSA_KV = 4
NSA_GROUP = NSA_HEADS // NSA_KV
NSA_HD = 64
CMP_LEN = 32
CMP_STRIDE = 16
SEL_LEN = 64
SEL_TOPN = 16
WINDOW = 512
PAGE_SIZE = 128
ROPE_THETA = 10000.0
LN_EPS = 1e-5
NEG = -1e30

LANES = 128
SUBLANES = 8
VMEM_LIMIT_BYTES = 56 * 1024 * 1024

GLA_KW = GLA_HEADS * GLA_DK
GLA_VW = GLA_HEADS * GLA_DV
NSA_QW = NSA_HEADS * NSA_HD
NSA_KVW = NSA_KV * NSA_HD
ROWW = 2 * NSA_KVW
C_GQ = 0
C_GK = C_GQ + GLA_KW
C_GV = C_GK + GLA_KW
C_GR = C_GV + GLA_VW
C_NQ = C_GR + GLA_VW
C_CMP = C_NQ + NSA_QW
C_SLC = C_CMP + ROWW
C_WIN = C_SLC + ROWW
C_TAIL = C_WIN + ROWW
TAILW = ROWW
PROJ_W = C_TAIL + TAILW
GATE_LANE0 = GLA_GATE_RANK

FFN_TM = 512
FFN_TF = 512
PROJ_TM = 1024
PROJ_TN = 1024
OUT_TM = 512
GLA_TB = 1024
GLA_CHUNK = 128
GLA_SUB = 32
TQ = 256
SROWS = 128
KT = 128
FKT = 256
NBLK = 128


def _cparams(sem):
    return pltpu.CompilerParams(dimension_semantics=sem, vmem_limit_bytes=VMEM_LIMIT_BYTES)


def _dot(a, b):
    return jnp.dot(a, b, preferred_element_type=F32)


def _dot_nt(a, b):
    return lax.dot_general(a, b, (((1,), (1,)), ((), ())), preferred_element_type=F32)


def _layer_norm(y, g, b):
    mu = jnp.mean(y, -1, keepdims=True)
    d = y - mu
    var = jnp.mean(d * d, -1, keepdims=True)
    return d * lax.rsqrt(var + LN_EPS) * g + b


def _split_bf16(x, n):
    parts = []
    for _ in range(n - 1):
        hi = x.astype(BF16)
        parts.append(hi)
        x = x - hi.astype(F32)
    parts.append(x.astype(BF16))
    return parts


def _ffn_ln_kernel(x_ref, wg_ref, wu_ref, wo_ref, g_ref, b_ref, y_ref, yb_ref, xb_ref, acc_ref, *, alpha):
    j = pl.program_id(1)

    @pl.when(j == 0)
    def _():
        xb_ref[...] = x_ref[...].astype(BF16)
        acc_ref[...] = jnp.zeros_like(acc_ref)

    xb = xb_ref[...]
    h = _dot(xb, wg_ref[...])
    u = _dot(xb, wu_ref[...])
    a = (h * jax.nn.sigmoid(h) * u).astype(BF16)
    acc_ref[...] += _dot(a, wo_ref[...])

    @pl.when(j == pl.num_programs(1) - 1)
    def _():
        y = _layer_norm(alpha * x_ref[...] + 0.5 * acc_ref[...], g_ref[...], b_ref[...])
        y_ref[...] = y
        yb_ref[...] = y.astype(BF16)


def _ffn_ln(x, w_in, w_out, g, b, alpha, tm, tf):
    n, d = x.shape
    dff = w_out.shape[0]
    nf = dff // tf
    assert n % tm == 0 and dff % tf == 0
    return pl.pallas_call(
        functools.partial(_ffn_ln_kernel, alpha=alpha),
        grid=(n // tm, nf),
        in_specs=[
            pl.BlockSpec((tm, d), lambda i, j: (i, 0)),
            pl.BlockSpec((d, tf), lambda i, j: (0, j)),
            pl.BlockSpec((d, tf), lambda i, j: (0, j + nf)),
            pl.BlockSpec((tf, d), lambda i, j: (j, 0)),
            pl.BlockSpec((1, d), lambda i, j: (0, 0)),
            pl.BlockSpec((1, d), lambda i, j: (0, 0)),
        ],
        out_specs=[pl.BlockSpec((tm, d), lambda i, j: (i, 0)),
                   pl.BlockSpec((tm, d), lambda i, j: (i, 0))],
        out_shape=[jax.ShapeDtypeStruct((n, d), F32), jax.ShapeDtypeStruct((n, d), BF16)],
        scratch_shapes=[pltpu.VMEM((tm, d), BF16), pltpu.VMEM((tm, d), F32)],
        compiler_params=_cparams(("parallel", "arbitrary")),
        name="ffn_ln",
    )(x, w_in, w_in, w_out, g, b)


def _rope_lanes(x, cos, sin_signed):
    lane = lax.broadcasted_iota(jnp.int32, x.shape, 1)
    first_half = (lane & (NSA_HD - 1)) < (NSA_HD // 2)
    partner = jnp.where(first_half, pltpu.roll(x, LANES - NSA_HD // 2, 1), pltpu.roll(x, NSA_HD // 2, 1))
    return x * cos + partner * sin_signed


def _is_rotary_col(col):
    if C_NQ <= col < C_CMP:
        return True
    return C_CMP <= col < C_TAIL and (col - C_CMP) % ROWW < NSA_KVW


def _in_proj_kernel(x_ref, w_ref, cos_ref, sin_ref, o_ref, *t_refs, tn):
    j = pl.program_id(1)
    chunk = 2 * LANES

    for jj in range(PROJ_W // tn):
        @pl.when(j == jj)
        def _(jj=jj):
            for c0 in range(0, tn, chunk):
                acc = _dot(x_ref[...], w_ref[:, c0:c0 + chunk])
                for s0 in range(0, chunk, LANES):
                    col = jj * tn + c0 + s0
                    val = acc[:, s0:s0 + LANES]
                    if _is_rotary_col(col):
                        val = _rope_lanes(val, cos_ref[...], sin_ref[...])
                    o_ref[:, c0 + s0:c0 + s0 + LANES] = val
                    group, off = divmod(col - C_CMP, ROWW)
                    if t_refs and 0 <= group < len(t_refs):
                        t_refs[group][0, off:off + LANES, :] = val.T


def _in_proj(xb, w, cos, sin, tm, tn, emit_t):
    n, d = xb.shape
    period = cos.shape[0]
    assert n % tm == 0 and period % tm == 0 and PROJ_W % tn == 0 and tn % ROWW == 0
    npb = period // tm
    out_specs = [pl.BlockSpec((tm, tn), lambda i, j: (i, j))]
    out_shape = [jax.ShapeDtypeStruct((n, PROJ_W), F32)]
    if emit_t:
        out_specs += [pl.BlockSpec((1, ROWW, tm), lambda i, j: (i // npb, 0, i % npb))] * 3
        out_shape += [jax.ShapeDtypeStruct((n // period, ROWW, period), F32)] * 3
    return pl.pallas_call(
        functools.partial(_in_proj_kernel, tn=tn),
        grid=(n // tm, PROJ_W // tn),
        in_specs=[
            pl.BlockSpec((tm, d), lambda i, j: (i, 0)),
            pl.BlockSpec((d, tn), lambda i, j: (0, j)),
            pl.BlockSpec((tm, LANES), lambda i, j: (i % npb, 0)),
            pl.BlockSpec((tm, LANES), lambda i, j: (i % npb, 0)),
        ],
        out_specs=out_specs,
        out_shape=out_shape,
        compiler_params=_cparams(("parallel", "arbitrary")),
        name="in_proj",
    )(xb, w, cos, sin)


def _gla_kernel(q_ref, k_ref, v_ref, r_ref, ga_ref, w2_ref, b2_ref, gn_ref, s0_ref,
                o_ref, sout_ref, st_ref, *, n_valid, n_chunks, n_sb):
    t = pl.program_id(1)
    C, S, H = GLA_CHUNK, GLA_SUB, GLA_HEADS

    @pl.when(t == 0)
    def _():
        for sb in range(n_sb):
            for h in range(H):
                st_ref[sb * H + h] = s0_ref[sb, h].T

    row_k = lax.broadcasted_iota(jnp.int32, (C, GLA_DK), 0)
    row_a = lax.broadcasted_iota(jnp.int32, (C, GLA_KW), 0)
    row_c = lax.broadcasted_iota(jnp.int32, (C, C), 0)
    col_c = lax.broadcasted_iota(jnp.int32, (C, C), 1)
    causal = row_c >= col_c
    tri = causal.astype(BF16)

    def load(ref, r0, sb, c0, w):
        if n_valid >= C:
            return ref[pl.ds(r0, C), c0:c0 + w]
        x = ref[sb * n_valid:(sb + 1) * n_valid, c0:c0 + w]
        return jnp.concatenate([x, jnp.zeros((C - n_valid, w), x.dtype)], axis=0)

    def chunk(c, carry):
        r0 = pl.multiple_of(c * C, C)
        for sb in range(n_sb):
            pre = _dot(load(ga_ref, r0, sb, 0, LANES).astype(BF16), w2_ref[...]) + b2_ref[...]
            la = (jnp.minimum(pre, 0.0) - jnp.log(1.0 + jnp.exp(-jnp.abs(pre)))) * (1.0 / GLA_TAU)
            if n_valid < C:
                la = jnp.where(row_a < n_valid, la, 0.0)
            la_hi, la_lo = _split_bf16(la, 2)
            b_all = _dot(tri, la_hi) + _dot(tri, la_lo)
            for h in range(H):
                b = b_all[:, h * GLA_DK:(h + 1) * GLA_DK]
                q = load(q_ref, r0, sb, h * GLA_DK, GLA_DK) * (GLA_DK ** -0.5)
                k = load(k_ref, r0, sb, h * GLA_DK, GLA_DK)
                v = load(v_ref, r0, sb, h * GLA_DV, GLA_DV)
                b_last = b[C - 1:C, :]

                strips = []
                n_strips = min(C // S, -(-n_valid // S))
                for i in range(n_strips):
                    rows = slice(i * S, (i + 1) * S)
                    anchor = jnp.zeros((1, GLA_DK), F32) if i == 0 else b[i * S - 1:i * S, :]
                    qi = q[rows] * jnp.exp(b[rows] - anchor)
                    ki = k * jnp.exp(jnp.where(row_k < (i + 1) * S, anchor - b, 0.0))
                    strips.append(_dot_nt(qi.astype(BF16), ki.astype(BF16)))
                if n_strips * S < C:
                    strips.append(jnp.zeros((C - n_strips * S, C), F32))
                attn = jnp.where(causal, jnp.concatenate(strips, axis=0), 0.0)

                st = st_ref[sb * H + h]
                o = (_dot_nt((q * jnp.exp(b)).astype(BF16), st.astype(BF16))
                     + _dot(attn.astype(BF16), v.astype(BF16)))
                ke = k * jnp.exp(b_last - b)
                st_ref[sb * H + h] = st * jnp.exp(b_last) + _dot(v.T.astype(BF16), ke.astype(BF16))

                rms = lax.rsqrt(jnp.mean(o * o, -1, keepdims=True) + LN_EPS)
                r = load(r_ref, r0, sb, h * GLA_DV, GLA_DV)
                out = (o * rms * gn_ref[...] * (r * jax.nn.sigmoid(r))).astype(o_ref.dtype)
                cols = slice(h * GLA_DV, (h + 1) * GLA_DV)
                if n_valid >= C:
                    o_ref[pl.ds(r0, C), cols] = out
                else:
                    o_ref[sb * n_valid:(sb + 1) * n_valid, cols] = out[:n_valid]
        return carry

    lax.fori_loop(0, n_chunks, chunk, 0)

    @pl.when(t == pl.num_programs(1) - 1)
    def _():
        for sb in range(n_sb):
            for h in range(H):
                sout_ref[sb, h] = st_ref[sb * H + h].T


def _gla(proj, w2h, b2h, gn, s0, n_seq, t_seq):
    C, H = GLA_CHUNK, GLA_HEADS
    if t_seq >= C:
        tb = math.gcd(t_seq, GLA_TB)
        assert tb % C == 0
        n_valid, n_chunks, n_sb = C, tb // C, 1
    else:
        tb, n_valid, n_chunks = t_seq, t_seq, 1
        n_sb = math.gcd(n_seq, 2)
        assert t_seq % SUBLANES == 0
    nt = t_seq // tb
    rows = n_sb * tb
    rowblk = lambda b, t: b * nt + t
    return pl.pallas_call(
        functools.partial(_gla_kernel, n_valid=n_valid, n_chunks=n_chunks, n_sb=n_sb),
        grid=(n_seq // n_sb, nt),
        in_specs=[
            pl.BlockSpec((rows, GLA_KW), lambda b, t: (rowblk(b, t), C_GQ // GLA_KW)),
            pl.BlockSpec((rows, GLA_KW), lambda b, t: (rowblk(b, t), C_GK // GLA_KW)),
            pl.BlockSpec((rows, GLA_VW), lambda b, t: (rowblk(b, t), C_GV // GLA_VW)),
            pl.BlockSpec((rows, GLA_VW), lambda b, t: (rowblk(b, t), C_GR // GLA_VW)),
            pl.BlockSpec((rows, LANES), lambda b, t: (rowblk(b, t), C_TAIL // LANES)),
            pl.BlockSpec((LANES, GLA_KW), lambda b, t: (0, 0)),
            pl.BlockSpec((1, GLA_KW), lambda b, t: (0, 0)),
            pl.BlockSpec((1, GLA_DV), lambda b, t: (0, 0)),
            pl.BlockSpec((n_sb, H, GLA_DK, GLA_DV), lambda b, t: (b, 0, 0, 0)),
        ],
        out_specs=[
            pl.BlockSpec((rows, GLA_VW), lambda b, t: (rowblk(b, t), 0)),
            pl.BlockSpec((n_sb, H, GLA_DK, GLA_DV), lambda b, t: (b, 0, 0, 0)),
        ],
        out_shape=[jax.ShapeDtypeStruct((n_seq * t_seq, GLA_VW), BF16 if t_seq >= C else F32),
                   jax.ShapeDtypeStruct((n_seq, H, GLA_DK, GLA_DV), F32)],
        scratch_shapes=[pltpu.VMEM((n_sb * H, GLA_DV, GLA_DK), F32)],
        compiler_params=_cparams(("parallel", "arbitrary")),
        name="gla",
    )(proj, proj, proj, proj, proj, w2h, b2h, gn, s0)


def _gelu_tanh(x):
    return 0.5 * x * (1.0 + jnp.tanh(math.sqrt(2.0 / math.pi) * (x + 0.044715 * (x * x * x))))


def _compress_kernel(*refs, n_scalar, n_src, n_cmp, feature_major):
    refs = refs[n_scalar:]
    src = refs[:n_src]
    pos_ref, w1_ref, w2_ref, kc_ref, vc_ref, slab_ref, u_ref = refs[n_src:]
    r_src = src[0].shape[1] if feature_major else src[0].shape[0]
    n_grp = n_src * r_src // CMP_STRIDE
    row = lax.broadcasted_iota(jnp.int32, (n_grp, LANES), 0)
    outs = (kc_ref, vc_ref)
    hw = 2 * LANES
    for part in range(2):
        pos_bias = _dot(pos_ref[part], w1_ref[part])
        for pair in range(NSA_KVW // LANES):
            c0 = part * NSA_KVW + pair * LANES
            for n, s in enumerate(src):
                slab_ref[n * r_src:(n + 1) * r_src, :] = s[c0:c0 + LANES, :].T if feature_major else s[:, c0:c0 + LANES]
            for l in range(CMP_STRIDE):
                u_ref[:, l * LANES:(l + 1) * LANES] = slab_ref[pl.ds(l, n_grp, stride=CMP_STRIDE), :].astype(BF16)
            ab = _dot(u_ref[...], w1_ref[part])
            hidden = []
            for hh in range(2):
                a = ab[:, hh * hw:hh * hw + LANES] + pos_bias[0:1, hh * hw:hh * hw + LANES]
                b = ab[:, hh * hw + LANES:(hh + 1) * hw] + pos_bias[SUBLANES:SUBLANES + 1, hh * hw + LANES:(hh + 1) * hw]
                hidden.append(a + pltpu.roll(b, n_grp - 1, 0))
            out = _dot(_gelu_tanh(jnp.concatenate(hidden, axis=1)).astype(BF16), w2_ref[part])
            outs[part][0, :, pair * LANES:(pair + 1) * LANES] = jnp.where(row < n_cmp, out, 0.0)


def _compress(srcs, src_specs, pos2, w1bd, w2bd, n_seq, n_grp, n_cmp, scalar_args=(), feature_major=False):
    n_src = len(srcs)
    nsp = len(scalar_args)
    in_specs = list(src_specs) + [
        pl.BlockSpec(pos2.shape, lambda b, *_: (0, 0, 0)),
        pl.BlockSpec(w1bd.shape, lambda b, *_: (0, 0, 0)),
        pl.BlockSpec(w2bd.shape, lambda b, *_: (0, 0, 0)),
    ]
    out_spec = pl.BlockSpec((1, n_grp, NSA_KVW), lambda b, *_: (b, 0, 0))
    return pl.pallas_call(
        functools.partial(_compress_kernel, n_scalar=nsp, n_src=n_src, n_cmp=n_cmp, feature_major=feature_major),
        grid_spec=pltpu.PrefetchScalarGridSpec(
            num_scalar_prefetch=nsp, grid=(n_seq,), in_specs=in_specs, out_specs=[out_spec, out_spec],
            scratch_shapes=[pltpu.VMEM((n_grp * CMP_STRIDE, LANES), F32),
                            pltpu.VMEM((n_grp, CMP_STRIDE * LANES), BF16)]),
        out_shape=[jax.ShapeDtypeStruct((n_seq, n_grp, NSA_KVW), F32)] * 2,
        compiler_params=_cparams(("parallel",)),
        name="compress",
    )(*scalar_args, *srcs, pos2, w1bd, w2bd)


def _build_q(q_ref, qs_ref, tq):
    lane = lax.broadcasted_iota(jnp.int32, (tq, LANES), 1)
    for j in range(NSA_HEADS):
        h = j // NSA_GROUP
        x = q_ref[:, (j // 2) * LANES:(j // 2 + 1) * LANES] * (NSA_HD ** -0.5)
        if (j % 2) != (h % 2):
            x = pltpu.roll(x, NSA_HD, 1)
        keep = (lane >= NSA_HD) if (h % 2) else (lane < NSA_HD)
        qs_ref[j * tq:(j + 1) * tq, :] = jnp.where(keep, x, 0.0).astype(qs_ref.dtype)


def _pair_slab(h):
    return slice((h // 2) * LANES, (h // 2 + 1) * LANES)


def _tile_rows(x, n):
    return jnp.concatenate([x] * n, axis=0)


def _cmp_attention(qh, kc, vc, qpos, n_cmp):
    tq = qpos.shape[0]
    blk = lax.broadcasted_iota(jnp.int32, (tq, NBLK), 1)
    valid = ((blk * CMP_STRIDE + (CMP_LEN - 1) <= qpos) & (blk < n_cmp)).astype(F32)
    valid4 = _tile_rows(valid, NSA_GROUP)
    s = jnp.where(valid4 > 0.5, _dot_nt(qh, kc.astype(BF16)), NEG)
    e = jnp.exp(s - jnp.max(s, -1, keepdims=True))
    p = e * (valid4 * (1.0 / jnp.sum(e, -1, keepdims=True)))
    o = _dot(p.astype(BF16), vc.astype(BF16))
    psum = p[0:tq]
    for g in range(1, NSA_GROUP):
        psum = psum + p[g * tq:(g + 1) * tq]
    return o, psum


def _select_blocks(psum, ovl, qpos, n_sel):
    imp = sum(_dot(t, ovl) for t in _split_bf16(psum, 3))
    blk = lax.broadcasted_iota(jnp.int32, psum.shape, 1)
    ss = blk * SEL_LEN
    cur = qpos - (qpos & (SEL_LEN - 1))
    forced = (ss == 0) | (ss == cur) | (ss == cur - SEL_LEN)
    score = jnp.where(ss <= qpos, jnp.where(forced, 1e9, imp), -1e9)
    nb = -(-n_sel // SUBLANES) * SUBLANES
    st = score.T[0:nb]
    sub = lax.broadcasted_iota(jnp.int32, st.shape, 0)
    rank = jnp.zeros(st.shape, F32)
    for s in range(n_sel):
        other = st[s:s + 1, :]
        beats = (other > st) | ((other == st) & (sub > s))
        rank = rank + beats.astype(F32)
    sel_t = ((rank < SEL_TOPN) & (st > -1e8)).astype(F32)
    sel_t = jnp.concatenate([sel_t, jnp.zeros((NBLK - nb, st.shape[1]), F32)], axis=0)
    return sel_t.T


def _emit_branch(out_ref, o, gate, h, tq, first, denom_in_other_half=False):
    lane = lax.broadcasted_iota(jnp.int32, (tq, LANES), 1)
    for gp in range(NSA_GROUP // 2):
        y0 = o[(2 * gp) * tq:(2 * gp + 1) * tq]
        y1 = o[(2 * gp + 1) * tq:(2 * gp + 2) * tq]
        if h % 2:
            y0 = pltpu.roll(y0, NSA_HD, 1)
        else:
            y1 = pltpu.roll(y1, NSA_HD, 1)
        m = 2 * h + gp
        sl = slice(m * LANES, (m + 1) * LANES)
        val = jnp.where(lane < NSA_HD, y0, y1)
        if denom_in_other_half:
            val = val / pltpu.roll(jnp.where(lane < NSA_HD, y1, y0), NSA_HD, 1)
        val = val * gate[:, sl]
        if first:
            out_ref[:, sl] = val
        else:
            out_ref[:, sl] += val


def _expand_gates(tail, gexp_ref):
    g_hi, g_lo = _split_bf16(jax.nn.sigmoid(tail), 2)
    return [_dot(g_hi, gexp_ref[c]) + _dot(g_lo, gexp_ref[c]) for c in range(3)]


def _own_half(h, shape):
    lane = lax.broadcasted_iota(jnp.int32, shape, len(shape) - 1)
    return (lane >= NSA_HD) if (h % 2) else (lane < NSA_HD)


def _nsa_prompt_kernel(q_ref, slc_ref, win_ref, tail_ref, kc_ref, vc_ref, ovl_ref, bexp_ref, gexp_ref,
                       o_ref, qs_ref, mexp_ref, m_ref, acc_ref, out_ref, *, n_cmp, n_sel):
    i = pl.program_id(1)
    q0 = i * TQ
    G = NSA_GROUP
    R = G * TQ
    _build_q(q_ref, qs_ref, TQ)
    gates = _expand_gates(tail_ref[:, 0:LANES], gexp_ref)
    qpos = q0 + lax.broadcasted_iota(jnp.int32, (TQ, NBLK), 0)
    qpos_k = q0 + lax.broadcasted_iota(jnp.int32, (TQ, FKT), 0)
    lane_k = lax.broadcasted_iota(jnp.int32, (TQ, FKT), 1)
    n_kt = mexp_ref.shape[1]

    for h in range(NSA_KV):
        qh = qs_ref[h * R:(h + 1) * R, :]
        o_c, psum = _cmp_attention(qh, kc_ref[0, :, _pair_slab(h)], vc_ref[0, :, _pair_slab(h)], qpos, n_cmp)
        _emit_branch(out_ref, o_c, gates[0], h, TQ, True)
        sel = _select_blocks(psum, ovl_ref[...], qpos, n_sel)
        mexp = _dot(sel.astype(BF16), bexp_ref[...])
        for kt in range(n_kt):
            mexp_ref[h, kt] = mexp[:, kt * FKT:(kt + 1) * FKT].astype(BF16)

    def flash(src_ref, lo, hi, bias_fn):
        m_ref[...] = jnp.full(m_ref.shape, NEG, F32)
        acc_ref[...] = jnp.zeros(acc_ref.shape, F32)

        def body(kt, carry):
            k0 = pl.multiple_of(kt * FKT, FKT)
            kpos = k0 + lane_k
            kslab = [src_ref[pl.ds(k0, FKT), m * LANES:(m + 1) * LANES].astype(BF16) for m in range(2)]
            vslab = [src_ref[pl.ds(k0, FKT), NSA_KVW + m * LANES:NSA_KVW + (m + 1) * LANES] for m in range(2)]
            for h in range(NSA_KV):
                vv = jnp.where(_own_half(h, (FKT, LANES)), vslab[h // 2], 1.0).astype(BF16)
                s = _dot_nt(qs_ref[h * R:(h + 1) * R, :], kslab[h // 2]) + _tile_rows(bias_fn(h, kt, kpos), G)
                m_prev = m_ref[h]
                smax = s[:, 0:LANES]
                for c in range(1, FKT // LANES):
                    smax = jnp.maximum(smax, s[:, c * LANES:(c + 1) * LANES])
                m_new = jnp.maximum(m_prev, jnp.max(smax, -1, keepdims=True))
                alpha = jnp.exp(m_prev - m_new)
                p = jnp.exp(s - jnp.concatenate([m_new] * (FKT // LANES), axis=1))
                acc_ref[h] = alpha * acc_ref[h] + _dot(p.astype(BF16), vv)
                m_ref[h] = m_new
            return carry

        lax.fori_loop(lo, hi, body, 0)

    def sel_bias(h, kt, kpos):
        return jnp.where((mexp_ref[h, kt].astype(F32) > 0.5) & (kpos <= qpos_k), 0.0, NEG)

    def win_bias(h, kt, kpos):
        dist = qpos_k - kpos
        return jnp.where((dist >= 0) & (dist < WINDOW), 0.0, NEG)

    hi = (q0 + TQ + FKT - 1) // FKT
    flash(slc_ref, 0, hi, sel_bias)
    for h in range(NSA_KV):
        _emit_branch(out_ref, acc_ref[h], gates[1], h, TQ, False, denom_in_other_half=True)
    flash(win_ref, jnp.maximum(q0 - WINDOW, 0) // FKT, hi, win_bias)
    for h in range(NSA_KV):
        _emit_branch(out_ref, acc_ref[h], gates[2], h, TQ, False, denom_in_other_half=True)
    o_ref[...] = out_ref[...].astype(o_ref.dtype)


def _nsa_prompt(proj, kc, vc, ovl, bexp, gexp, n_seq, t_seq):
    assert t_seq % FKT == 0 and t_seq % TQ == 0 and TQ % LANES == 0 and NBLK == LANES
    nq = t_seq // TQ
    n_cmp = (t_seq - CMP_LEN) // CMP_STRIDE + 1
    n_sel = -(-t_seq // SEL_LEN)
    G = NSA_GROUP
    return pl.pallas_call(
        functools.partial(_nsa_prompt_kernel, n_cmp=n_cmp, n_sel=n_sel),
        grid=(n_seq, nq),
        in_specs=[
            pl.BlockSpec((TQ, NSA_QW), lambda b, i: (b * nq + i, C_NQ // NSA_QW)),
            pl.BlockSpec((t_seq, ROWW), lambda b, i: (b, C_SLC // ROWW)),
            pl.BlockSpec((t_seq, ROWW), lambda b, i: (b, C_WIN // ROWW)),
            pl.BlockSpec((TQ, TAILW), lambda b, i: (b * nq + i, C_TAIL // TAILW)),
            pl.BlockSpec((1, NBLK, NSA_KVW), lambda b, i: (b, 0, 0)),
            pl.BlockSpec((1, NBLK, NSA_KVW), lambda b, i: (b, 0, 0)),
            pl.BlockSpec(ovl.shape, lambda b, i: (0, 0)),
            pl.BlockSpec(bexp.shape, lambda b, i: (0, 0)),
            pl.BlockSpec(gexp.shape, lambda b, i: (0, 0, 0)),
        ],
        out_specs=pl.BlockSpec((TQ, NSA_QW), lambda b, i: (b * nq + i, 0)),
        out_shape=jax.ShapeDtypeStruct((n_seq * t_seq, NSA_QW), BF16),
        scratch_shapes=[
            pltpu.VMEM((NSA_HEADS * TQ, LANES), BF16),
            pltpu.VMEM((NSA_KV, t_seq // FKT, TQ, FKT), BF16),
            pltpu.VMEM((NSA_KV, G * TQ, LANES), F32),
            pltpu.VMEM((NSA_KV, G * TQ, LANES), F32),
            pltpu.VMEM((TQ, NSA_QW), F32),
        ],
        compiler_params=_cparams(("parallel", "arbitrary")),
        name="nsa_prompt",
    )(proj, proj, proj, proj, kc, vc, ovl, bexp, gexp)


def _nsa_sample_kernel(*refs, n_sb, n_pages, t_new, **kw):
    q_ref, snew_ref, wnew_ref, tail_ref, kc_ref, vc_ref, wcache_ref = refs[1:8]
    pages = refs[8:8 + n_sb * n_pages]
    ovl_ref, bexp_ref, gexp_ref, o_ref, qs_ref, out_ref = refs[8 + n_sb * n_pages:]
    for sb in range(n_sb):
        rows = pl.ds(sb * t_new, t_new)
        _nsa_sample_one(q_ref.at[rows], snew_ref.at[rows], wnew_ref.at[rows], tail_ref.at[rows],
                        kc_ref.at[pl.ds(sb, 1)], vc_ref.at[pl.ds(sb, 1)], wcache_ref.at[pl.ds(sb * ROWW, ROWW)],
                        pages[sb * n_pages:(sb + 1) * n_pages], ovl_ref, bexp_ref, gexp_ref, o_ref.at[rows],
                        qs_ref.at[pl.ds(sb * NSA_HEADS * t_new, NSA_HEADS * t_new)], out_ref.at[rows],
                        t_new=t_new, **kw)


def _nsa_sample_one(q_ref, snew_ref, wnew_ref, tail_ref, kc_ref, vc_ref, wcache_ref, pages, ovl_ref, bexp_ref,
                    gexp_ref, o_ref, qs_ref, out_ref, *, n_wt, t_new, past_len, n_cmp, n_sel):
    n_pages = len(pages)
    G = NSA_GROUP
    R = NSA_HEADS * t_new
    half = R // 2
    _build_q(q_ref, qs_ref, t_new)
    gates = _expand_gates(tail_ref[:, 0:LANES], gexp_ref)
    zeros = jnp.zeros((half, LANES), F32)
    q_all = jnp.concatenate([jnp.concatenate([qs_ref[0:half, :], zeros], axis=1),
                             jnp.concatenate([zeros, qs_ref[half:R, :]], axis=1)], axis=0).astype(BF16)
    row_t = lax.broadcasted_iota(jnp.int32, (SROWS, NBLK), 0) & (t_new - 1)
    qpos_blk = past_len + row_t
    pad_new = lambda x: jnp.concatenate([x, jnp.zeros((KT - t_new, x.shape[1]), x.dtype)], axis=0)

    def attend(k_t_tiles, v_t_tiles, k_new, v_new, bias):
        s = [_dot(q_all, kt.astype(BF16)) for kt in k_t_tiles] + [_dot_nt(q_all, pad_new(k_new).astype(BF16))]
        s = jnp.concatenate(s, axis=1) + bias
        e = jnp.exp(s - jnp.max(s, -1, keepdims=True))
        p = (e / jnp.sum(e, -1, keepdims=True)).astype(BF16)
        n = len(k_t_tiles)
        o = _dot(p[:, n * KT:(n + 1) * KT], pad_new(v_new).astype(BF16))
        for c, vt in enumerate(v_t_tiles):
            o = o + _dot_nt(p[:, c * KT:(c + 1) * KT], vt.astype(BF16))
        return o

    def emit(o_all, gate, first):
        for h in range(NSA_KV):
            _emit_branch(out_ref, o_all[h * G * t_new:(h + 1) * G * t_new, _pair_slab(h)], gate, h, t_new, first)

    qpos_r = qpos_blk[0:R]
    blk = lax.broadcasted_iota(jnp.int32, (R, NBLK), 1)
    valid = ((blk * CMP_STRIDE + (CMP_LEN - 1) <= qpos_r) & (blk < n_cmp)).astype(F32)
    s = jnp.where(valid > 0.5, _dot_nt(q_all, kc_ref[0].astype(BF16)), NEG)
    e = jnp.exp(s - jnp.max(s, -1, keepdims=True))
    p = e / jnp.sum(e, -1, keepdims=True) * valid
    emit(_dot(p.astype(BF16), vc_ref[0].astype(BF16)), gates[0], True)

    psum = []
    for h in range(NSA_KV):
        acc = p[(h * G) * t_new:(h * G + 1) * t_new]
        for g in range(1, G):
            acc = acc + p[(h * G + g) * t_new:(h * G + g + 1) * t_new]
        psum.append(acc)
    psum = jnp.concatenate(psum + [jnp.zeros((SROWS - NSA_KV * t_new, NBLK), F32)], axis=0)
    sel = _select_blocks(psum, ovl_ref[...], qpos_blk, n_sel)
    mexp = _dot(sel.astype(BF16), bexp_ref[...])
    mexp = jnp.concatenate([_tile_rows(mexp[h * t_new:(h + 1) * t_new], G) for h in range(NSA_KV)], axis=0)
    n_keys = (n_pages + 1) * KT
    kpos = lax.broadcasted_iota(jnp.int32, (R, n_keys), 1)
    qpos_k = past_len + (lax.broadcasted_iota(jnp.int32, (R, n_keys), 0) & (t_new - 1))
    bias = jnp.where((mexp > 0.5) & (kpos <= qpos_k), 0.0, NEG)
    o_s = attend([pg[0:NSA_KVW, :] for pg in pages], [pg[NSA_KVW:ROWW, :] for pg in pages],
                 snew_ref[:, 0:NSA_KVW], snew_ref[:, NSA_KVW:ROWW], bias)
    emit(o_s, gates[1], False)

    n_keys = (n_wt + 1) * KT
    kpos = (past_len - n_wt * KT) + lax.broadcasted_iota(jnp.int32, (R, n_keys), 1)
    dist = past_len + (lax.broadcasted_iota(jnp.int32, (R, n_keys), 0) & (t_new - 1)) - kpos
    bias = jnp.where((dist >= 0) & (dist < WINDOW), 0.0, NEG)
    o_w = attend([wcache_ref[0:NSA_KVW, c * KT:(c + 1) * KT] for c in range(n_wt)],
                 [wcache_ref[NSA_KVW:ROWW, c * KT:(c + 1) * KT] for c in range(n_wt)],
                 wnew_ref[:, 0:NSA_KVW], wnew_ref[:, NSA_KVW:ROWW], bias)
    emit(o_w, gates[2], False)
    o_ref[...] = out_ref[...].astype(o_ref.dtype)


def _nsa_sample(proj, kc, vc, slc_pool_t, win_cache_t, page_table, ovl, bexp, gexp, n_seq, t_new, past_len):
    n_pages = page_table.shape[1]
    w_buf = win_cache_t.shape[1]
    assert w_buf % KT == 0 and t_new == SUBLANES and NSA_HEADS * t_new <= SROWS and PAGE_SIZE == KT
    n_wt = w_buf // KT
    n_cmp = (past_len + t_new - CMP_LEN) // CMP_STRIDE + 1
    n_sel = -(-(past_len + t_new) // SEL_LEN)
    n_sb = math.gcd(n_seq, 2)
    rows = n_sb * t_new
    page_spec = lambda sb, p: pl.BlockSpec((ROWW, PAGE_SIZE), lambda b, pt: (pt[n_sb * b + sb, p], 0))
    in_specs = [
        pl.BlockSpec((rows, NSA_QW), lambda b, pt: (b, C_NQ // NSA_QW)),
        pl.BlockSpec((rows, ROWW), lambda b, pt: (b, C_SLC // ROWW)),
        pl.BlockSpec((rows, ROWW), lambda b, pt: (b, C_WIN // ROWW)),
        pl.BlockSpec((rows, TAILW), lambda b, pt: (b, C_TAIL // TAILW)),
        pl.BlockSpec((n_sb, NBLK, NSA_KVW), lambda b, pt: (b, 0, 0)),
        pl.BlockSpec((n_sb, NBLK, NSA_KVW), lambda b, pt: (b, 0, 0)),
        pl.BlockSpec((n_sb * ROWW, w_buf), lambda b, pt: (b, 0)),
    ] + [page_spec(sb, p) for sb in range(n_sb) for p in range(n_pages)] + [
        pl.BlockSpec(ovl.shape, lambda b, pt: (0, 0)),
        pl.BlockSpec(bexp.shape, lambda b, pt: (0, 0)),
        pl.BlockSpec(gexp.shape, lambda b, pt: (0, 0, 0)),
    ]
    return pl.pallas_call(
        functools.partial(_nsa_sample_kernel, n_sb=n_sb, n_pages=n_pages, n_wt=n_wt, t_new=t_new,
                          past_len=past_len, n_cmp=n_cmp, n_sel=n_sel),
        grid_spec=pltpu.PrefetchScalarGridSpec(
            num_scalar_prefetch=1, grid=(n_seq // n_sb,), in_specs=in_specs,
            out_specs=pl.BlockSpec((rows, NSA_QW), lambda b, pt: (b, 0)),
            scratch_shapes=[pltpu.VMEM((n_sb * NSA_HEADS * t_new, LANES), F32),
                            pltpu.VMEM((rows, NSA_QW), F32)]),
        out_shape=jax.ShapeDtypeStruct((n_seq * t_new, NSA_QW), F32),
        compiler_params=_cparams(("parallel",)),
        name="nsa_sample",
    )(page_table, proj, proj, proj, proj, kc, vc, win_cache_t, *([slc_pool_t] * (n_sb * n_pages)), ovl, bexp, gexp)


def _out_ln_kernel(og_ref, on_ref, wg_ref, wn_ref, x_ref, g_ref, b_ref, y_ref, yb_ref, *, alpha):
    tm = x_ref.shape[0]
    rows = math.gcd(tm, 256)
    for r0 in range(0, tm, rows):
        sl = slice(r0, r0 + rows)
        m = _dot(og_ref[sl, :].astype(BF16), wg_ref[...]) + _dot(on_ref[sl, :].astype(BF16), wn_ref[...])
        y = _layer_norm(alpha * x_ref[sl, :] + m, g_ref[...], b_ref[...])
        y_ref[sl, :] = y
        yb_ref[sl, :] = y.astype(BF16)


def _out_ln(og, on, wg, wn, x, g, b, alpha, tm):
    n, d = x.shape
    assert n % tm == 0
    return pl.pallas_call(
        functools.partial(_out_ln_kernel, alpha=alpha),
        grid=(n // tm,),
        in_specs=[
            pl.BlockSpec((tm, og.shape[1]), lambda i: (i, 0)),
            pl.BlockSpec((tm, on.shape[1]), lambda i: (i, 0)),
            pl.BlockSpec(wg.shape, lambda i: (0, 0)),
            pl.BlockSpec(wn.shape, lambda i: (0, 0)),
            pl.BlockSpec((tm, d), lambda i: (i, 0)),
            pl.BlockSpec((1, d), lambda i: (0, 0)),
            pl.BlockSpec((1, d), lambda i: (0, 0)),
        ],
        out_specs=[pl.BlockSpec((tm, d), lambda i: (i, 0)), pl.BlockSpec((tm, d), lambda i: (i, 0))],
        out_shape=[jax.ShapeDtypeStruct((n, d), F32), jax.ShapeDtypeStruct((n, d), BF16)],
        compiler_params=_cparams(("parallel",)),
        name="out_ln",
    )(og, on, wg, wn, x, g, b)


def _rope_tables(pos):
    half = NSA_HD // 2
    freq = ROPE_THETA ** (-jnp.arange(half, dtype=F32) / half)
    ang = pos.astype(F32)[:, None] * freq[None, :]
    cos, sin = jnp.cos(ang), jnp.sin(ang)
    cos_h = jnp.concatenate([cos, cos], -1)
    sin_h = jnp.concatenate([-sin, sin], -1)
    reps = LANES // NSA_HD
    return jnp.tile(cos_h, (1, reps)), jnp.tile(sin_h, (1, reps))


def _reorder_w_in(w):
    d = w.shape[0]
    sizes = (GLA_KW, GLA_KW, GLA_VW, GLA_GATE_RANK, GLA_VW, NSA_QW) + (NSA_KVW,) * 6 + (NSA_HEADS * 3,)
    offs = np.concatenate([[0], np.cumsum(sizes)])
    gq, gk, gv, ga, gr, nq, kc, vc, ks, vs, kw, vw, ng = [w[:, offs[i]:offs[i + 1]] for i in range(len(sizes))]
    pad = jnp.zeros((d, TAILW - GLA_GATE_RANK - NSA_HEADS * 3), w.dtype)
    return jnp.concatenate([gq, gk, gv, gr, nq, kc, vc, ks, vs, kw, vw, ga, ng, pad], axis=1)


def _block_diag2(w):
    z = jnp.zeros_like(w)
    return jnp.concatenate([jnp.concatenate([w, z], -1), jnp.concatenate([z, w], -1)], -2)


def _compress_params(pos, w1):
    half = CMP_LEN // 2
    hidden = w1.shape[-1]
    w1r = w1.reshape(2, 2, half, NSA_HD, hidden)
    eye = jnp.eye(2, dtype=w1.dtype)
    w1p = jnp.einsum("hg,palDn->plhDgan", eye, w1r).reshape(2, half * 2 * NSA_HD, 2 * 2 * hidden)
    posr = jnp.tile(pos.reshape(2, 2, half, 1, NSA_HD), (1, 1, 1, 2, 1)).reshape(2, 2, 1, half * 2 * NSA_HD)
    pos16 = jnp.tile(posr, (1, 1, SUBLANES, 1)).reshape(2, 2 * SUBLANES, half * 2 * NSA_HD)
    return pos16.astype(BF16), w1p.astype(BF16)


def _tile_size(n, pref):
    t = math.gcd(n, pref)
    assert t % SUBLANES == 0
    return t


def _const_tables(n_keys):
    cs = np.arange(NBLK)[:, None] * CMP_STRIDE
    ss = np.arange(NBLK)[None, :] * SEL_LEN
    ovl = ((cs < ss + SEL_LEN) & (cs + CMP_LEN > ss)).astype(np.float32)
    bexp = (np.arange(NBLK)[:, None] == (np.arange(n_keys)[None, :] // SEL_LEN)).astype(np.float32)
    gexp = np.zeros((3, LANES, NSA_QW), np.float32)
    for c in range(3):
        for j in range(NSA_HEADS):
            gexp[c, GATE_LANE0 + 3 * j + c, j * NSA_HD:(j + 1) * NSA_HD] = 1.0
    return jnp.asarray(ovl, BF16), jnp.asarray(bexp, BF16), jnp.asarray(gexp, BF16)


def kernel(x_prompt, x_sample, cache_cmp_kv, cache_slc_kv, cache_win_kv, state_gla, page_table,
           w_ffn1_in, w_ffn1_out, w_in, w_gla_gate2, b_gla_gate2, gla_norm_g,
           cmp_pos, cmp_w1, cmp_w2, w_out, w_ffn2_in, w_ffn2_out, ln_g, ln_b):
    B, T, D = x_prompt.shape
    DB, TS, _ = x_sample.shape
    depth = w_in.shape[0]
    n_pages = page_table.shape[1]
    past_len = n_pages * PAGE_SIZE
    w_buf = cache_win_kv.shape[2]
    alpha = (2.0 * depth) ** 0.25
    assert w_buf == WINDOW and T % TQ == 0 and past_len % KT == 0

    cos_p, sin_p = _rope_tables(jnp.arange(T, dtype=jnp.int32))
    cos_s, sin_s = _rope_tables(past_len + jnp.arange(TS, dtype=jnp.int32))
    ovl, bexp_p, gexp = _const_tables(T)
    _, bexp_s, _ = _const_tables(past_len + KT)

    hp = x_prompt.reshape(B * T, D)
    hs = x_sample.reshape(DB * TS, D)
    tm_s = _tile_size(DB * TS, PROJ_TM)
    cos_s, sin_s = (jnp.tile(t, (tm_s // TS, 1)) for t in (cos_s, sin_s))
    outs = {k: [] for k in ("cmp_p", "slc_p", "win_p", "gla_p", "cmp_s", "slc_s", "win_s", "gla_s")}

    for l in range(depth):
        wf1i, wf1o = w_ffn1_in[l].astype(BF16), w_ffn1_out[l].astype(BF16)
        wf2i, wf2o = w_ffn2_in[l].astype(BF16), w_ffn2_out[l].astype(BF16)
        wi = _reorder_w_in(w_in[l].astype(BF16))
        wo_g, wo_n = w_out[l][:GLA_VW].astype(BF16), w_out[l][GLA_VW:].astype(BF16)
        lg, lb = ln_g[l][:, None, :], ln_b[l][:, None, :]
        w2h = jnp.zeros((LANES, GLA_KW), F32).at[:GLA_GATE_RANK].set(w_gla_gate2[l]).astype(BF16)
        b2h = b_gla_gate2[l][None, :]
        gn = gla_norm_g[l][None, :]
        pos2, w1bd = _compress_params(cmp_pos[l], cmp_w1[l])
        w2bd = _block_diag2(cmp_w2[l]).astype(BF16)

        h1, h1b = _ffn_ln(hp, wf1i, wf1o, lg[0], lb[0], alpha, _tile_size(B * T, FFN_TM), FFN_TF)
        proj, cmp_t, slc_t, win_t = _in_proj(h1b, wi, cos_p, sin_p, _tile_size(T, PROJ_TM), PROJ_TN, True)
        og, gla_state = _gla(proj, w2h, b2h, gn, jnp.zeros((B, GLA_HEADS, GLA_DK, GLA_DV), F32), B, T)
        n_grp = T // CMP_STRIDE
        n_cmp = (T - CMP_LEN) // CMP_STRIDE + 1
        kc, vc = _compress([proj], [pl.BlockSpec((T, ROWW), lambda b: (b, C_CMP // ROWW))],
                           pos2, w1bd, w2bd, B, n_grp, n_cmp)
        kc, vc = (jnp.pad(a, ((0, 0), (0, NBLK - n_grp), (0, 0))) for a in (kc, vc))
        on = _nsa_prompt(proj, kc, vc, ovl, bexp_p, gexp, B, T)
        h2, h2b = _out_ln(og, on, wo_g, wo_n, h1, lg[1], lb[1], alpha, _tile_size(B * T, OUT_TM))
        hp, _ = _ffn_ln(h2, wf2i, wf2o, lg[2], lb[2], alpha, _tile_size(B * T, FFN_TM), FFN_TF)
        row6 = lambda a, n, t: a.reshape(n, t, 2, NSA_KV, NSA_HD)
        row6_t = lambda a: a.reshape(a.shape[0], 2, NSA_KV, NSA_HD, a.shape[2]).transpose(0, 4, 1, 2, 3)
        outs["cmp_p"].append(row6_t(cmp_t))
        outs["slc_p"].append(row6_t(slc_t))
        outs["win_p"].append(row6_t(win_t[:, :, T - w_buf:]))
        outs["gla_p"].append(gla_state)

        s1, s1b = _ffn_ln(hs, wf1i, wf1o, lg[0], lb[0], alpha, _tile_size(DB * TS, FFN_TM), FFN_TF)
        (sproj,) = _in_proj(s1b, wi, cos_s, sin_s, tm_s, PROJ_TN, False)
        sog, sgla_state = _gla(sproj, w2h, b2h, gn, state_gla[l], DB, TS)
        feature_major = lambda c: c.transpose(0, 2, 3, 4, 1).reshape(c.shape[0] * ROWW, c.shape[1])
        cmp_pool_t = feature_major(cache_cmp_kv[l])
        slc_pool_t = feature_major(cache_slc_kv[l])
        win_cache_t = feature_major(cache_win_kv[l])
        n_grp_s = past_len // CMP_STRIDE
        n_cmp_s = (past_len + TS - CMP_LEN) // CMP_STRIDE + 1
        page_specs = [pl.BlockSpec((ROWW, PAGE_SIZE), lambda b, pt, p=p: (pt[b, p], 0)) for p in range(n_pages)]
        skc, svc = _compress([cmp_pool_t] * n_pages, page_specs, pos2, w1bd, w2bd, DB, n_grp_s, n_cmp_s,
                             scalar_args=(page_table,), feature_major=True)
        skc, svc = (jnp.pad(a, ((0, 0), (0, NBLK - n_grp_s), (0, 0))) for a in (skc, svc))
        son = _nsa_sample(sproj, skc, svc, slc_pool_t, win_cache_t, page_table, ovl, bexp_s, gexp, DB, TS, past_len)
        s2, s2b = _out_ln(sog, son, wo_g, wo_n, s1, lg[1], lb[1], alpha, _tile_size(DB * TS, OUT_TM))
        hs, _ = _ffn_ln(s2, wf2i, wf2o, lg[2], lb[2], alpha, _tile_size(DB * TS, FFN_TM), FFN_TF)
        new_win = row6(sproj[:, C_WIN:C_WIN + ROWW], DB, TS)
        outs["cmp_s"].append(row6(sproj[:, C_CMP:C_CMP + ROWW], DB, TS))
        outs["slc_s"].append(row6(sproj[:, C_SLC:C_SLC + ROWW], DB, TS))
        outs["win_s"].append(jnp.concatenate([cache_win_kv[l], new_win], axis=1)[:, -w_buf:])
        outs["gla_s"].append(sgla_state)

    st = lambda k: jnp.stack(outs[k])
    return (hp.reshape(B, T, D), hs.reshape(DB, TS, D),
            st("cmp_p"), st("slc_p"), st("win_p"), st("gla_p"),
            st("cmp_s"), st("slc_s"), st("win_s"), st("gla_s"))
```

```python
import functools
import math

import numpy as np
import jax
import jax.numpy as jnp
from jax import lax
from jax.experimental import pallas as pl
from jax.experimental.pallas import tpu as pltpu

F32 = jnp.float32
BF16 = jnp.bfloat16

GLA_HEADS = 4
GLA_DK = 128
GLA_DV = 256
GLA_GATE_RANK = 16
GLA_TAU = 16.0
NSA_HEADS = 16
NSA_KV = 4
NSA_GROUP = NSA_HEADS // NSA_KV
NSA_HD = 64
CMP_LEN = 32
CMP_STRIDE = 16
SEL_LEN = 64
SEL_TOPN = 16
WINDOW = 512
PAGE_SIZE = 128
ROPE_THETA = 10000.0
LN_EPS = 1e-5
NEG = -1e30

LANES = 128
SUBLANES = 8
VMEM_LIMIT_BYTES = 56 * 1024 * 1024

GLA_KW = GLA_HEADS * GLA_DK
GLA_VW = GLA_HEADS * GLA_DV
NSA_QW = NSA_HEADS * NSA_HD
NSA_KVW = NSA_KV * NSA_HD
ROWW = 2 * NSA_KVW
C_GQ = 0
C_GK = C_GQ + GLA_KW
C_GV = C_GK + GLA_KW
C_GR = C_GV + GLA_VW
C_NQ = C_GR + GLA_VW
C_CMP = C_NQ + NSA_QW
C_SLC = C_CMP + ROWW
C_WIN = C_SLC + ROWW
C_TAIL = C_WIN + ROWW
TAILW = ROWW
PROJ_W = C_TAIL + TAILW
GATE_LANE0 = GLA_GATE_RANK

FFN_TM = 512
FFN_TF = 512
PROJ_TM = 1024
PROJ_TN = 1024
OUT_TM = 512
GLA_TB = 1024
GLA_CHUNK = 128
GLA_SUB = 32
TQ = 256
SROWS = 128
KT = 128
FKT = 256
NBLK = 128


def _cparams(sem):
    return pltpu.CompilerParams(dimension_semantics=sem, vmem_limit_bytes=VMEM_LIMIT_BYTES)


def _dot(a, b):
    return jnp.dot(a, b, preferred_element_type=F32)


def _dot_nt(a, b):
    return lax.dot_general(a, b, (((1,), (1,)), ((), ())), preferred_element_type=F32)


def _layer_norm(y, g, b):
    mu = jnp.mean(y, -1, keepdims=True)
    d = y - mu
    var = jnp.mean(d * d, -1, keepdims=True)
    return d * lax.rsqrt(var + LN_EPS) * g + b


def _split_bf16(x, n):
    parts = []
    for _ in range(n - 1):
        hi = x.astype(BF16)
        parts.append(hi)
        x = x - hi.astype(F32)
    parts.append(x.astype(BF16))
    return parts


def _ffn_ln_kernel(x_ref, wg_ref, wu_ref, wo_ref, g_ref, b_ref, y_ref, yb_ref, xb_ref, acc_ref, *, alpha):
    j = pl.program_id(1)

    @pl.when(j == 0)
    def _():
        xb_ref[...] = x_ref[...].astype(BF16)
        acc_ref[...] = jnp.zeros_like(acc_ref)

    xb = xb_ref[...]
    h = _dot(xb, wg_ref[...])
    u = _dot(xb, wu_ref[...])
    a = (h * jax.nn.sigmoid(h) * u).astype(BF16)
    acc_ref[...] += _dot(a, wo_ref[...])

    @pl.when(j == pl.num_programs(1) - 1)
    def _():
        y = _layer_norm(alpha * x_ref[...] + 0.5 * acc_ref[...], g_ref[...], b_ref[...])
        y_ref[...] = y
        yb_ref[...] = y.astype(BF16)


def _ffn_ln(x, w_in, w_out, g, b, alpha, tm, tf):
    n, d = x.shape
    dff = w_out.shape[0]
    nf = dff // tf
    assert n % tm == 0 and dff % tf == 0
    return pl.pallas_call(
        functools.partial(_ffn_ln_kernel, alpha=alpha),
        grid=(n // tm, nf),
        in_specs=[
            pl.BlockSpec((tm, d), lambda i, j: (i, 0)),
            pl.BlockSpec((d, tf), lambda i, j: (0, j)),
            pl.BlockSpec((d, tf), lambda i, j: (0, j + nf)),
            pl.BlockSpec((tf, d), lambda i, j: (j, 0)),
            pl.BlockSpec((1, d), lambda i, j: (0, 0)),
            pl.BlockSpec((1, d), lambda i, j: (0, 0)),
        ],
        out_specs=[pl.BlockSpec((tm, d), lambda i, j: (i, 0)),
                   pl.BlockSpec((tm, d), lambda i, j: (i, 0))],
        out_shape=[jax.ShapeDtypeStruct((n, d), F32), jax.ShapeDtypeStruct((n, d), BF16)],
        scratch_shapes=[pltpu.VMEM((tm, d), BF16), pltpu.VMEM((tm, d), F32)],
        compiler_params=_cparams(("parallel", "arbitrary")),
        name="ffn_ln",
    )(x, w_in, w_in, w_out, g, b)


def _rope_lanes(x, cos, sin_signed):
    lane = lax.broadcasted_iota(jnp.int32, x.shape, 1)
    first_half = (lane & (NSA_HD - 1)) < (NSA_HD // 2)
    partner = jnp.where(first_half, pltpu.roll(x, LANES - NSA_HD // 2, 1), pltpu.roll(x, NSA_HD // 2, 1))
    return x * cos + partner * sin_signed


def _is_rotary_col(col):
    if C_NQ <= col < C_CMP:
        return True
    return C_CMP <= col < C_TAIL and (col - C_CMP) % ROWW < NSA_KVW


def _in_proj_kernel(x_ref, w_ref, cos_ref, sin_ref, o_ref, *t_refs, tn):
    j = pl.program_id(1)
    chunk = 2 * LANES

    for jj in range(PROJ_W // tn):
        @pl.when(j == jj)
        def _(jj=jj):
            for c0 in range(0, tn, chunk):
                acc = _dot(x_ref[...], w_ref[:, c0:c0 + chunk])
                for s0 in range(0, chunk, LANES):
                    col = jj * tn + c0 + s0
                    val = acc[:, s0:s0 + LANES]
                    if _is_rotary_col(col):
                        val = _rope_lanes(val, cos_ref[...], sin_ref[...])
                    o_ref[:, c0 + s0:c0 + s0 + LANES] = val
                    group, off = divmod(col - C_CMP, ROWW)
                    if t_refs and 0 <= group < len(t_refs):
                        t_refs[group][0, off:off + LANES, :] = val.T


def _in_proj(xb, w, cos, sin, tm, tn, emit_t):
    n, d = xb.shape
    period = cos.shape[0]
    assert n % tm == 0 and period % tm == 0 and PROJ_W % tn == 0 and tn % ROWW == 0
    npb = period // tm
    out_specs = [pl.BlockSpec((tm, tn), lambda i, j: (i, j))]
    out_shape = [jax.ShapeDtypeStruct((n, PROJ_W), F32)]
    if emit_t:
        out_specs += [pl.BlockSpec((1, ROWW, tm), lambda i, j: (i // npb, 0, i % npb))] * 3
        out_shape += [jax.ShapeDtypeStruct((n // period, ROWW, period), F32)] * 3
    return pl.pallas_call(
        functools.partial(_in_proj_kernel, tn=tn),
        grid=(n // tm, PROJ_W // tn),
        in_specs=[
            pl.BlockSpec((tm, d), lambda i, j: (i, 0)),
            pl.BlockSpec((d, tn), lambda i, j: (0, j)),
            pl.BlockSpec((tm, LANES), lambda i, j: (i % npb, 0)),
            pl.BlockSpec((tm, LANES), lambda i, j: (i % npb, 0)),
        ],
        out_specs=out_specs,
        out_shape=out_shape,
        compiler_params=_cparams(("parallel", "arbitrary")),
        name="in_proj",
    )(xb, w, cos, sin)


def _gla_kernel(q_ref, k_ref, v_ref, r_ref, ga_ref, w2_ref, b2_ref, gn_ref, s0_ref,
                o_ref, sout_ref, st_ref, *, n_valid, n_chunks, n_sb):
    t = pl.program_id(1)
    C, S, H = GLA_CHUNK, GLA_SUB, GLA_HEADS

    @pl.when(t == 0)
    def _():
        for sb in range(n_sb):
            for h in range(H):
                st_ref[sb * H + h] = s0_ref[sb, h].T

    row_k = lax.broadcasted_iota(jnp.int32, (C, GLA_DK), 0)
    row_a = lax.broadcasted_iota(jnp.int32, (C, GLA_KW), 0)
    row_c = lax.broadcasted_iota(jnp.int32, (C, C), 0)
    col_c = lax.broadcasted_iota(jnp.int32, (C, C), 1)
    causal = row_c >= col_c
    tri = causal.astype(BF16)

    def load(ref, r0, sb, c0, w):
        if n_valid >= C:
            return ref[pl.ds(r0, C), c0:c0 + w]
        x = ref[sb * n_valid:(sb + 1) * n_valid, c0:c0 + w]
        return jnp.concatenate([x, jnp.zeros((C - n_valid, w), x.dtype)], axis=0)

    def head_chain(r0, sb, h, b):
        q = load(q_ref, r0, sb, h * GLA_DK, GLA_DK) * (GLA_DK ** -0.5)
        k = load(k_ref, r0, sb, h * GLA_DK, GLA_DK)
        v = load(v_ref, r0, sb, h * GLA_DV, GLA_DV)
        b_last = b[C - 1:C, :]
        yield
        strips = []
        n_strips = min(C // S, -(-n_valid // S))
        for i in range(n_strips):
            rows = slice(i * S, (i + 1) * S)
            anchor = jnp.zeros((1, GLA_DK), F32) if i == 0 else b[i * S - 1:i * S, :]
            qi = q[rows] * jnp.exp(b[rows] - anchor)
            ki = k * jnp.exp(jnp.where(row_k < (i + 1) * S, anchor - b, 0.0))
            strips.append(_dot_nt(qi.astype(BF16), ki.astype(BF16)))
            yield
        if n_strips * S < C:
            strips.append(jnp.zeros((C - n_strips * S, C), F32))
        attn = jnp.where(causal, jnp.concatenate(strips, axis=0), 0.0)
        yield
        st = st_ref[sb * H + h]
        o = (_dot_nt((q * jnp.exp(b)).astype(BF16), st.astype(BF16))
             + _dot(attn.astype(BF16), v.astype(BF16)))
        yield
        ke = k * jnp.exp(b_last - b)
        st_ref[sb * H + h] = st * jnp.exp(b_last) + _dot(v.T.astype(BF16), ke.astype(BF16))
        yield
        rms = lax.rsqrt(jnp.mean(o * o, -1, keepdims=True) + LN_EPS)
        r = load(r_ref, r0, sb, h * GLA_DV, GLA_DV)
        out = (o * rms * gn_ref[...] * (r * jax.nn.sigmoid(r))).astype(o_ref.dtype)
        cols = slice(h * GLA_DV, (h + 1) * GLA_DV)
        if n_valid >= C:
            o_ref[pl.ds(r0, C), cols] = out
        else:
            o_ref[sb * n_valid:(sb + 1) * n_valid, cols] = out[:n_valid]

    def chunk(c, carry):
        r0 = pl.multiple_of(c * C, C)
        chains = []
        for sb in range(n_sb):
            pre = _dot(load(ga_ref, r0, sb, 0, LANES).astype(BF16), w2_ref[...]) + b2_ref[...]
            la = (jnp.minimum(pre, 0.0) - jnp.log(1.0 + jnp.exp(-jnp.abs(pre)))) * (1.0 / GLA_TAU)
            if n_valid < C:
                la = jnp.where(row_a < n_valid, la, 0.0)
            la_hi, la_lo = _split_bf16(la, 2)
            b_all = _dot(tri, la_hi) + _dot(tri, la_lo)
            chains += [head_chain(r0, sb, h, b_all[:, h * GLA_DK:(h + 1) * GLA_DK]) for h in range(H)]
        _lockstep(chains)
        return carry

    lax.fori_loop(0, n_chunks, chunk, 0)

    @pl.when(t == pl.num_programs(1) - 1)
    def _():
        for sb in range(n_sb):
            for h in range(H):
                sout_ref[sb, h] = st_ref[sb * H + h].T


def _gla(proj, w2h, b2h, gn, s0, n_seq, t_seq):
    C, H = GLA_CHUNK, GLA_HEADS
    if t_seq >= C:
        tb = math.gcd(t_seq, GLA_TB)
        assert tb % C == 0
        n_valid, n_chunks, n_sb = C, tb // C, 1
    else:
        tb, n_valid, n_chunks = t_seq, t_seq, 1
        n_sb = math.gcd(n_seq, 4)
        assert t_seq % SUBLANES == 0
    nt = t_seq // tb
    rows = n_sb * tb
    rowblk = lambda b, t: b * nt + t
    return pl.pallas_call(
        functools.partial(_gla_kernel, n_valid=n_valid, n_chunks=n_chunks, n_sb=n_sb),
        grid=(n_seq // n_sb, nt),
        in_specs=[
            pl.BlockSpec((rows, GLA_KW), lambda b, t: (rowblk(b, t), C_GQ // GLA_KW)),
            pl.BlockSpec((rows, GLA_KW), lambda b, t: (rowblk(b, t), C_GK // GLA_KW)),
            pl.BlockSpec((rows, GLA_VW), lambda b, t: (rowblk(b, t), C_GV // GLA_VW)),
            pl.BlockSpec((rows, GLA_VW), lambda b, t: (rowblk(b, t), C_GR // GLA_VW)),
            pl.BlockSpec((rows, LANES), lambda b, t: (rowblk(b, t), C_TAIL // LANES)),
            pl.BlockSpec((LANES, GLA_KW), lambda b, t: (0, 0)),
            pl.BlockSpec((1, GLA_KW), lambda b, t: (0, 0)),
            pl.BlockSpec((1, GLA_DV), lambda b, t: (0, 0)),
            pl.BlockSpec((n_sb, H, GLA_DK, GLA_DV), lambda b, t: (b, 0, 0, 0)),
        ],
        out_specs=[
            pl.BlockSpec((rows, GLA_VW), lambda b, t: (rowblk(b, t), 0)),
            pl.BlockSpec((n_sb, H, GLA_DK, GLA_DV), lambda b, t: (b, 0, 0, 0)),
        ],
        out_shape=[jax.ShapeDtypeStruct((n_seq * t_seq, GLA_VW), BF16 if t_seq >= C else F32),
                   jax.ShapeDtypeStruct((n_seq, H, GLA_DK, GLA_DV), F32)],
        scratch_shapes=[pltpu.VMEM((n_sb * H, GLA_DV, GLA_DK), F32)],
        compiler_params=_cparams(("parallel", "arbitrary")),
        name="gla",
    )(proj, proj, proj, proj, proj, w2h, b2h, gn, s0)


def _gelu_tanh(x):
    return 0.5 * x * (1.0 + jnp.tanh(math.sqrt(2.0 / math.pi) * (x + 0.044715 * (x * x * x))))


def _compress_kernel(*refs, n_scalar, n_src, n_cmp, feature_major):
    refs = refs[n_scalar:]
    src = refs[:n_src]
    pos_ref, w1_ref, w2_ref, kc_ref, vc_ref, slab_ref, u_ref = refs[n_src:]
    r_src = src[0].shape[1] if feature_major else src[0].shape[0]
    n_grp = n_src * r_src // CMP_STRIDE
    row = lax.broadcasted_iota(jnp.int32, (n_grp, LANES), 0)
    outs = (kc_ref, vc_ref)
    hw = 2 * LANES
    for part in range(2):
        pos_bias = _dot(pos_ref[part], w1_ref[part])
        for pair in range(NSA_KVW // LANES):
            c0 = part * NSA_KVW + pair * LANES
            for n, s in enumerate(src):
                slab_ref[n * r_src:(n + 1) * r_src, :] = s[c0:c0 + LANES, :].T if feature_major else s[:, c0:c0 + LANES]
            for l in range(CMP_STRIDE):
                u_ref[:, l * LANES:(l + 1) * LANES] = slab_ref[pl.ds(l, n_grp, stride=CMP_STRIDE), :].astype(BF16)
            ab = _dot(u_ref[...], w1_ref[part])
            hidden = []
            for hh in range(2):
                a = ab[:, hh * hw:hh * hw + LANES] + pos_bias[0:1, hh * hw:hh * hw + LANES]
                b = ab[:, hh * hw + LANES:(hh + 1) * hw] + pos_bias[SUBLANES:SUBLANES + 1, hh * hw + LANES:(hh + 1) * hw]
                hidden.append(a + pltpu.roll(b, n_grp - 1, 0))
            out = _dot(_gelu_tanh(jnp.concatenate(hidden, axis=1)).astype(BF16), w2_ref[part])
            outs[part][0, :, pair * LANES:(pair + 1) * LANES] = jnp.where(row < n_cmp, out, 0.0)


def _compress(srcs, src_specs, pos2, w1bd, w2bd, n_seq, n_grp, n_cmp, scalar_args=(), feature_major=False):
    n_src = len(srcs)
    nsp = len(scalar_args)
    in_specs = list(src_specs) + [
        pl.BlockSpec(pos2.shape, lambda b, *_: (0, 0, 0)),
        pl.BlockSpec(w1bd.shape, lambda b, *_: (0, 0, 0)),
        pl.BlockSpec(w2bd.shape, lambda b, *_: (0, 0, 0)),
    ]
    out_spec = pl.BlockSpec((1, n_grp, NSA_KVW), lambda b, *_: (b, 0, 0))
    return pl.pallas_call(
        functools.partial(_compress_kernel, n_scalar=nsp, n_src=n_src, n_cmp=n_cmp, feature_major=feature_major),
        grid_spec=pltpu.PrefetchScalarGridSpec(
            num_scalar_prefetch=nsp, grid=(n_seq,), in_specs=in_specs, out_specs=[out_spec, out_spec],
            scratch_shapes=[pltpu.VMEM((n_grp * CMP_STRIDE, LANES), F32),
                            pltpu.VMEM((n_grp, CMP_STRIDE * LANES), BF16)]),
        out_shape=[jax.ShapeDtypeStruct((n_seq, n_grp, NSA_KVW), F32)] * 2,
        compiler_params=_cparams(("parallel",)),
        name="compress",
    )(*scalar_args, *srcs, pos2, w1bd, w2bd)


def _build_q(q_ref, qs_ref, tq):
    lane = lax.broadcasted_iota(jnp.int32, (tq, LANES), 1)
    for j in range(NSA_HEADS):
        h = j // NSA_GROUP
        x = q_ref[:, (j // 2) * LANES:(j // 2 + 1) * LANES] * (NSA_HD ** -0.5)
        if (j % 2) != (h % 2):
            x = pltpu.roll(x, NSA_HD, 1)
        keep = (lane >= NSA_HD) if (h % 2) else (lane < NSA_HD)
        qs_ref[j * tq:(j + 1) * tq, :] = jnp.where(keep, x, 0.0).astype(qs_ref.dtype)


def _pair_slab(h):
    return slice((h // 2) * LANES, (h // 2 + 1) * LANES)


def _tile_rows(x, n):
    return jnp.concatenate([x] * n, axis=0)


def _lockstep(chains):
    chains = list(chains)
    out = [None] * len(chains)
    live = set(range(len(chains)))
    while live:
        for k in sorted(live):
            try:
                next(chains[k])
            except StopIteration as stop:
                out[k] = stop.value
                live.discard(k)
    return out


def _cmp_attention(qh, kc, vc, qpos, n_cmp):
    tq = qpos.shape[0]
    blk = lax.broadcasted_iota(jnp.int32, (tq, NBLK), 1)
    valid = ((blk * CMP_STRIDE + (CMP_LEN - 1) <= qpos) & (blk < n_cmp)).astype(F32)
    valid4 = _tile_rows(valid, NSA_GROUP)
    s = jnp.where(valid4 > 0.5, _dot_nt(qh, kc.astype(BF16)), NEG)
    yield
    e = jnp.exp(s - jnp.max(s, -1, keepdims=True))
    yield
    p = e * (valid4 * (1.0 / jnp.sum(e, -1, keepdims=True)))
    yield
    o = _dot(p.astype(BF16), vc.astype(BF16))
    psum = p[0:tq]
    for g in range(1, NSA_GROUP):
        psum = psum + p[g * tq:(g + 1) * tq]
    return o, psum


def _select_blocks(psum, ovl, qpos, n_sel):
    imp = sum(_dot(t, ovl) for t in _split_bf16(psum, 3))
    yield
    blk = lax.broadcasted_iota(jnp.int32, psum.shape, 1)
    ss = blk * SEL_LEN
    cur = qpos - (qpos & (SEL_LEN - 1))
    forced = (ss == 0) | (ss == cur) | (ss == cur - SEL_LEN)
    score = jnp.where(ss <= qpos, jnp.where(forced, 1e9, imp), -1e9)
    nb = -(-n_sel // SUBLANES) * SUBLANES
    st = score.T[0:nb]
    yield
    sub = lax.broadcasted_iota(jnp.int32, st.shape, 0)
    rank = jnp.zeros(st.shape, F32)
    for s in range(n_sel):
        other = st[s:s + 1, :]
        beats = (other > st) | ((other == st) & (sub > s))
        rank = rank + beats.astype(F32)
        if s % SUBLANES == SUBLANES - 1:
            yield
    sel_t = ((rank < SEL_TOPN) & (st > -1e8)).astype(F32)
    sel_t = jnp.concatenate([sel_t, jnp.zeros((NBLK - nb, st.shape[1]), F32)], axis=0)
    yield
    return sel_t.T


def _emit_branch(out_ref, o, gate, h, tq, first, denom_in_other_half=False):
    lane = lax.broadcasted_iota(jnp.int32, (tq, LANES), 1)
    for gp in range(NSA_GROUP // 2):
        y0 = o[(2 * gp) * tq:(2 * gp + 1) * tq]
        y1 = o[(2 * gp + 1) * tq:(2 * gp + 2) * tq]
        if h % 2:
            y0 = pltpu.roll(y0, NSA_HD, 1)
        else:
            y1 = pltpu.roll(y1, NSA_HD, 1)
        m = 2 * h + gp
        sl = slice(m * LANES, (m + 1) * LANES)
        val = jnp.where(lane < NSA_HD, y0, y1)
        if denom_in_other_half:
            val = val / pltpu.roll(jnp.where(lane < NSA_HD, y1, y0), NSA_HD, 1)
        val = val * gate[:, sl]
        if first:
            out_ref[:, sl] = val
        else:
            out_ref[:, sl] += val


def _expand_gates(tail, gexp_ref):
    g_hi, g_lo = _split_bf16(jax.nn.sigmoid(tail), 2)
    return [_dot(g_hi, gexp_ref[c]) + _dot(g_lo, gexp_ref[c]) for c in range(3)]


def _own_half(h, shape):
    lane = lax.broadcasted_iota(jnp.int32, shape, len(shape) - 1)
    return (lane >= NSA_HD) if (h % 2) else (lane < NSA_HD)


def _nsa_prompt_kernel(q_ref, slc_ref, win_ref, tail_ref, kc_ref, vc_ref, ovl_ref, bexp_ref, gexp_ref,
                       o_ref, qs_ref, mexp_ref, m_ref, acc_ref, out_ref, *, n_cmp, n_sel):
    i = pl.program_id(1)
    q0 = i * TQ
    G = NSA_GROUP
    R = G * TQ
    _build_q(q_ref, qs_ref, TQ)
    gates = _expand_gates(tail_ref[:, 0:LANES], gexp_ref)
    qpos = q0 + lax.broadcasted_iota(jnp.int32, (TQ, NBLK), 0)
    qpos_k = q0 + lax.broadcasted_iota(jnp.int32, (TQ, FKT), 0)
    lane_k = lax.broadcasted_iota(jnp.int32, (TQ, FKT), 1)
    n_kt = mexp_ref.shape[1]

    def head_chain(h):
        o_c, psum = yield from _cmp_attention(qs_ref[h * R:(h + 1) * R, :], kc_ref[0, :, _pair_slab(h)],
                                              vc_ref[0, :, _pair_slab(h)], qpos, n_cmp)
        yield
        sel = yield from _select_blocks(psum, ovl_ref[...], qpos, n_sel)
        _emit_branch(out_ref, o_c, gates[0], h, TQ, True)
        yield
        mexp = _dot(sel.astype(BF16), bexp_ref[...])
        for kt in range(n_kt):
            mexp_ref[h, kt] = mexp[:, kt * FKT:(kt + 1) * FKT].astype(BF16)

    _lockstep([head_chain(h) for h in range(NSA_KV)])

    def flash(src_ref, lo, hi, bias_fn):
        m_ref[...] = jnp.full(m_ref.shape, NEG, F32)
        acc_ref[...] = jnp.zeros(acc_ref.shape, F32)

        def body(kt, carry):
            k0 = pl.multiple_of(kt * FKT, FKT)
            kpos = k0 + lane_k
            kslab = [src_ref[pl.ds(k0, FKT), m * LANES:(m + 1) * LANES].astype(BF16) for m in range(2)]
            vslab = [src_ref[pl.ds(k0, FKT), NSA_KVW + m * LANES:NSA_KVW + (m + 1) * LANES] for m in range(2)]
            for h in range(NSA_KV):
                vv = jnp.where(_own_half(h, (FKT, LANES)), vslab[h // 2], 1.0).astype(BF16)
                s = _dot_nt(qs_ref[h * R:(h + 1) * R, :], kslab[h // 2]) + _tile_rows(bias_fn(h, kt, kpos), G)
                m_prev = m_ref[h]
                smax = s[:, 0:LANES]
                for c in range(1, FKT // LANES):
                    smax = jnp.maximum(smax, s[:, c * LANES:(c + 1) * LANES])
                m_new = jnp.maximum(m_prev, jnp.max(smax, -1, keepdims=True))
                alpha = jnp.exp(m_prev - m_new)
                p = jnp.exp(s - jnp.concatenate([m_new] * (FKT // LANES), axis=1))
                acc_ref[h] = alpha * acc_ref[h] + _dot(p.astype(BF16), vv)
                m_ref[h] = m_new
            return carry

        lax.fori_loop(lo, hi, body, 0)

    def sel_bias(h, kt, kpos):
        return jnp.where((mexp_ref[h, kt].astype(F32) > 0.5) & (kpos <= qpos_k), 0.0, NEG)

    def win_bias(h, kt, kpos):
        dist = qpos_k - kpos
        return jnp.where((dist >= 0) & (dist < WINDOW), 0.0, NEG)

    hi = (q0 + TQ + FKT - 1) // FKT
    flash(slc_ref, 0, hi, sel_bias)
    for h in range(NSA_KV):
        _emit_branch(out_ref, acc_ref[h], gates[1], h, TQ, False, denom_in_other_half=True)
    flash(win_ref, jnp.maximum(q0 - WINDOW, 0) // FKT, hi, win_bias)
    for h in range(NSA_KV):
        _emit_branch(out_ref, acc_ref[h], gates[2], h, TQ, False, denom_in_other_half=True)
    o_ref[...] = out_ref[...].astype(o_ref.dtype)


def _nsa_prompt(proj, kc, vc, ovl, bexp, gexp, n_seq, t_seq):
    assert t_seq % FKT == 0 and t_seq % TQ == 0 and TQ % LANES == 0 and NBLK == LANES
    nq = t_seq // TQ
    n_cmp = (t_seq - CMP_LEN) // CMP_STRIDE + 1
    n_sel = -(-t_seq // SEL_LEN)
    G = NSA_GROUP
    return pl.pallas_call(
        functools.partial(_nsa_prompt_kernel, n_cmp=n_cmp, n_sel=n_sel),
        grid=(n_seq, nq),
        in_specs=[
            pl.BlockSpec((TQ, NSA_QW), lambda b, i: (b * nq + i, C_NQ // NSA_QW)),
            pl.BlockSpec((t_seq, ROWW), lambda b, i: (b, C_SLC // ROWW)),
            pl.BlockSpec((t_seq, ROWW), lambda b, i: (b, C_WIN // ROWW)),
            pl.BlockSpec((TQ, TAILW), lambda b, i: (b * nq + i, C_TAIL // TAILW)),
            pl.BlockSpec((1, NBLK, NSA_KVW), lambda b, i: (b, 0, 0)),
            pl.BlockSpec((1, NBLK, NSA_KVW), lambda b, i: (b, 0, 0)),
            pl.BlockSpec(ovl.shape, lambda b, i: (0, 0)),
            pl.BlockSpec(bexp.shape, lambda b, i: (0, 0)),
            pl.BlockSpec(gexp.shape, lambda b, i: (0, 0, 0)),
        ],
        out_specs=pl.BlockSpec((TQ, NSA_QW), lambda b, i: (b * nq + i, 0)),
        out_shape=jax.ShapeDtypeStruct((n_seq * t_seq, NSA_QW), BF16),
        scratch_shapes=[
            pltpu.VMEM((NSA_HEADS * TQ, LANES), BF16),
            pltpu.VMEM((NSA_KV, t_seq // FKT, TQ, FKT), BF16),
            pltpu.VMEM((NSA_KV, G * TQ, LANES), F32),
            pltpu.VMEM((NSA_KV, G * TQ, LANES), F32),
            pltpu.VMEM((TQ, NSA_QW), F32),
        ],
        compiler_params=_cparams(("parallel", "arbitrary")),
        name="nsa_prompt",
    )(proj, proj, proj, proj, kc, vc, ovl, bexp, gexp)


def _nsa_sample_kernel(*refs, n_sb, n_pages, t_new, **kw):
    q_ref, snew_ref, wnew_ref, tail_ref, kc_ref, vc_ref, wcache_ref = refs[1:8]
    pages = refs[8:8 + n_sb * n_pages]
    ovl_ref, bexp_ref, gexp_ref, o_ref, qs_ref, out_ref = refs[8 + n_sb * n_pages:]
    chains = []
    for sb in range(n_sb):
        rows = pl.ds(sb * t_new, t_new)
        chains += _nsa_sample_chains(
            q_ref.at[rows], snew_ref.at[rows], wnew_ref.at[rows], tail_ref.at[rows], kc_ref.at[pl.ds(sb, 1)],
            vc_ref.at[pl.ds(sb, 1)], wcache_ref.at[pl.ds(sb * ROWW, ROWW)], pages[sb * n_pages:(sb + 1) * n_pages],
            ovl_ref, bexp_ref, gexp_ref, o_ref.at[rows],
            qs_ref.at[pl.ds(sb * NSA_HEADS * t_new, NSA_HEADS * t_new)], out_ref.at[rows], t_new=t_new, **kw)
    _lockstep(chains)


def _nsa_sample_chains(q_ref, snew_ref, wnew_ref, tail_ref, kc_ref, vc_ref, wcache_ref, pages, ovl_ref, bexp_ref,
                       gexp_ref, o_ref, qs_ref, out_ref, *, n_wt, t_new, past_len, n_cmp, n_sel):
    n_pages = len(pages)
    G = NSA_GROUP
    R = NSA_HEADS * t_new
    half = R // 2
    _build_q(q_ref, qs_ref, t_new)
    gates = _expand_gates(tail_ref[:, 0:LANES], gexp_ref)
    zeros = jnp.zeros((half, LANES), F32)
    q_all = jnp.concatenate([jnp.concatenate([qs_ref[0:half, :], zeros], axis=1),
                             jnp.concatenate([zeros, qs_ref[half:R, :]], axis=1)], axis=0).astype(BF16)
    row_t = lax.broadcasted_iota(jnp.int32, (SROWS, NBLK), 0) & (t_new - 1)
    qpos_blk = past_len + row_t
    pad_new = lambda x: jnp.concatenate([x, jnp.zeros((KT - t_new, x.shape[1]), x.dtype)], axis=0)
    window_out = []

    def attend(k_t_tiles, v_t_tiles, k_new, v_new, bias):
        s = [_dot(q_all, kt.astype(BF16)) for kt in k_t_tiles] + [_dot_nt(q_all, pad_new(k_new).astype(BF16))]
        yield
        s = jnp.concatenate(s, axis=1) + bias
        e = jnp.exp(s - jnp.max(s, -1, keepdims=True))
        yield
        p = (e / jnp.sum(e, -1, keepdims=True)).astype(BF16)
        yield
        n = len(k_t_tiles)
        o = _dot(p[:, n * KT:(n + 1) * KT], pad_new(v_new).astype(BF16))
        for c, vt in enumerate(v_t_tiles):
            o = o + _dot_nt(p[:, c * KT:(c + 1) * KT], vt.astype(BF16))
        return o

    def emit(o_all, gate, first):
        for h in range(NSA_KV):
            _emit_branch(out_ref, o_all[h * G * t_new:(h + 1) * G * t_new, _pair_slab(h)], gate, h, t_new, first)

    def window_chain():
        n_keys = (n_wt + 1) * KT
        kpos = (past_len - n_wt * KT) + lax.broadcasted_iota(jnp.int32, (R, n_keys), 1)
        dist = past_len + (lax.broadcasted_iota(jnp.int32, (R, n_keys), 0) & (t_new - 1)) - kpos
        bias = jnp.where((dist >= 0) & (dist < WINDOW), 0.0, NEG)
        o_w = yield from attend([wcache_ref[0:NSA_KVW, c * KT:(c + 1) * KT] for c in range(n_wt)],
                                [wcache_ref[NSA_KVW:ROWW, c * KT:(c + 1) * KT] for c in range(n_wt)],
                                wnew_ref[:, 0:NSA_KVW], wnew_ref[:, NSA_KVW:ROWW], bias)
        window_out.append(o_w)

    def main_chain():
        qpos_r = qpos_blk[0:R]
        blk = lax.broadcasted_iota(jnp.int32, (R, NBLK), 1)
        valid = ((blk * CMP_STRIDE + (CMP_LEN - 1) <= qpos_r) & (blk < n_cmp)).astype(F32)
        s = jnp.where(valid > 0.5, _dot_nt(q_all, kc_ref[0].astype(BF16)), NEG)
        yield
        e = jnp.exp(s - jnp.max(s, -1, keepdims=True))
        p = e / jnp.sum(e, -1, keepdims=True) * valid
        yield
        emit(_dot(p.astype(BF16), vc_ref[0].astype(BF16)), gates[0], True)

        psum = []
        for h in range(NSA_KV):
            acc = p[(h * G) * t_new:(h * G + 1) * t_new]
            for g in range(1, G):
                acc = acc + p[(h * G + g) * t_new:(h * G + g + 1) * t_new]
            psum.append(acc)
        psum = jnp.concatenate(psum + [jnp.zeros((SROWS - NSA_KV * t_new, NBLK), F32)], axis=0)
        sel = yield from _select_blocks(psum, ovl_ref[...], qpos_blk, n_sel)
        mexp = _dot(sel.astype(BF16), bexp_ref[...])
        mexp = jnp.concatenate([_tile_rows(mexp[h * t_new:(h + 1) * t_new], G) for h in range(NSA_KV)], axis=0)
        n_keys = (n_pages + 1) * KT
        kpos = lax.broadcasted_iota(jnp.int32, (R, n_keys), 1)
        qpos_k = past_len + (lax.broadcasted_iota(jnp.int32, (R, n_keys), 0) & (t_new - 1))
        bias = jnp.where((mexp > 0.5) & (kpos <= qpos_k), 0.0, NEG)
        yield
        o_s = yield from attend([pg[0:NSA_KVW, :] for pg in pages], [pg[NSA_KVW:ROWW, :] for pg in pages],
                                snew_ref[:, 0:NSA_KVW], snew_ref[:, NSA_KVW:ROWW], bias)
        emit(o_s, gates[1], False)
        while not window_out:
            yield
        emit(window_out[0], gates[2], False)
        o_ref[...] = out_ref[...].astype(o_ref.dtype)

    return [window_chain(), main_chain()]


def _nsa_sample(proj, kc, vc, slc_pool_t, win_cache_t, page_table, ovl, bexp, gexp, n_seq, t_new, past_len):
    n_pages = page_table.shape[1]
    w_buf = win_cache_t.shape[1]
    assert w_buf % KT == 0 and t_new == SUBLANES and NSA_HEADS * t_new <= SROWS and PAGE_SIZE == KT
    n_wt = w_buf // KT
    n_cmp = (past_len + t_new - CMP_LEN) // CMP_STRIDE + 1
    n_sel = -(-(past_len + t_new) // SEL_LEN)
    n_sb = math.gcd(n_seq, 2)
    rows = n_sb * t_new
    page_spec = lambda sb, p: pl.BlockSpec((ROWW, PAGE_SIZE), lambda b, pt: (pt[n_sb * b + sb, p], 0))
    in_specs = [
        pl.BlockSpec((rows, NSA_QW), lambda b, pt: (b, C_NQ // NSA_QW)),
        pl.BlockSpec((rows, ROWW), lambda b, pt: (b, C_SLC // ROWW)),
        pl.BlockSpec((rows, ROWW), lambda b, pt: (b, C_WIN // ROWW)),
        pl.BlockSpec((rows, TAILW), lambda b, pt: (b, C_TAIL // TAILW)),
        pl.BlockSpec((n_sb, NBLK, NSA_KVW), lambda b, pt: (b, 0, 0)),
        pl.BlockSpec((n_sb, NBLK, NSA_KVW), lambda b, pt: (b, 0, 0)),
        pl.BlockSpec((n_sb * ROWW, w_buf), lambda b, pt: (b, 0)),
    ] + [page_spec(sb, p) for sb in range(n_sb) for p in range(n_pages)] + [
        pl.BlockSpec(ovl.shape, lambda b, pt: (0, 0)),
        pl.BlockSpec(bexp.shape, lambda b, pt: (0, 0)),
        pl.BlockSpec(gexp.shape, lambda b, pt: (0, 0, 0)),
    ]
    return pl.pallas_call(
        functools.partial(_nsa_sample_kernel, n_sb=n_sb, n_pages=n_pages, n_wt=n_wt, t_new=t_new,
                          past_len=past_len, n_cmp=n_cmp, n_sel=n_sel),
        grid_spec=pltpu.PrefetchScalarGridSpec(
            num_scalar_prefetch=1, grid=(n_seq // n_sb,), in_specs=in_specs,
            out_specs=pl.BlockSpec((rows, NSA_QW), lambda b, pt: (b, 0)),
            scratch_shapes=[pltpu.VMEM((n_sb * NSA_HEADS * t_new, LANES), F32),
                            pltpu.VMEM((rows, NSA_QW), F32)]),
        out_shape=jax.ShapeDtypeStruct((n_seq * t_new, NSA_QW), F32),
        compiler_params=_cparams(("parallel",)),
        name="nsa_sample",
    )(page_table, proj, proj, proj, proj, kc, vc, win_cache_t, *([slc_pool_t] * (n_sb * n_pages)), ovl, bexp, gexp)


def _out_ln_kernel(og_ref, on_ref, wg_ref, wn_ref, x_ref, g_ref, b_ref, y_ref, yb_ref, *, alpha):
    tm = x_ref.shape[0]
    rows = math.gcd(tm, 256)
    for r0 in range(0, tm, rows):
        sl = slice(r0, r0 + rows)
        m = _dot(og_ref[sl, :].astype(BF16), wg_ref[...]) + _dot(on_ref[sl, :].astype(BF16), wn_ref[...])
        y = _layer_norm(alpha * x_ref[sl, :] + m, g_ref[...], b_ref[...])
        y_ref[sl, :] = y
        yb_ref[sl, :] = y.astype(BF16)


def _out_ln(og, on, wg, wn, x, g, b, alpha, tm):
    n, d = x.shape
    assert n % tm == 0
    return pl.pallas_call(
        functools.partial(_out_ln_kernel, alpha=alpha),
        grid=(n // tm,),
        in_specs=[
            pl.BlockSpec((tm, og.shape[1]), lambda i: (i, 0)),
            pl.BlockSpec((tm, on.shape[1]), lambda i: (i, 0)),
            pl.BlockSpec(wg.shape, lambda i: (0, 0)),
            pl.BlockSpec(wn.shape, lambda i: (0, 0)),
            pl.BlockSpec((tm, d), lambda i: (i, 0)),
            pl.BlockSpec((1, d), lambda i: (0, 0)),
            pl.BlockSpec((1, d), lambda i: (0, 0)),
        ],
        out_specs=[pl.BlockSpec((tm, d), lambda i: (i, 0)), pl.BlockSpec((tm, d), lambda i: (i, 0))],
        out_shape=[jax.ShapeDtypeStruct((n, d), F32), jax.ShapeDtypeStruct((n, d), BF16)],
        compiler_params=_cparams(("parallel",)),
        name="out_ln",
    )(og, on, wg, wn, x, g, b)


def _rope_tables(pos):
    half = NSA_HD // 2
    freq = ROPE_THETA ** (-jnp.arange(half, dtype=F32) / half)
    ang = pos.astype(F32)[:, None] * freq[None, :]
    cos, sin = jnp.cos(ang), jnp.sin(ang)
    cos_h = jnp.concatenate([cos, cos], -1)
    sin_h = jnp.concatenate([-sin, sin], -1)
    reps = LANES // NSA_HD
    return jnp.tile(cos_h, (1, reps)), jnp.tile(sin_h, (1, reps))


def _reorder_w_in(w):
    d = w.shape[0]
    sizes = (GLA_KW, GLA_KW, GLA_VW, GLA_GATE_RANK, GLA_VW, NSA_QW) + (NSA_KVW,) * 6 + (NSA_HEADS * 3,)
    offs = np.concatenate([[0], np.cumsum(sizes)])
    gq, gk, gv, ga, gr, nq, kc, vc, ks, vs, kw, vw, ng = [w[:, offs[i]:offs[i + 1]] for i in range(len(sizes))]
    pad = jnp.zeros((d, TAILW - GLA_GATE_RANK - NSA_HEADS * 3), w.dtype)
    return jnp.concatenate([gq, gk, gv, gr, nq, kc, vc, ks, vs, kw, vw, ga, ng, pad], axis=1)


def _block_diag2(w):
    z = jnp.zeros_like(w)
    return jnp.concatenate([jnp.concatenate([w, z], -1), jnp.concatenate([z, w], -1)], -2)


def _compress_params(pos, w1):
    half = CMP_LEN // 2
    hidden = w1.shape[-1]
    w1r = w1.reshape(2, 2, half, NSA_HD, hidden)
    eye = jnp.eye(2, dtype=w1.dtype)
    w1p = jnp.einsum("hg,palDn->plhDgan", eye, w1r).reshape(2, half * 2 * NSA_HD, 2 * 2 * hidden)
    posr = jnp.tile(pos.reshape(2, 2, half, 1, NSA_HD), (1, 1, 1, 2, 1)).reshape(2, 2, 1, half * 2 * NSA_HD)
    pos16 = jnp.tile(posr, (1, 1, SUBLANES, 1)).reshape(2, 2 * SUBLANES, half * 2 * NSA_HD)
    return pos16.astype(BF16), w1p.astype(BF16)


def _tile_size(n, pref):
    t = math.gcd(n, pref)
    assert t % SUBLANES == 0
    return t


def _const_tables(n_keys):
    cs = np.arange(NBLK)[:, None] * CMP_STRIDE
    ss = np.arange(NBLK)[None, :] * SEL_LEN
    ovl = ((cs < ss + SEL_LEN) & (cs + CMP_LEN > ss)).astype(np.float32)
    bexp = (np.arange(NBLK)[:, None] == (np.arange(n_keys)[None, :] // SEL_LEN)).astype(np.float32)
    gexp = np.zeros((3, LANES, NSA_QW), np.float32)
    for c in range(3):
        for j in range(NSA_HEADS):
            gexp[c, GATE_LANE0 + 3 * j + c, j * NSA_HD:(j + 1) * NSA_HD] = 1.0
    return jnp.asarray(ovl, BF16), jnp.asarray(bexp, BF16), jnp.asarray(gexp, BF16)


def kernel(x_prompt, x_sample, cache_cmp_kv, cache_slc_kv, cache_win_kv, state_gla, page_table,
           w_ffn1_in, w_ffn1_out, w_in, w_gla_gate2, b_gla_gate2, gla_norm_g,
           cmp_pos, cmp_w1, cmp_w2, w_out, w_ffn2_in, w_ffn2_out, ln_g, ln_b):
    B, T, D = x_prompt.shape
    DB, TS, _ = x_sample.shape
    depth = w_in.shape[0]
    n_pages = page_table.shape[1]
    past_len = n_pages * PAGE_SIZE
    w_buf = cache_win_kv.shape[2]
    alpha = (2.0 * depth) ** 0.25
    assert w_buf == WINDOW and T % TQ == 0 and past_len % KT == 0

    cos_p, sin_p = _rope_tables(jnp.arange(T, dtype=jnp.int32))
    cos_s, sin_s = _rope_tables(past_len + jnp.arange(TS, dtype=jnp.int32))
    ovl, bexp_p, gexp = _const_tables(T)
    _, bexp_s, _ = _const_tables(past_len + KT)

    hp = x_prompt.reshape(B * T, D)
    hs = x_sample.reshape(DB * TS, D)
    tm_s = _tile_size(DB * TS, PROJ_TM)
    cos_s, sin_s = (jnp.tile(t, (tm_s // TS, 1)) for t in (cos_s, sin_s))
    outs = {k: [] for k in ("cmp_p", "slc_p", "win_p", "gla_p", "cmp_s", "slc_s", "win_s", "gla_s")}

    for l in range(depth):
        wf1i, wf1o = w_ffn1_in[l].astype(BF16), w_ffn1_out[l].astype(BF16)
        wf2i, wf2o = w_ffn2_in[l].astype(BF16), w_ffn2_out[l].astype(BF16)
        wi = _reorder_w_in(w_in[l].astype(BF16))
        wo_g, wo_n = w_out[l][:GLA_VW].astype(BF16), w_out[l][GLA_VW:].astype(BF16)
        lg, lb = ln_g[l][:, None, :], ln_b[l][:, None, :]
        w2h = jnp.zeros((LANES, GLA_KW), F32).at[:GLA_GATE_RANK].set(w_gla_gate2[l]).astype(BF16)
        b2h = b_gla_gate2[l][None, :]
        gn = gla_norm_g[l][None, :]
        pos2, w1bd = _compress_params(cmp_pos[l], cmp_w1[l])
        w2bd = _block_diag2(cmp_w2[l]).astype(BF16)

        h1, h1b = _ffn_ln(hp, wf1i, wf1o, lg[0], lb[0], alpha, _tile_size(B * T, FFN_TM), FFN_TF)
        proj, cmp_t, slc_t, win_t = _in_proj(h1b, wi, cos_p, sin_p, _tile_size(T, PROJ_TM), PROJ_TN, True)
        og, gla_state = _gla(proj, w2h, b2h, gn, jnp.zeros((B, GLA_HEADS, GLA_DK, GLA_DV), F32), B, T)
        n_grp = T // CMP_STRIDE
        n_cmp = (T - CMP_LEN) // CMP_STRIDE + 1
        kc, vc = _compress([proj], [pl.BlockSpec((T, ROWW), lambda b: (b, C_CMP // ROWW))],
                           pos2, w1bd, w2bd, B, n_grp, n_cmp)
        kc, vc = (jnp.pad(a, ((0, 0), (0, NBLK - n_grp), (0, 0))) for a in (kc, vc))
        on = _nsa_prompt(proj, kc, vc, ovl, bexp_p, gexp, B, T)
        h2, h2b = _out_ln(og, on, wo_g, wo_n, h1, lg[1], lb[1], alpha, _tile_size(B * T, OUT_TM))
        hp, _ = _ffn_ln(h2, wf2i, wf2o, lg[2], lb[2], alpha, _tile_size(B * T, FFN_TM), FFN_TF)
        row6 = lambda a, n, t: a.reshape(n, t, 2, NSA_KV, NSA_HD)
        row6_t = lambda a: a.reshape(a.shape[0], 2, NSA_KV, NSA_HD, a.shape[2]).transpose(0, 4, 1, 2, 3)
        outs["cmp_p"].append(row6_t(cmp_t))
        outs["slc_p"].append(row6_t(slc_t))
        outs["win_p"].append(row6_t(win_t[:, :, T - w_buf:]))
        outs["gla_p"].append(gla_state)

        s1, s1b = _ffn_ln(hs, wf1i, wf1o, lg[0], lb[0], alpha, _tile_size(DB * TS, FFN_TM), FFN_TF)
        (sproj,) = _in_proj(s1b, wi, cos_s, sin_s, tm_s, PROJ_TN, False)
        sog, sgla_state = _gla(sproj, w2h, b2h, gn, state_gla[l], DB, TS)
        feature_major = lambda c: c.transpose(0, 2, 3, 4, 1).reshape(c.shape[0] * ROWW, c.shape[1])
        cmp_pool_t = feature_major(cache_cmp_kv[l])
        slc_pool_t = feature_major(cache_slc_kv[l])
        win_cache_t = feature_major(cache_win_kv[l])
        n_grp_s = past_len // CMP_STRIDE
        n_cmp_s = (past_len + TS - CMP_LEN) // CMP_STRIDE + 1
        page_specs = [pl.BlockSpec((ROWW, PAGE_SIZE), lambda b, pt, p=p: (pt[b, p], 0)) for p in range(n_pages)]
        skc, svc = _compress([cmp_pool_t] * n_pages, page_specs, pos2, w1bd, w2bd, DB, n_grp_s, n_cmp_s,
                             scalar_args=(page_table,), feature_major=True)
        skc, svc = (jnp.pad(a, ((0, 0), (0, NBLK - n_grp_s), (0, 0))) for a in (skc, svc))
        son = _nsa_sample(sproj, skc, svc, slc_pool_t, win_cache_t, page_table, ovl, bexp_s, gexp, DB, TS, past_len)
        s2, s2b = _out_ln(sog, son, wo_g, wo_n, s1, lg[1], lb[1], alpha, _tile_size(DB * TS, OUT_TM))
        hs, _ = _ffn_ln(s2, wf2i, wf2o, lg[2], lb[2], alpha, _tile_size(DB * TS, FFN_TM), FFN_TF)
        new_win = row6(sproj[:, C_WIN:C_WIN + ROWW], DB, TS)
        outs["cmp_s"].append(row6(sproj[:, C_CMP:C_CMP + ROWW], DB, TS))
        outs["slc_s"].append(row6(sproj[:, C_SLC:C_SLC + ROWW], DB, TS))
        outs["win_s"].append(jnp.concatenate([cache_win_kv[l], new_win], axis=1)[:, -w_buf:])
        outs["gla_s"].append(sgla_state)

    st = lambda k: jnp.stack(outs[k])
    return (hp.reshape(B, T, D), hs.reshape(DB, TS, D),
            st("cmp_p"), st("slc_p"), st("win_p"), st("gla_p"),
            st("cmp_s"), st("slc_s"), st("win_s"), st("gla_s"))
```

```python
import functools
import math

import numpy as np
import jax
import jax.numpy as jnp
from jax import lax
from jax.experimental import pallas as pl
from jax.experimental.pallas import tpu as pltpu

F32 = jnp.float32
BF16 = jnp.bfloat16

GLA_HEADS = 4
GLA_DK = 128
GLA_DV = 256
GLA_GATE_RANK = 16
GLA_TAU = 16.0
NSA_HEADS = 16
NSA_KV = 4
NSA_GROUP = NSA_HEADS // NSA_KV
NSA_HD = 64
CMP_LEN = 32
CMP_STRIDE = 16
SEL_LEN = 64
SEL_TOPN = 16
WINDOW = 512
PAGE_SIZE = 128
ROPE_THETA = 10000.0
LN_EPS = 1e-5
NEG = -1e30

LANES = 128
SUBLANES = 8
VMEM_LIMIT_BYTES = 56 * 1024 * 1024

GLA_KW = GLA_HEADS * GLA_DK
GLA_VW = GLA_HEADS * GLA_DV
NSA_QW = NSA_HEADS * NSA_HD
NSA_KVW = NSA_KV * NSA_HD
ROWW = 2 * NSA_KVW
C_GQ = 0
C_GK = C_GQ + GLA_KW
C_GV = C_GK + GLA_KW
C_GR = C_GV + GLA_VW
C_NQ = C_GR + GLA_VW
C_CMP = C_NQ + NSA_QW
C_SLC = C_CMP + ROWW
C_WIN = C_SLC + ROWW
C_TAIL = C_WIN + ROWW
TAILW = ROWW
PROJ_W = C_TAIL + TAILW
GATE_LANE0 = GLA_GATE_RANK

FFN_TM = 512
FFN_TF = 512
PROJ_TM = 1024
PROJ_TN = 1024
OUT_TM = 512
GLA_TB = 1024
GLA_CHUNK = 128
GLA_SUB = 32
TQ = 256
SROWS = 128
KT = 128
FKT = 256
NBLK = 128


def _cparams(sem):
    return pltpu.CompilerParams(dimension_semantics=sem, vmem_limit_bytes=VMEM_LIMIT_BYTES)


def _dot(a, b):
    return jnp.dot(a, b, preferred_element_type=F32)


def _dot_nt(a, b):
    return lax.dot_general(a, b, (((1,), (1,)), ((), ())), preferred_element_type=F32)


def _layer_norm(y, g, b):
    mu = jnp.mean(y, -1, keepdims=True)
    d = y - mu
    var = jnp.mean(d * d, -1, keepdims=True)
    return d * lax.rsqrt(var + LN_EPS) * g + b


def _split_bf16(x, n):
    parts = []
    for _ in range(n - 1):
        hi = x.astype(BF16)
        parts.append(hi)
        x = x - hi.astype(F32)
    parts.append(x.astype(BF16))
    return parts


def _ffn_ln_kernel(x_ref, wg_ref, wu_ref, wo_ref, g_ref, b_ref, y_ref, yb_ref, xb_ref, acc_ref, *, alpha):
    j = pl.program_id(1)

    @pl.when(j == 0)
    def _():
        xb_ref[...] = x_ref[...].astype(BF16)
        acc_ref[...] = jnp.zeros_like(acc_ref)

    xb = xb_ref[...]
    h = _dot(xb, wg_ref[...])
    u = _dot(xb, wu_ref[...])
    a = (h * jax.nn.sigmoid(h) * u).astype(BF16)
    acc_ref[...] += _dot(a, wo_ref[...])

    @pl.when(j == pl.num_programs(1) - 1)
    def _():
        y = _layer_norm(alpha * x_ref[...] + 0.5 * acc_ref[...], g_ref[...], b_ref[...])
        y_ref[...] = y
        yb_ref[...] = y.astype(BF16)


def _ffn_ln(x, w_in, w_out, g, b, alpha, tm, tf):
    n, d = x.shape
    dff = w_out.shape[0]
    nf = dff // tf
    assert n % tm == 0 and dff % tf == 0
    return pl.pallas_call(
        functools.partial(_ffn_ln_kernel, alpha=alpha),
        grid=(n // tm, nf),
        in_specs=[
            pl.BlockSpec((tm, d), lambda i, j: (i, 0)),
            pl.BlockSpec((d, tf), lambda i, j: (0, j)),
            pl.BlockSpec((d, tf), lambda i, j: (0, j + nf)),
            pl.BlockSpec((tf, d), lambda i, j: (j, 0)),
            pl.BlockSpec((1, d), lambda i, j: (0, 0)),
            pl.BlockSpec((1, d), lambda i, j: (0, 0)),
        ],
        out_specs=[pl.BlockSpec((tm, d), lambda i, j: (i, 0)),
                   pl.BlockSpec((tm, d), lambda i, j: (i, 0))],
        out_shape=[jax.ShapeDtypeStruct((n, d), F32), jax.ShapeDtypeStruct((n, d), BF16)],
        scratch_shapes=[pltpu.VMEM((tm, d), BF16), pltpu.VMEM((tm, d), F32)],
        compiler_params=_cparams(("parallel", "arbitrary")),
        name="ffn_ln",
    )(x, w_in, w_in, w_out, g, b)


def _rope_lanes(x, cos, sin_signed):
    lane = lax.broadcasted_iota(jnp.int32, x.shape, 1)
    first_half = (lane & (NSA_HD - 1)) < (NSA_HD // 2)
    partner = jnp.where(first_half, pltpu.roll(x, LANES - NSA_HD // 2, 1), pltpu.roll(x, NSA_HD // 2, 1))
    return x * cos + partner * sin_signed


def _is_rotary_col(col):
    if C_NQ <= col < C_CMP:
        return True
    return C_CMP <= col < C_TAIL and (col - C_CMP) % ROWW < NSA_KVW


def _in_proj_kernel(x_ref, w_ref, cos_ref, sin_ref, o_ref, *t_refs, tn):
    j = pl.program_id(1)
    chunk = 2 * LANES

    for jj in range(PROJ_W // tn):
        @pl.when(j == jj)
        def _(jj=jj):
            for c0 in range(0, tn, chunk):
                acc = _dot(x_ref[...], w_ref[:, c0:c0 + chunk])
                for s0 in range(0, chunk, LANES):
                    col = jj * tn + c0 + s0
                    val = acc[:, s0:s0 + LANES]
                    if _is_rotary_col(col):
                        val = _rope_lanes(val, cos_ref[...], sin_ref[...])
                    o_ref[:, c0 + s0:c0 + s0 + LANES] = val
                    group, off = divmod(col - C_CMP, ROWW)
                    if t_refs and 0 <= group < len(t_refs):
                        t_refs[group][0, off:off + LANES, :] = val.T


def _in_proj(xb, w, cos, sin, tm, tn, emit_t):
    n, d = xb.shape
    period = cos.shape[0]
    assert n % tm == 0 and period % tm == 0 and PROJ_W % tn == 0 and tn % ROWW == 0
    npb = period // tm
    out_specs = [pl.BlockSpec((tm, tn), lambda i, j: (i, j))]
    out_shape = [jax.ShapeDtypeStruct((n, PROJ_W), F32)]
    if emit_t:
        out_specs += [pl.BlockSpec((1, ROWW, tm), lambda i, j: (i // npb, 0, i % npb))] * 3
        out_shape += [jax.ShapeDtypeStruct((n // period, ROWW, period), F32)] * 3
    return pl.pallas_call(
        functools.partial(_in_proj_kernel, tn=tn),
        grid=(n // tm, PROJ_W // tn),
        in_specs=[
            pl.BlockSpec((tm, d), lambda i, j: (i, 0)),
            pl.BlockSpec((d, tn), lambda i, j: (0, j)),
            pl.BlockSpec((tm, LANES), lambda i, j: (i % npb, 0)),
            pl.BlockSpec((tm, LANES), lambda i, j: (i % npb, 0)),
        ],
        out_specs=out_specs,
        out_shape=out_shape,
        compiler_params=_cparams(("parallel", "arbitrary")),
        name="in_proj",
    )(xb, w, cos, sin)


def _gla_kernel(q_ref, k_ref, v_ref, r_ref, ga_ref, w2_ref, b2_ref, gn_ref, s0_ref,
                o_ref, sout_ref, st_ref, *, n_valid, n_chunks, n_sb):
    t = pl.program_id(1)
    C, S, H = GLA_CHUNK, GLA_SUB, GLA_HEADS

    @pl.when(t == 0)
    def _():
        for sb in range(n_sb):
            for h in range(H):
                st_ref[sb * H + h] = s0_ref[sb, h].T

    row_k = lax.broadcasted_iota(jnp.int32, (C, GLA_DK), 0)
    row_a = lax.broadcasted_iota(jnp.int32, (C, GLA_KW), 0)
    row_c = lax.broadcasted_iota(jnp.int32, (C, C), 0)
    col_c = lax.broadcasted_iota(jnp.int32, (C, C), 1)
    causal = row_c >= col_c
    tri = causal.astype(BF16)

    def load(ref, r0, sb, c0, w):
        if n_valid >= C:
            return ref[pl.ds(r0, C), c0:c0 + w]
        x = ref[sb * n_valid:(sb + 1) * n_valid, c0:c0 + w]
        return jnp.concatenate([x, jnp.zeros((C - n_valid, w), x.dtype)], axis=0)

    def head_chain(r0, sb, h, b):
        q = load(q_ref, r0, sb, h * GLA_DK, GLA_DK) * (GLA_DK ** -0.5)
        k = load(k_ref, r0, sb, h * GLA_DK, GLA_DK)
        v = load(v_ref, r0, sb, h * GLA_DV, GLA_DV)
        b_last = b[C - 1:C, :]
        yield
        strips = []
        n_strips = min(C // S, -(-n_valid // S))
        for i in range(n_strips):
            rows = slice(i * S, (i + 1) * S)
            anchor = jnp.zeros((1, GLA_DK), F32) if i == 0 else b[i * S - 1:i * S, :]
            qi = q[rows] * jnp.exp(b[rows] - anchor)
            ki = k * jnp.exp(jnp.where(row_k < (i + 1) * S, anchor - b, 0.0))
            strips.append(_dot_nt(qi.astype(BF16), ki.astype(BF16)))
            yield
        if n_strips * S < C:
            strips.append(jnp.zeros((C - n_strips * S, C), F32))
        attn = jnp.where(causal, jnp.concatenate(strips, axis=0), 0.0)
        yield
        st = st_ref[sb * H + h]
        o = (_dot_nt((q * jnp.exp(b)).astype(BF16), st.astype(BF16))
             + _dot(attn.astype(BF16), v.astype(BF16)))
        yield
        ke = k * jnp.exp(b_last - b)
        st_ref[sb * H + h] = st * jnp.exp(b_last) + _dot(v.T.astype(BF16), ke.astype(BF16))
        yield
        rms = lax.rsqrt(jnp.mean(o * o, -1, keepdims=True) + LN_EPS)
        r = load(r_ref, r0, sb, h * GLA_DV, GLA_DV)
        out = (o * rms * gn_ref[...] * (r * jax.nn.sigmoid(r))).astype(o_ref.dtype)
        cols = slice(h * GLA_DV, (h + 1) * GLA_DV)
        if n_valid >= C:
            o_ref[pl.ds(r0, C), cols] = out
        else:
            o_ref[sb * n_valid:(sb + 1) * n_valid, cols] = out[:n_valid]

    def chunk(c, carry):
        r0 = pl.multiple_of(c * C, C)
        chains = []
        for sb in range(n_sb):
            pre = _dot(load(ga_ref, r0, sb, 0, LANES).astype(BF16), w2_ref[...]) + b2_ref[...]
            la = (jnp.minimum(pre, 0.0) - jnp.log(1.0 + jnp.exp(-jnp.abs(pre)))) * (1.0 / GLA_TAU)
            if n_valid < C:
                la = jnp.where(row_a < n_valid, la, 0.0)
            la_hi, la_lo = _split_bf16(la, 2)
            b_all = _dot(tri, la_hi) + _dot(tri, la_lo)
            chains += [head_chain(r0, sb, h, b_all[:, h * GLA_DK:(h + 1) * GLA_DK]) for h in range(H)]
        _lockstep(chains)
        return carry

    lax.fori_loop(0, n_chunks, chunk, 0)

    @pl.when(t == pl.num_programs(1) - 1)
    def _():
        for sb in range(n_sb):
            for h in range(H):
                sout_ref[sb, h] = st_ref[sb * H + h].T


def _gla(proj, w2h, b2h, gn, s0, n_seq, t_seq):
    C, H = GLA_CHUNK, GLA_HEADS
    if t_seq >= C:
        tb = math.gcd(t_seq, GLA_TB)
        assert tb % C == 0
        n_valid, n_chunks, n_sb = C, tb // C, 1
    else:
        tb, n_valid, n_chunks = t_seq, t_seq, 1
        n_sb = math.gcd(n_seq, 4)
        assert t_seq % SUBLANES == 0
    nt = t_seq // tb
    rows = n_sb * tb
    rowblk = lambda b, t: b * nt + t
    return pl.pallas_call(
        functools.partial(_gla_kernel, n_valid=n_valid, n_chunks=n_chunks, n_sb=n_sb),
        grid=(n_seq // n_sb, nt),
        in_specs=[
            pl.BlockSpec((rows, GLA_KW), lambda b, t: (rowblk(b, t), C_GQ // GLA_KW)),
            pl.BlockSpec((rows, GLA_KW), lambda b, t: (rowblk(b, t), C_GK // GLA_KW)),
            pl.BlockSpec((rows, GLA_VW), lambda b, t: (rowblk(b, t), C_GV // GLA_VW)),
            pl.BlockSpec((rows, GLA_VW), lambda b, t: (rowblk(b, t), C_GR // GLA_VW)),
            pl.BlockSpec((rows, LANES), lambda b, t: (rowblk(b, t), C_TAIL // LANES)),
            pl.BlockSpec((LANES, GLA_KW), lambda b, t: (0, 0)),
            pl.BlockSpec((1, GLA_KW), lambda b, t: (0, 0)),
            pl.BlockSpec((1, GLA_DV), lambda b, t: (0, 0)),
            pl.BlockSpec((n_sb, H, GLA_DK, GLA_DV), lambda b, t: (b, 0, 0, 0)),
        ],
        out_specs=[
            pl.BlockSpec((rows, GLA_VW), lambda b, t: (rowblk(b, t), 0)),
            pl.BlockSpec((n_sb, H, GLA_DK, GLA_DV), lambda b, t: (b, 0, 0, 0)),
        ],
        out_shape=[jax.ShapeDtypeStruct((n_seq * t_seq, GLA_VW), BF16 if t_seq >= C else F32),
                   jax.ShapeDtypeStruct((n_seq, H, GLA_DK, GLA_DV), F32)],
        scratch_shapes=[pltpu.VMEM((n_sb * H, GLA_DV, GLA_DK), F32)],
        compiler_params=_cparams(("parallel", "arbitrary")),
        name="gla",
    )(proj, proj, proj, proj, proj, w2h, b2h, gn, s0)


def _gelu_tanh(x):
    return 0.5 * x * (1.0 + jnp.tanh(math.sqrt(2.0 / math.pi) * (x + 0.044715 * (x * x * x))))


def _compress_kernel(*refs, n_scalar, n_src, n_cmp, feature_major):
    refs = refs[n_scalar:]
    src = refs[:n_src]
    pos_ref, w1_ref, w2_ref, kc_ref, vc_ref, slab_ref, u_ref = refs[n_src:]
    r_src = src[0].shape[1] if feature_major else src[0].shape[0]
    n_grp = n_src * r_src // CMP_STRIDE
    row = lax.broadcasted_iota(jnp.int32, (n_grp, LANES), 0)
    outs = (kc_ref, vc_ref)
    hw = 2 * LANES
    for part in range(2):
        pos_bias = _dot(pos_ref[part], w1_ref[part])
        for pair in range(NSA_KVW // LANES):
            c0 = part * NSA_KVW + pair * LANES
            for n, s in enumerate(src):
                slab_ref[n * r_src:(n + 1) * r_src, :] = s[c0:c0 + LANES, :].T if feature_major else s[:, c0:c0 + LANES]
            for l in range(CMP_STRIDE):
                u_ref[:, l * LANES:(l + 1) * LANES] = slab_ref[pl.ds(l, n_grp, stride=CMP_STRIDE), :].astype(BF16)
            ab = _dot(u_ref[...], w1_ref[part])
            hidden = []
            for hh in range(2):
                a = ab[:, hh * hw:hh * hw + LANES] + pos_bias[0:1, hh * hw:hh * hw + LANES]
                b = ab[:, hh * hw + LANES:(hh + 1) * hw] + pos_bias[SUBLANES:SUBLANES + 1, hh * hw + LANES:(hh + 1) * hw]
                hidden.append(a + pltpu.roll(b, n_grp - 1, 0))
            out = _dot(_gelu_tanh(jnp.concatenate(hidden, axis=1)).astype(BF16), w2_ref[part])
            outs[part][0, :, pair * LANES:(pair + 1) * LANES] = jnp.where(row < n_cmp, out, 0.0)


def _compress(srcs, src_specs, pos2, w1bd, w2bd, n_seq, n_grp, n_cmp, scalar_args=(), feature_major=False):
    n_src = len(srcs)
    nsp = len(scalar_args)
    in_specs = list(src_specs) + [
        pl.BlockSpec(pos2.shape, lambda b, *_: (0, 0, 0)),
        pl.BlockSpec(w1bd.shape, lambda b, *_: (0, 0, 0)),
        pl.BlockSpec(w2bd.shape, lambda b, *_: (0, 0, 0)),
    ]
    out_spec = pl.BlockSpec((1, n_grp, NSA_KVW), lambda b, *_: (b, 0, 0))
    return pl.pallas_call(
        functools.partial(_compress_kernel, n_scalar=nsp, n_src=n_src, n_cmp=n_cmp, feature_major=feature_major),
        grid_spec=pltpu.PrefetchScalarGridSpec(
            num_scalar_prefetch=nsp, grid=(n_seq,), in_specs=in_specs, out_specs=[out_spec, out_spec],
            scratch_shapes=[pltpu.VMEM((n_grp * CMP_STRIDE, LANES), F32),
                            pltpu.VMEM((n_grp, CMP_STRIDE * LANES), BF16)]),
        out_shape=[jax.ShapeDtypeStruct((n_seq, n_grp, NSA_KVW), F32)] * 2,
        compiler_params=_cparams(("parallel",)),
        name="compress",
    )(*scalar_args, *srcs, pos2, w1bd, w2bd)


def _build_q(q_ref, qs_ref, tq):
    lane = lax.broadcasted_iota(jnp.int32, (tq, LANES), 1)
    for j in range(NSA_HEADS):
        h = j // NSA_GROUP
        x = q_ref[:, (j // 2) * LANES:(j // 2 + 1) * LANES] * (NSA_HD ** -0.5)
        if (j % 2) != (h % 2):
            x = pltpu.roll(x, NSA_HD, 1)
        keep = (lane >= NSA_HD) if (h % 2) else (lane < NSA_HD)
        qs_ref[j * tq:(j + 1) * tq, :] = jnp.where(keep, x, 0.0).astype(qs_ref.dtype)


def _pair_slab(h):
    return slice((h // 2) * LANES, (h // 2 + 1) * LANES)


def _tile_rows(x, n):
    return jnp.concatenate([x] * n, axis=0)


def _lockstep(chains):
    chains = list(chains)
    out = [None] * len(chains)
    live = set(range(len(chains)))
    while live:
        for k in sorted(live):
            try:
                next(chains[k])
            except StopIteration as stop:
                out[k] = stop.value
                live.discard(k)
    return out


def _cmp_attention(qh, kc, vc, qpos, n_cmp):
    tq = qpos.shape[0]
    blk = lax.broadcasted_iota(jnp.int32, (tq, NBLK), 1)
    valid = ((blk * CMP_STRIDE + (CMP_LEN - 1) <= qpos) & (blk < n_cmp)).astype(F32)
    valid4 = _tile_rows(valid, NSA_GROUP)
    s = jnp.where(valid4 > 0.5, _dot_nt(qh, kc.astype(BF16)), NEG)
    yield
    e = jnp.exp(s - jnp.max(s, -1, keepdims=True))
    yield
    p = e * (valid4 * (1.0 / jnp.sum(e, -1, keepdims=True)))
    yield
    o = _dot(p.astype(BF16), vc.astype(BF16))
    psum = p[0:tq]
    for g in range(1, NSA_GROUP):
        psum = psum + p[g * tq:(g + 1) * tq]
    return o, psum


def _select_blocks(psum, ovl, qpos, n_sel):
    imp = sum(_dot(t, ovl) for t in _split_bf16(psum, 3))
    yield
    blk = lax.broadcasted_iota(jnp.int32, psum.shape, 1)
    ss = blk * SEL_LEN
    cur = qpos - (qpos & (SEL_LEN - 1))
    forced = (ss == 0) | (ss == cur) | (ss == cur - SEL_LEN)
    score = jnp.where(ss <= qpos, jnp.where(forced, 1e9, imp), -1e9)
    nb = -(-n_sel // SUBLANES) * SUBLANES
    st = score.T[0:nb]
    yield
    sub = lax.broadcasted_iota(jnp.int32, st.shape, 0)
    rank = jnp.zeros(st.shape, F32)
    for s in range(n_sel):
        other = st[s:s + 1, :]
        beats = (other > st) | ((other == st) & (sub > s))
        rank = rank + beats.astype(F32)
        if s % SUBLANES == SUBLANES - 1:
            yield
    sel_t = ((rank < SEL_TOPN) & (st > -1e8)).astype(F32)
    sel_t = jnp.concatenate([sel_t, jnp.zeros((NBLK - nb, st.shape[1]), F32)], axis=0)
    yield
    return sel_t.T


def _emit_branch(out_ref, o, gate, h, tq, first, denom_in_other_half=False):
    lane = lax.broadcasted_iota(jnp.int32, (tq, LANES), 1)
    for gp in range(NSA_GROUP // 2):
        y0 = o[(2 * gp) * tq:(2 * gp + 1) * tq]
        y1 = o[(2 * gp + 1) * tq:(2 * gp + 2) * tq]
        if h % 2:
            y0 = pltpu.roll(y0, NSA_HD, 1)
        else:
            y1 = pltpu.roll(y1, NSA_HD, 1)
        m = 2 * h + gp
        sl = slice(m * LANES, (m + 1) * LANES)
        val = jnp.where(lane < NSA_HD, y0, y1)
        if denom_in_other_half:
            val = val / pltpu.roll(jnp.where(lane < NSA_HD, y1, y0), NSA_HD, 1)
        val = val * gate[:, sl]
        if first:
            out_ref[:, sl] = val
        else:
            out_ref[:, sl] += val


def _expand_gates(tail, gexp_ref):
    g_hi, g_lo = _split_bf16(jax.nn.sigmoid(tail), 2)
    return [_dot(g_hi, gexp_ref[c]) + _dot(g_lo, gexp_ref[c]) for c in range(3)]


def _own_half(h, shape):
    lane = lax.broadcasted_iota(jnp.int32, shape, len(shape) - 1)
    return (lane >= NSA_HD) if (h % 2) else (lane < NSA_HD)


def _nsa_prompt_kernel(q_ref, slc_ref, win_ref, tail_ref, kc_ref, vc_ref, ovl_ref, bexp_ref, gexp_ref,
                       o_ref, qs_ref, mexp_ref, m_ref, acc_ref, out_ref, *, n_cmp, n_sel):
    i = pl.program_id(1)
    q0 = i * TQ
    G = NSA_GROUP
    R = G * TQ
    _build_q(q_ref, qs_ref, TQ)
    gates = _expand_gates(tail_ref[:, 0:LANES], gexp_ref)
    qpos = q0 + lax.broadcasted_iota(jnp.int32, (TQ, NBLK), 0)
    qpos_k = q0 + lax.broadcasted_iota(jnp.int32, (TQ, FKT), 0)
    lane_k = lax.broadcasted_iota(jnp.int32, (TQ, FKT), 1)
    n_kt = mexp_ref.shape[1]

    def head_chain(h):
        o_c, psum = yield from _cmp_attention(qs_ref[h * R:(h + 1) * R, :], kc_ref[0, :, _pair_slab(h)],
                                              vc_ref[0, :, _pair_slab(h)], qpos, n_cmp)
        yield
        sel = yield from _select_blocks(psum, ovl_ref[...], qpos, n_sel)
        _emit_branch(out_ref, o_c, gates[0], h, TQ, True)
        yield
        mexp = _dot(sel.astype(BF16), bexp_ref[...])
        for kt in range(n_kt):
            mexp_ref[h, kt] = mexp[:, kt * FKT:(kt + 1) * FKT].astype(BF16)

    _lockstep([head_chain(h) for h in range(NSA_KV)])

    def flash(src_ref, lo, hi, bias_fn):
        m_ref[...] = jnp.full(m_ref.shape, NEG, F32)
        acc_ref[...] = jnp.zeros(acc_ref.shape, F32)

        def body(kt, carry):
            k0 = pl.multiple_of(kt * FKT, FKT)
            kpos = k0 + lane_k
            kslab = [src_ref[pl.ds(k0, FKT), m * LANES:(m + 1) * LANES].astype(BF16) for m in range(2)]
            vslab = [src_ref[pl.ds(k0, FKT), NSA_KVW + m * LANES:NSA_KVW + (m + 1) * LANES] for m in range(2)]
            for h in range(NSA_KV):
                vv = jnp.where(_own_half(h, (FKT, LANES)), vslab[h // 2], 1.0).astype(BF16)
                s = _dot_nt(qs_ref[h * R:(h + 1) * R, :], kslab[h // 2]) + _tile_rows(bias_fn(h, kt, kpos), G)
                m_prev = m_ref[h]
                smax = s[:, 0:LANES]
                for c in range(1, FKT // LANES):
                    smax = jnp.maximum(smax, s[:, c * LANES:(c + 1) * LANES])
                m_new = jnp.maximum(m_prev, jnp.max(smax, -1, keepdims=True))
                alpha = jnp.exp(m_prev - m_new)
                p = jnp.exp(s - jnp.concatenate([m_new] * (FKT // LANES), axis=1))
                acc_ref[h] = alpha * acc_ref[h] + _dot(p.astype(BF16), vv)
                m_ref[h] = m_new
            return carry

        lax.fori_loop(lo, hi, body, 0)

    def sel_bias(h, kt, kpos):
        return jnp.where((mexp_ref[h, kt].astype(F32) > 0.5) & (kpos <= qpos_k), 0.0, NEG)

    def win_bias(h, kt, kpos):
        dist = qpos_k - kpos
        return jnp.where((dist >= 0) & (dist < WINDOW), 0.0, NEG)

    hi = (q0 + TQ + FKT - 1) // FKT
    flash(slc_ref, 0, hi, sel_bias)
    for h in range(NSA_KV):
        _emit_branch(out_ref, acc_ref[h], gates[1], h, TQ, False, denom_in_other_half=True)
    flash(win_ref, jnp.maximum(q0 - WINDOW, 0) // FKT, hi, win_bias)
    for h in range(NSA_KV):
        _emit_branch(out_ref, acc_ref[h], gates[2], h, TQ, False, denom_in_other_half=True)
    o_ref[...] = out_ref[...].astype(o_ref.dtype)


def _nsa_prompt(proj, kc, vc, ovl, bexp, gexp, n_seq, t_seq):
    assert t_seq % FKT == 0 and t_seq % TQ == 0 and TQ % LANES == 0 and NBLK == LANES
    nq = t_seq // TQ
    n_cmp = (t_seq - CMP_LEN) // CMP_STRIDE + 1
    n_sel = -(-t_seq // SEL_LEN)
    G = NSA_GROUP
    return pl.pallas_call(
        functools.partial(_nsa_prompt_kernel, n_cmp=n_cmp, n_sel=n_sel),
        grid=(n_seq, nq),
        in_specs=[
            pl.BlockSpec((TQ, NSA_QW), lambda b, i: (b * nq + i, C_NQ // NSA_QW)),
            pl.BlockSpec((t_seq, ROWW), lambda b, i: (b, C_SLC // ROWW)),
            pl.BlockSpec((t_seq, ROWW), lambda b, i: (b, C_WIN // ROWW)),
            pl.BlockSpec((TQ, TAILW), lambda b, i: (b * nq + i, C_TAIL // TAILW)),
            pl.BlockSpec((1, NBLK, NSA_KVW), lambda b, i: (b, 0, 0)),
            pl.BlockSpec((1, NBLK, NSA_KVW), lambda b, i: (b, 0, 0)),
            pl.BlockSpec(ovl.shape, lambda b, i: (0, 0)),
            pl.BlockSpec(bexp.shape, lambda b, i: (0, 0)),
            pl.BlockSpec(gexp.shape, lambda b, i: (0, 0, 0)),
        ],
        out_specs=pl.BlockSpec((TQ, NSA_QW), lambda b, i: (b * nq + i, 0)),
        out_shape=jax.ShapeDtypeStruct((n_seq * t_seq, NSA_QW), BF16),
        scratch_shapes=[
            pltpu.VMEM((NSA_HEADS * TQ, LANES), BF16),
            pltpu.VMEM((NSA_KV, t_seq // FKT, TQ, FKT), BF16),
            pltpu.VMEM((NSA_KV, G * TQ, LANES), F32),
            pltpu.VMEM((NSA_KV, G * TQ, LANES), F32),
            pltpu.VMEM((TQ, NSA_QW), F32),
        ],
        compiler_params=_cparams(("parallel", "arbitrary")),
        name="nsa_prompt",
    )(proj, proj, proj, proj, kc, vc, ovl, bexp, gexp)


def _nsa_sample_kernel(*refs, n_sb, n_pages, t_new, **kw):
    q_ref, snew_ref, wnew_ref, tail_ref, kc_ref, vc_ref, wcache_ref = refs[1:8]
    pages = refs[8:8 + n_sb * n_pages]
    ovl_ref, bexp_ref, gexp_ref, o_ref, qs_ref, out_ref = refs[8 + n_sb * n_pages:]
    chains = []
    for sb in range(n_sb):
        rows = pl.ds(sb * t_new, t_new)
        chains += _nsa_sample_chains(
            q_ref.at[rows], snew_ref.at[rows], wnew_ref.at[rows], tail_ref.at[rows], kc_ref.at[pl.ds(sb, 1)],
            vc_ref.at[pl.ds(sb, 1)], wcache_ref.at[pl.ds(sb * ROWW, ROWW)], pages[sb * n_pages:(sb + 1) * n_pages],
            ovl_ref, bexp_ref, gexp_ref, o_ref.at[rows],
            qs_ref.at[pl.ds(sb * NSA_HEADS * t_new, NSA_HEADS * t_new)], out_ref.at[rows], t_new=t_new, **kw)
    _lockstep(chains)


def _nsa_sample_chains(q_ref, snew_ref, wnew_ref, tail_ref, kc_ref, vc_ref, wcache_ref, pages, ovl_ref, bexp_ref,
                       gexp_ref, o_ref, qs_ref, out_ref, *, n_wt, t_new, past_len, n_cmp, n_sel):
    n_pages = len(pages)
    G = NSA_GROUP
    R = NSA_HEADS * t_new
    half = R // 2
    _build_q(q_ref, qs_ref, t_new)
    gates = _expand_gates(tail_ref[:, 0:LANES], gexp_ref)
    zeros = jnp.zeros((half, LANES), F32)
    q_all = jnp.concatenate([jnp.concatenate([qs_ref[0:half, :], zeros], axis=1),
                             jnp.concatenate([zeros, qs_ref[half:R, :]], axis=1)], axis=0).astype(BF16)
    row_t = lax.broadcasted_iota(jnp.int32, (SROWS, NBLK), 0) & (t_new - 1)
    qpos_blk = past_len + row_t
    pad_new = lambda x: jnp.concatenate([x, jnp.zeros((KT - t_new, x.shape[1]), x.dtype)], axis=0)
    window_out = []

    def attend(k_t_tiles, v_t_tiles, k_new, v_new, bias):
        s = [_dot(q_all, kt.astype(BF16)) for kt in k_t_tiles] + [_dot_nt(q_all, pad_new(k_new).astype(BF16))]
        yield
        s = jnp.concatenate(s, axis=1) + bias
        e = jnp.exp(s - jnp.max(s, -1, keepdims=True))
        yield
        p = (e / jnp.sum(e, -1, keepdims=True)).astype(BF16)
        yield
        n = len(k_t_tiles)
        o = _dot(p[:, n * KT:(n + 1) * KT], pad_new(v_new).astype(BF16))
        for c, vt in enumerate(v_t_tiles):
            o = o + _dot_nt(p[:, c * KT:(c + 1) * KT], vt.astype(BF16))
        return o

    def emit(o_all, gate, first):
        for h in range(NSA_KV):
            _emit_branch(out_ref, o_all[h * G * t_new:(h + 1) * G * t_new, _pair_slab(h)], gate, h, t_new, first)

    def window_chain():
        n_keys = (n_wt + 1) * KT
        kpos = (past_len - n_wt * KT) + lax.broadcasted_iota(jnp.int32, (R, n_keys), 1)
        dist = past_len + (lax.broadcasted_iota(jnp.int32, (R, n_keys), 0) & (t_new - 1)) - kpos
        bias = jnp.where((dist >= 0) & (dist < WINDOW), 0.0, NEG)
        o_w = yield from attend([wcache_ref[0:NSA_KVW, c * KT:(c + 1) * KT] for c in range(n_wt)],
                                [wcache_ref[NSA_KVW:ROWW, c * KT:(c + 1) * KT] for c in range(n_wt)],
                                wnew_ref[:, 0:NSA_KVW], wnew_ref[:, NSA_KVW:ROWW], bias)
        window_out.append(o_w)

    def main_chain():
        qpos_r = qpos_blk[0:R]
        blk = lax.broadcasted_iota(jnp.int32, (R, NBLK), 1)
        valid = ((blk * CMP_STRIDE + (CMP_LEN - 1) <= qpos_r) & (blk < n_cmp)).astype(F32)
        s = jnp.where(valid > 0.5, _dot_nt(q_all, kc_ref[0].astype(BF16)), NEG)
        yield
        e = jnp.exp(s - jnp.max(s, -1, keepdims=True))
        p = e / jnp.sum(e, -1, keepdims=True) * valid
        yield
        emit(_dot(p.astype(BF16), vc_ref[0].astype(BF16)), gates[0], True)

        psum = []
        for h in range(NSA_KV):
            acc = p[(h * G) * t_new:(h * G + 1) * t_new]
            for g in range(1, G):
                acc = acc + p[(h * G + g) * t_new:(h * G + g + 1) * t_new]
            psum.append(acc)
        psum = jnp.concatenate(psum + [jnp.zeros((SROWS - NSA_KV * t_new, NBLK), F32)], axis=0)
        sel = yield from _select_blocks(psum, ovl_ref[...], qpos_blk, n_sel)
        mexp = _dot(sel.astype(BF16), bexp_ref[...])
        mexp = jnp.concatenate([_tile_rows(mexp[h * t_new:(h + 1) * t_new], G) for h in range(NSA_KV)], axis=0)
        n_keys = (n_pages + 1) * KT
        kpos = lax.broadcasted_iota(jnp.int32, (R, n_keys), 1)
        qpos_k = past_len + (lax.broadcasted_iota(jnp.int32, (R, n_keys), 0) & (t_new - 1))
        bias = jnp.where((mexp > 0.5) & (kpos <= qpos_k), 0.0, NEG)
        yield
        o_s = yield from attend([pg[0:NSA_KVW, :] for pg in pages], [pg[NSA_KVW:ROWW, :] for pg in pages],
                                snew_ref[:, 0:NSA_KVW], snew_ref[:, NSA_KVW:ROWW], bias)
        emit(o_s, gates[1], False)
        while not window_out:
            yield
        emit(window_out[0], gates[2], False)
        o_ref[...] = out_ref[...].astype(o_ref.dtype)

    return [window_chain(), main_chain()]


def _nsa_sample(proj, kc, vc, slc_pool_t, win_cache_t, page_table, ovl, bexp, gexp, n_seq, t_new, past_len):
    n_pages = page_table.shape[1]
    w_buf = win_cache_t.shape[1]
    assert w_buf % KT == 0 and t_new == SUBLANES and NSA_HEADS * t_new <= SROWS and PAGE_SIZE == KT
    n_wt = w_buf // KT
    n_cmp = (past_len + t_new - CMP_LEN) // CMP_STRIDE + 1
    n_sel = -(-(past_len + t_new) // SEL_LEN)
    n_sb = math.gcd(n_seq, 4)
    rows = n_sb * t_new
    page_spec = lambda sb, p: pl.BlockSpec((ROWW, PAGE_SIZE), lambda b, pt: (pt[n_sb * b + sb, p], 0))
    in_specs = [
        pl.BlockSpec((rows, NSA_QW), lambda b, pt: (b, C_NQ // NSA_QW)),
        pl.BlockSpec((rows, ROWW), lambda b, pt: (b, C_SLC // ROWW)),
        pl.BlockSpec((rows, ROWW), lambda b, pt: (b, C_WIN // ROWW)),
        pl.BlockSpec((rows, TAILW), lambda b, pt: (b, C_TAIL // TAILW)),
        pl.BlockSpec((n_sb, NBLK, NSA_KVW), lambda b, pt: (b, 0, 0)),
        pl.BlockSpec((n_sb, NBLK, NSA_KVW), lambda b, pt: (b, 0, 0)),
        pl.BlockSpec((n_sb * ROWW, w_buf), lambda b, pt: (b, 0)),
    ] + [page_spec(sb, p) for sb in range(n_sb) for p in range(n_pages)] + [
        pl.BlockSpec(ovl.shape, lambda b, pt: (0, 0)),
        pl.BlockSpec(bexp.shape, lambda b, pt: (0, 0)),
        pl.BlockSpec(gexp.shape, lambda b, pt: (0, 0, 0)),
    ]
    return pl.pallas_call(
        functools.partial(_nsa_sample_kernel, n_sb=n_sb, n_pages=n_pages, n_wt=n_wt, t_new=t_new,
                          past_len=past_len, n_cmp=n_cmp, n_sel=n_sel),
        grid_spec=pltpu.PrefetchScalarGridSpec(
            num_scalar_prefetch=1, grid=(n_seq // n_sb,), in_specs=in_specs,
            out_specs=pl.BlockSpec((rows, NSA_QW), lambda b, pt: (b, 0)),
            scratch_shapes=[pltpu.VMEM((n_sb * NSA_HEADS * t_new, LANES), F32),
                            pltpu.VMEM((rows, NSA_QW), F32)]),
        out_shape=jax.ShapeDtypeStruct((n_seq * t_new, NSA_QW), F32),
        compiler_params=_cparams(("parallel",)),
        name="nsa_sample",
    )(page_table, proj, proj, proj, proj, kc, vc, win_cache_t, *([slc_pool_t] * (n_sb * n_pages)), ovl, bexp, gexp)


def _out_ln_kernel(og_ref, on_ref, wg_ref, wn_ref, x_ref, g_ref, b_ref, y_ref, yb_ref, *, alpha):
    tm = x_ref.shape[0]
    rows = math.gcd(tm, 256)
    for r0 in range(0, tm, rows):
        sl = slice(r0, r0 + rows)
        m = _dot(og_ref[sl, :].astype(BF16), wg_ref[...]) + _dot(on_ref[sl, :].astype(BF16), wn_ref[...])
        y = _layer_norm(alpha * x_ref[sl, :] + m, g_ref[...], b_ref[...])
        y_ref[sl, :] = y
        yb_ref[sl, :] = y.astype(BF16)


def _out_ln(og, on, wg, wn, x, g, b, alpha, tm):
    n, d = x.shape
    assert n % tm == 0
    return pl.pallas_call(
        functools.partial(_out_ln_kernel, alpha=alpha),
        grid=(n // tm,),
        in_specs=[
            pl.BlockSpec((tm, og.shape[1]), lambda i: (i, 0)),
            pl.BlockSpec((tm, on.shape[1]), lambda i: (i, 0)),
            pl.BlockSpec(wg.shape, lambda i: (0, 0)),
            pl.BlockSpec(wn.shape, lambda i: (0, 0)),
            pl.BlockSpec((tm, d), lambda i: (i, 0)),
            pl.BlockSpec((1, d), lambda i: (0, 0)),
            pl.BlockSpec((1, d), lambda i: (0, 0)),
        ],
        out_specs=[pl.BlockSpec((tm, d), lambda i: (i, 0)), pl.BlockSpec((tm, d), lambda i: (i, 0))],
        out_shape=[jax.ShapeDtypeStruct((n, d), F32), jax.ShapeDtypeStruct((n, d), BF16)],
        compiler_params=_cparams(("parallel",)),
        name="out_ln",
    )(og, on, wg, wn, x, g, b)


def _rope_tables(pos):
    half = NSA_HD // 2
    freq = ROPE_THETA ** (-jnp.arange(half, dtype=F32) / half)
    ang = pos.astype(F32)[:, None] * freq[None, :]
    cos, sin = jnp.cos(ang), jnp.sin(ang)
    cos_h = jnp.concatenate([cos, cos], -1)
    sin_h = jnp.concatenate([-sin, sin], -1)
    reps = LANES // NSA_HD
    return jnp.tile(cos_h, (1, reps)), jnp.tile(sin_h, (1, reps))


def _reorder_w_in(w):
    d = w.shape[0]
    sizes = (GLA_KW, GLA_KW, GLA_VW, GLA_GATE_RANK, GLA_VW, NSA_QW) + (NSA_KVW,) * 6 + (NSA_HEADS * 3,)
    offs = np.concatenate([[0], np.cumsum(sizes)])
    gq, gk, gv, ga, gr, nq, kc, vc, ks, vs, kw, vw, ng = [w[:, offs[i]:offs[i + 1]] for i in range(len(sizes))]
    pad = jnp.zeros((d, TAILW - GLA_GATE_RANK - NSA_HEADS * 3), w.dtype)
    return jnp.concatenate([gq, gk, gv, gr, nq, kc, vc, ks, vs, kw, vw, ga, ng, pad], axis=1)


def _block_diag2(w):
    z = jnp.zeros_like(w)
    return jnp.concatenate([jnp.concatenate([w, z], -1), jnp.concatenate([z, w], -1)], -2)


def _compress_params(pos, w1):
    half = CMP_LEN // 2
    hidden = w1.shape[-1]
    w1r = w1.reshape(2, 2, half, NSA_HD, hidden)
    eye = jnp.eye(2, dtype=w1.dtype)
    w1p = jnp.einsum("hg,palDn->plhDgan", eye, w1r).reshape(2, half * 2 * NSA_HD, 2 * 2 * hidden)
    posr = jnp.tile(pos.reshape(2, 2, half, 1, NSA_HD), (1, 1, 1, 2, 1)).reshape(2, 2, 1, half * 2 * NSA_HD)
    pos16 = jnp.tile(posr, (1, 1, SUBLANES, 1)).reshape(2, 2 * SUBLANES, half * 2 * NSA_HD)
    return pos16.astype(BF16), w1p.astype(BF16)


def _tile_size(n, pref):
    t = math.gcd(n, pref)
    assert t % SUBLANES == 0
    return t


def _const_tables(n_keys):
    cs = np.arange(NBLK)[:, None] * CMP_STRIDE
    ss = np.arange(NBLK)[None, :] * SEL_LEN
    ovl = ((cs < ss + SEL_LEN) & (cs + CMP_LEN > ss)).astype(np.float32)
    bexp = (np.arange(NBLK)[:, None] == (np.arange(n_keys)[None, :] // SEL_LEN)).astype(np.float32)
    gexp = np.zeros((3, LANES, NSA_QW), np.float32)
    for c in range(3):
        for j in range(NSA_HEADS):
            gexp[c, GATE_LANE0 + 3 * j + c, j * NSA_HD:(j + 1) * NSA_HD] = 1.0
    return jnp.asarray(ovl, BF16), jnp.asarray(bexp, BF16), jnp.asarray(gexp, BF16)


def kernel(x_prompt, x_sample, cache_cmp_kv, cache_slc_kv, cache_win_kv, state_gla, page_table,
           w_ffn1_in, w_ffn1_out, w_in, w_gla_gate2, b_gla_gate2, gla_norm_g,
           cmp_pos, cmp_w1, cmp_w2, w_out, w_ffn2_in, w_ffn2_out, ln_g, ln_b):
    B, T, D = x_prompt.shape
    DB, TS, _ = x_sample.shape
    depth = w_in.shape[0]
    n_pages = page_table.shape[1]
    past_len = n_pages * PAGE_SIZE
    w_buf = cache_win_kv.shape[2]
    alpha = (2.0 * depth) ** 0.25
    assert w_buf == WINDOW and T % TQ == 0 and past_len % KT == 0

    cos_p, sin_p = _rope_tables(jnp.arange(T, dtype=jnp.int32))
    cos_s, sin_s = _rope_tables(past_len + jnp.arange(TS, dtype=jnp.int32))
    ovl, bexp_p, gexp = _const_tables(T)
    _, bexp_s, _ = _const_tables(past_len + KT)

    hp = x_prompt.reshape(B * T, D)
    hs = x_sample.reshape(DB * TS, D)
    tm_s = _tile_size(DB * TS, PROJ_TM)
    cos_s, sin_s = (jnp.tile(t, (tm_s // TS, 1)) for t in (cos_s, sin_s))
    outs = {k: [] for k in ("cmp_p", "slc_p", "win_p", "gla_p", "cmp_s", "slc_s", "win_s", "gla_s")}

    for l in range(depth):
        wf1i, wf1o = w_ffn1_in[l].astype(BF16), w_ffn1_out[l].astype(BF16)
        wf2i, wf2o = w_ffn2_in[l].astype(BF16), w_ffn2_out[l].astype(BF16)
        wi = _reorder_w_in(w_in[l].astype(BF16))
        wo_g, wo_n = w_out[l][:GLA_VW].astype(BF16), w_out[l][GLA_VW:].astype(BF16)
        lg, lb = ln_g[l][:, None, :], ln_b[l][:, None, :]
        w2h = jnp.zeros((LANES, GLA_KW), F32).at[:GLA_GATE_RANK].set(w_gla_gate2[l]).astype(BF16)
        b2h = b_gla_gate2[l][None, :]
        gn = gla_norm_g[l][None, :]
        pos2, w1bd = _compress_params(cmp_pos[l], cmp_w1[l])
        w2bd = _block_diag2(cmp_w2[l]).astype(BF16)

        h1, h1b = _ffn_ln(hp, wf1i, wf1o, lg[0], lb[0], alpha, _tile_size(B * T, FFN_TM), FFN_TF)
        proj, cmp_t, slc_t, win_t = _in_proj(h1b, wi, cos_p, sin_p, _tile_size(T, PROJ_TM), PROJ_TN, True)
        og, gla_state = _gla(proj, w2h, b2h, gn, jnp.zeros((B, GLA_HEADS, GLA_DK, GLA_DV), F32), B, T)
        n_grp = T // CMP_STRIDE
        n_cmp = (T - CMP_LEN) // CMP_STRIDE + 1
        kc, vc = _compress([proj], [pl.BlockSpec((T, ROWW), lambda b: (b, C_CMP // ROWW))],
                           pos2, w1bd, w2bd, B, n_grp, n_cmp)
        kc, vc = (jnp.pad(a, ((0, 0), (0, NBLK - n_grp), (0, 0))) for a in (kc, vc))
        on = _nsa_prompt(proj, kc, vc, ovl, bexp_p, gexp, B, T)
        h2, h2b = _out_ln(og, on, wo_g, wo_n, h1, lg[1], lb[1], alpha, _tile_size(B * T, OUT_TM))
        hp, _ = _ffn_ln(h2, wf2i, wf2o, lg[2], lb[2], alpha, _tile_size(B * T, FFN_TM), FFN_TF)
        row6 = lambda a, n, t: a.reshape(n, t, 2, NSA_KV, NSA_HD)
        row6_t = lambda a: a.reshape(a.shape[0], 2, NSA_KV, NSA_HD, a.shape[2]).transpose(0, 4, 1, 2, 3)
        outs["cmp_p"].append(row6_t(cmp_t))
        outs["slc_p"].append(row6_t(slc_t))
        outs["win_p"].append(row6_t(win_t[:, :, T - w_buf:]))
        outs["gla_p"].append(gla_state)

        s1, s1b = _ffn_ln(hs, wf1i, wf1o, lg[0], lb[0], alpha, _tile_size(DB * TS, FFN_TM), FFN_TF)
        (sproj,) = _in_proj(s1b, wi, cos_s, sin_s, tm_s, PROJ_TN, False)
        sog, sgla_state = _gla(sproj, w2h, b2h, gn, state_gla[l], DB, TS)
        feature_major = lambda c: c.transpose(0, 2, 3, 4, 1).reshape(c.shape[0] * ROWW, c.shape[1])
        cmp_pool_t = feature_major(cache_cmp_kv[l])
        slc_pool_t = feature_major(cache_slc_kv[l])
        win_cache_t = feature_major(cache_win_kv[l])
        n_grp_s = past_len // CMP_STRIDE
        n_cmp_s = (past_len + TS - CMP_LEN) // CMP_STRIDE + 1
        page_specs = [pl.BlockSpec((ROWW, PAGE_SIZE), lambda b, pt, p=p: (pt[b, p], 0)) for p in range(n_pages)]
        skc, svc = _compress([cmp_pool_t] * n_pages, page_specs, pos2, w1bd, w2bd, DB, n_grp_s, n_cmp_s,
                             scalar_args=(page_table,), feature_major=True)
        skc, svc = (jnp.pad(a, ((0, 0), (0, NBLK - n_grp_s), (0, 0))) for a in (skc, svc))
        son = _nsa_sample(sproj, skc, svc, slc_pool_t, win_cache_t, page_table, ovl, bexp_s, gexp, DB, TS, past_len)
        s2, s2b = _out_ln(sog, son, wo_g, wo_n, s1, lg[1], lb[1], alpha, _tile_size(DB * TS, OUT_TM))
        hs, _ = _ffn_ln(s2, wf2i, wf2o, lg[2], lb[2], alpha, _tile_size(DB * TS, FFN_TM), FFN_TF)
        new_win = row6(sproj[:, C_WIN:C_WIN + ROWW], DB, TS)
        outs["cmp_s"].append(row6(sproj[:, C_CMP:C_CMP + ROWW], DB, TS))
        outs["slc_s"].append(row6(sproj[:, C_SLC:C_SLC + ROWW], DB, TS))
        outs["win_s"].append(jnp.concatenate([cache_win_kv[l], new_win], axis=1)[:, -w_buf:])
        outs["gla_s"].append(sgla_state)

    st = lambda k: jnp.stack(outs[k])
    return (hp.reshape(B, T, D), hs.reshape(DB, TS, D),
            st("cmp_p"), st("slc_p"), st("win_p"), st("gla_p"),
            st("cmp_s"), st("slc_s"), st("win_s"), st("gla_s"))
```

```python
import functools
import math

import numpy as np
import jax
import jax.numpy as jnp
from jax import lax
from jax.experimental import pallas as pl
from jax.experimental.pallas import tpu as pltpu

F32 = jnp.float32
BF16 = jnp.bfloat16

GLA_HEADS = 4
GLA_DK = 128
GLA_DV = 256
GLA_GATE_RANK = 16
GLA_TAU = 16.0
NSA_HEADS = 16
NSA_KV = 4
NSA_GROUP = NSA_HEADS // NSA_KV
NSA_HD = 64
CMP_LEN = 32
CMP_STRIDE = 16
SEL_LEN = 64
SEL_TOPN = 16
WINDOW = 512
PAGE_SIZE = 128
ROPE_THETA = 10000.0
LN_EPS = 1e-5
NEG = -1e30

LANES = 128
SUBLANES = 8
VMEM_LIMIT_BYTES = 56 * 1024 * 1024

GLA_KW = GLA_HEADS * GLA_DK
GLA_VW = GLA_HEADS * GLA_DV
NSA_QW = NSA_HEADS * NSA_HD
NSA_KVW = NSA_KV * NSA_HD
ROWW = 2 * NSA_KVW
C_GQ = 0
C_GK = C_GQ + GLA_KW
C_GV = C_GK + GLA_KW
C_GR = C_GV + GLA_VW
C_NQ = C_GR + GLA_VW
C_CMP = C_NQ + NSA_QW
C_SLC = C_CMP + ROWW
C_WIN = C_SLC + ROWW
C_TAIL = C_WIN + ROWW
TAILW = ROWW
PROJ_W = C_TAIL + TAILW
GATE_LANE0 = GLA_GATE_RANK

FFN_TM = 512
FFN_TF = 512
PROJ_TM = 1024
PROJ_TN = 1024
OUT_TM = 512
GLA_TB = 1024
GLA_CHUNK = 128
GLA_SUB = 32
TQ = 256
SROWS = 128
KT = 128
FKT = 256
NBLK = 128


def _cparams(sem):
    return pltpu.CompilerParams(dimension_semantics=sem, vmem_limit_bytes=VMEM_LIMIT_BYTES)


def _dot(a, b):
    return jnp.dot(a, b, preferred_element_type=F32)


def _dot_nt(a, b):
    return lax.dot_general(a, b, (((1,), (1,)), ((), ())), preferred_element_type=F32)


def _layer_norm(y, g, b):
    mu = jnp.mean(y, -1, keepdims=True)
    d = y - mu
    var = jnp.mean(d * d, -1, keepdims=True)
    return d * lax.rsqrt(var + LN_EPS) * g + b


def _split_bf16(x, n):
    parts = []
    for _ in range(n - 1):
        hi = x.astype(BF16)
        parts.append(hi)
        x = x - hi.astype(F32)
    parts.append(x.astype(BF16))
    return parts


def _ffn_ln_kernel(x_ref, wg_ref, wu_ref, wo_ref, g_ref, b_ref, y_ref, yb_ref, xb_ref, acc_ref, *, alpha):
    j = pl.program_id(1)

    @pl.when(j == 0)
    def _():
        xb_ref[...] = x_ref[...].astype(BF16)
        acc_ref[...] = jnp.zeros_like(acc_ref)

    xb = xb_ref[...]
    h = _dot(xb, wg_ref[...])
    u = _dot(xb, wu_ref[...])
    a = (h * jax.nn.sigmoid(h) * u).astype(BF16)
    acc_ref[...] += _dot(a, wo_ref[...])

    @pl.when(j == pl.num_programs(1) - 1)
    def _():
        y = _layer_norm(alpha * x_ref[...] + 0.5 * acc_ref[...], g_ref[...], b_ref[...])
        y_ref[...] = y
        yb_ref[...] = y.astype(BF16)


def _ffn_ln(x, w_in, w_out, g, b, alpha, tm, tf):
    n, d = x.shape
    dff = w_out.shape[0]
    nf = dff // tf
    assert n % tm == 0 and dff % tf == 0
    return pl.pallas_call(
        functools.partial(_ffn_ln_kernel, alpha=alpha),
        grid=(n // tm, nf),
        in_specs=[
            pl.BlockSpec((tm, d), lambda i, j: (i, 0)),
            pl.BlockSpec((d, tf), lambda i, j: (0, j)),
            pl.BlockSpec((d, tf), lambda i, j: (0, j + nf)),
            pl.BlockSpec((tf, d), lambda i, j: (j, 0)),
            pl.BlockSpec((1, d), lambda i, j: (0, 0)),
            pl.BlockSpec((1, d), lambda i, j: (0, 0)),
        ],
        out_specs=[pl.BlockSpec((tm, d), lambda i, j: (i, 0)),
                   pl.BlockSpec((tm, d), lambda i, j: (i, 0))],
        out_shape=[jax.ShapeDtypeStruct((n, d), F32), jax.ShapeDtypeStruct((n, d), BF16)],
        scratch_shapes=[pltpu.VMEM((tm, d), BF16), pltpu.VMEM((tm, d), F32)],
        compiler_params=_cparams(("parallel", "arbitrary")),
        name="ffn_ln",
    )(x, w_in, w_in, w_out, g, b)


def _rope_lanes(x, cos, sin_signed):
    lane = lax.broadcasted_iota(jnp.int32, x.shape, 1)
    first_half = (lane & (NSA_HD - 1)) < (NSA_HD // 2)
    partner = jnp.where(first_half, pltpu.roll(x, LANES - NSA_HD // 2, 1), pltpu.roll(x, NSA_HD // 2, 1))
    return x * cos + partner * sin_signed


def _is_rotary_col(col):
    if C_NQ <= col < C_CMP:
        return True
    return C_CMP <= col < C_TAIL and (col - C_CMP) % ROWW < NSA_KVW


def _in_proj_kernel(x_ref, w_ref, cos_ref, sin_ref, o_ref, *t_refs, tn):
    j = pl.program_id(1)
    chunk = 2 * LANES

    for jj in range(PROJ_W // tn):
        @pl.when(j == jj)
        def _(jj=jj):
            for c0 in range(0, tn, chunk):
                acc = _dot(x_ref[...], w_ref[:, c0:c0 + chunk])
                for s0 in range(0, chunk, LANES):
                    col = jj * tn + c0 + s0
                    val = acc[:, s0:s0 + LANES]
                    if _is_rotary_col(col):
                        val = _rope_lanes(val, cos_ref[...], sin_ref[...])
                    o_ref[:, c0 + s0:c0 + s0 + LANES] = val
                    group, off = divmod(col - C_CMP, ROWW)
                    if t_refs and 0 <= group < len(t_refs):
                        t_refs[group][0, off:off + LANES, :] = val.T


def _in_proj(xb, w, cos, sin, tm, tn, emit_t):
    n, d = xb.shape
    period = cos.shape[0]
    assert n % tm == 0 and period % tm == 0 and PROJ_W % tn == 0 and tn % ROWW == 0
    npb = period // tm
    out_specs = [pl.BlockSpec((tm, tn), lambda i, j: (i, j))]
    out_shape = [jax.ShapeDtypeStruct((n, PROJ_W), F32)]
    if emit_t:
        out_specs += [pl.BlockSpec((1, ROWW, tm), lambda i, j: (i // npb, 0, i % npb))] * 3
        out_shape += [jax.ShapeDtypeStruct((n // period, ROWW, period), F32)] * 3
    return pl.pallas_call(
        functools.partial(_in_proj_kernel, tn=tn),
        grid=(n // tm, PROJ_W // tn),
        in_specs=[
            pl.BlockSpec((tm, d), lambda i, j: (i, 0)),
            pl.BlockSpec((d, tn), lambda i, j: (0, j)),
            pl.BlockSpec((tm, LANES), lambda i, j: (i % npb, 0)),
            pl.BlockSpec((tm, LANES), lambda i, j: (i % npb, 0)),
        ],
        out_specs=out_specs,
        out_shape=out_shape,
        compiler_params=_cparams(("parallel", "arbitrary")),
        name="in_proj",
    )(xb, w, cos, sin)


def _gla_kernel(q_ref, k_ref, v_ref, r_ref, ga_ref, w2_ref, b2_ref, gn_ref, s0_ref,
                o_ref, sout_ref, st_ref, *, n_valid, n_chunks, n_sb):
    t = pl.program_id(1)
    C, S, H = GLA_CHUNK, GLA_SUB, GLA_HEADS

    @pl.when(t == 0)
    def _():
        for sb in range(n_sb):
            for h in range(H):
                st_ref[sb * H + h] = s0_ref[sb, h].T

    row_k = lax.broadcasted_iota(jnp.int32, (C, GLA_DK), 0)
    row_a = lax.broadcasted_iota(jnp.int32, (C, GLA_KW), 0)
    row_c = lax.broadcasted_iota(jnp.int32, (C, C), 0)
    col_c = lax.broadcasted_iota(jnp.int32, (C, C), 1)
    causal = row_c >= col_c
    tri = causal.astype(BF16)

    def load(ref, r0, sb, c0, w):
        if n_valid >= C:
            return ref[pl.ds(r0, C), c0:c0 + w]
        x = ref[sb * n_valid:(sb + 1) * n_valid, c0:c0 + w]
        return jnp.concatenate([x, jnp.zeros((C - n_valid, w), x.dtype)], axis=0)

    def head_chain(r0, sb, h, b):
        q = load(q_ref, r0, sb, h * GLA_DK, GLA_DK) * (GLA_DK ** -0.5)
        k = load(k_ref, r0, sb, h * GLA_DK, GLA_DK)
        v = load(v_ref, r0, sb, h * GLA_DV, GLA_DV)
        b_last = b[C - 1:C, :]
        yield
        strips = []
        n_strips = min(C // S, -(-n_valid // S))
        for i in range(n_strips):
            rows = slice(i * S, (i + 1) * S)
            anchor = jnp.zeros((1, GLA_DK), F32) if i == 0 else b[i * S - 1:i * S, :]
            qi = q[rows] * jnp.exp(b[rows] - anchor)
            ki = k * jnp.exp(jnp.where(row_k < (i + 1) * S, anchor - b, 0.0))
            strips.append(_dot_nt(qi.astype(BF16), ki.astype(BF16)))
            yield
        if n_strips * S < C:
            strips.append(jnp.zeros((C - n_strips * S, C), F32))
        attn = jnp.where(causal, jnp.concatenate(strips, axis=0), 0.0)
        yield
        st = st_ref[sb * H + h]
        o = (_dot_nt((q * jnp.exp(b)).astype(BF16), st.astype(BF16))
             + _dot(attn.astype(BF16), v.astype(BF16)))
        yield
        ke = k * jnp.exp(b_last - b)
        st_ref[sb * H + h] = st * jnp.exp(b_last) + _dot(v.T.astype(BF16), ke.astype(BF16))
        yield
        rms = lax.rsqrt(jnp.mean(o * o, -1, keepdims=True) + LN_EPS)
        r = load(r_ref, r0, sb, h * GLA_DV, GLA_DV)
        out = (o * rms * gn_ref[...] * (r * jax.nn.sigmoid(r))).astype(o_ref.dtype)
        cols = slice(h * GLA_DV, (h + 1) * GLA_DV)
        if n_valid >= C:
            o_ref[pl.ds(r0, C), cols] = out
        else:
            o_ref[sb * n_valid:(sb + 1) * n_valid, cols] = out[:n_valid]

    def chunk(c, carry):
        r0 = pl.multiple_of(c * C, C)
        chains = []
        for sb in range(n_sb):
            pre = _dot(load(ga_ref, r0, sb, 0, LANES).astype(BF16), w2_ref[...]) + b2_ref[...]
            la = (jnp.minimum(pre, 0.0) - jnp.log(1.0 + jnp.exp(-jnp.abs(pre)))) * (1.0 / GLA_TAU)
            if n_valid < C:
                la = jnp.where(row_a < n_valid, la, 0.0)
            la_hi, la_lo = _split_bf16(la, 2)
            b_all = _dot(tri, la_hi) + _dot(tri, la_lo)
            chains += [head_chain(r0, sb, h, b_all[:, h * GLA_DK:(h + 1) * GLA_DK]) for h in range(H)]
        _lockstep(chains)
        return carry

    lax.fori_loop(0, n_chunks, chunk, 0)

    @pl.when(t == pl.num_programs(1) - 1)
    def _():
        for sb in range(n_sb):
            for h in range(H):
                sout_ref[sb, h] = st_ref[sb * H + h].T


def _gla(proj, w2h, b2h, gn, s0, n_seq, t_seq):
    C, H = GLA_CHUNK, GLA_HEADS
    if t_seq >= C:
        tb = math.gcd(t_seq, GLA_TB)
        assert tb % C == 0
        n_valid, n_chunks, n_sb = C, tb // C, 1
    else:
        tb, n_valid, n_chunks = t_seq, t_seq, 1
        n_sb = math.gcd(n_seq, 4)
        assert t_seq % SUBLANES == 0
    nt = t_seq // tb
    rows = n_sb * tb
    rowblk = lambda b, t: b * nt + t
    return pl.pallas_call(
        functools.partial(_gla_kernel, n_valid=n_valid, n_chunks=n_chunks, n_sb=n_sb),
        grid=(n_seq // n_sb, nt),
        in_specs=[
            pl.BlockSpec((rows, GLA_KW), lambda b, t: (rowblk(b, t), C_GQ // GLA_KW)),
            pl.BlockSpec((rows, GLA_KW), lambda b, t: (rowblk(b, t), C_GK // GLA_KW)),
            pl.BlockSpec((rows, GLA_VW), lambda b, t: (rowblk(b, t), C_GV // GLA_VW)),
            pl.BlockSpec((rows, GLA_VW), lambda b, t: (rowblk(b, t), C_GR // GLA_VW)),
            pl.BlockSpec((rows, LANES), lambda b, t: (rowblk(b, t), C_TAIL // LANES)),
            pl.BlockSpec((LANES, GLA_KW), lambda b, t: (0, 0)),
            pl.BlockSpec((1, GLA_KW), lambda b, t: (0, 0)),
            pl.BlockSpec((1, GLA_DV), lambda b, t: (0, 0)),
            pl.BlockSpec((n_sb, H, GLA_DK, GLA_DV), lambda b, t: (b, 0, 0, 0)),
        ],
        out_specs=[
            pl.BlockSpec((rows, GLA_VW), lambda b, t: (rowblk(b, t), 0)),
            pl.BlockSpec((n_sb, H, GLA_DK, GLA_DV), lambda b, t: (b, 0, 0, 0)),
        ],
        out_shape=[jax.ShapeDtypeStruct((n_seq * t_seq, GLA_VW), BF16 if t_seq >= C else F32),
                   jax.ShapeDtypeStruct((n_seq, H, GLA_DK, GLA_DV), F32)],
        scratch_shapes=[pltpu.VMEM((n_sb * H, GLA_DV, GLA_DK), F32)],
        compiler_params=_cparams(("parallel", "arbitrary")),
        name="gla",
    )(proj, proj, proj, proj, proj, w2h, b2h, gn, s0)


def _gelu_tanh(x):
    return 0.5 * x * (1.0 + jnp.tanh(math.sqrt(2.0 / math.pi) * (x + 0.044715 * (x * x * x))))


def _compress_kernel(*refs, n_scalar, n_src, n_cmp, feature_major):
    refs = refs[n_scalar:]
    src = refs[:n_src]
    pos_ref, w1_ref, w2_ref, kc_ref, vc_ref, slab_ref, u_ref = refs[n_src:]
    r_src = src[0].shape[1] if feature_major else src[0].shape[0]
    n_grp = n_src * r_src // CMP_STRIDE
    row = lax.broadcasted_iota(jnp.int32, (n_grp, LANES), 0)
    outs = (kc_ref, vc_ref)
    hw = 2 * LANES
    for part in range(2):
        pos_bias = _dot(pos_ref[part], w1_ref[part])
        for pair in range(NSA_KVW // LANES):
            c0 = part * NSA_KVW + pair * LANES
            for n, s in enumerate(src):
                slab_ref[n * r_src:(n + 1) * r_src, :] = s[c0:c0 + LANES, :].T if feature_major else s[:, c0:c0 + LANES]
            for l in range(CMP_STRIDE):
                u_ref[:, l * LANES:(l + 1) * LANES] = slab_ref[pl.ds(l, n_grp, stride=CMP_STRIDE), :].astype(BF16)
            ab = _dot(u_ref[...], w1_ref[part])
            hidden = []
            for hh in range(2):
                a = ab[:, hh * hw:hh * hw + LANES] + pos_bias[0:1, hh * hw:hh * hw + LANES]
                b = ab[:, hh * hw + LANES:(hh + 1) * hw] + pos_bias[SUBLANES:SUBLANES + 1, hh * hw + LANES:(hh + 1) * hw]
                hidden.append(a + pltpu.roll(b, n_grp - 1, 0))
            out = _dot(_gelu_tanh(jnp.concatenate(hidden, axis=1)).astype(BF16), w2_ref[part])
            outs[part][0, :, pair * LANES:(pair + 1) * LANES] = jnp.where(row < n_cmp, out, 0.0)


def _compress(srcs, src_specs, pos2, w1bd, w2bd, n_seq, n_grp, n_cmp, scalar_args=(), feature_major=False):
    n_src = len(srcs)
    nsp = len(scalar_args)
    in_specs = list(src_specs) + [
        pl.BlockSpec(pos2.shape, lambda b, *_: (0, 0, 0), pipeline_mode=pl.Buffered(1)),
        pl.BlockSpec(w1bd.shape, lambda b, *_: (0, 0, 0), pipeline_mode=pl.Buffered(1)),
        pl.BlockSpec(w2bd.shape, lambda b, *_: (0, 0, 0), pipeline_mode=pl.Buffered(1)),
    ]
    out_spec = pl.BlockSpec((1, n_grp, NSA_KVW), lambda b, *_: (b, 0, 0))
    return pl.pallas_call(
        functools.partial(_compress_kernel, n_scalar=nsp, n_src=n_src, n_cmp=n_cmp, feature_major=feature_major),
        grid_spec=pltpu.PrefetchScalarGridSpec(
            num_scalar_prefetch=nsp, grid=(n_seq,), in_specs=in_specs, out_specs=[out_spec, out_spec],
            scratch_shapes=[pltpu.VMEM((n_grp * CMP_STRIDE, LANES), F32),
                            pltpu.VMEM((n_grp, CMP_STRIDE * LANES), BF16)]),
        out_shape=[jax.ShapeDtypeStruct((n_seq, n_grp, NSA_KVW), F32)] * 2,
        compiler_params=_cparams(("parallel",)),
        name="compress",
    )(*scalar_args, *srcs, pos2, w1bd, w2bd)


def _build_q(q_ref, qs_ref, tq):
    lane = lax.broadcasted_iota(jnp.int32, (tq, LANES), 1)
    for j in range(NSA_HEADS):
        h = j // NSA_GROUP
        x = q_ref[:, (j // 2) * LANES:(j // 2 + 1) * LANES] * (NSA_HD ** -0.5)
        if (j % 2) != (h % 2):
            x = pltpu.roll(x, NSA_HD, 1)
        keep = (lane >= NSA_HD) if (h % 2) else (lane < NSA_HD)
        qs_ref[j * tq:(j + 1) * tq, :] = jnp.where(keep, x, 0.0).astype(qs_ref.dtype)


def _pair_slab(h):
    return slice((h // 2) * LANES, (h // 2 + 1) * LANES)


def _tile_rows(x, n):
    return jnp.concatenate([x] * n, axis=0)


def _lockstep(chains):
    chains = list(chains)
    out = [None] * len(chains)
    live = set(range(len(chains)))
    while live:
        for k in sorted(live):
            try:
                next(chains[k])
            except StopIteration as stop:
                out[k] = stop.value
                live.discard(k)
    return out


def _cmp_attention(qh, kc, vc, qpos, n_cmp):
    tq = qpos.shape[0]
    blk = lax.broadcasted_iota(jnp.int32, (tq, NBLK), 1)
    valid = ((blk * CMP_STRIDE + (CMP_LEN - 1) <= qpos) & (blk < n_cmp)).astype(F32)
    valid4 = _tile_rows(valid, NSA_GROUP)
    s = jnp.where(valid4 > 0.5, _dot_nt(qh, kc.astype(BF16)), NEG)
    yield
    e = jnp.exp(s - jnp.max(s, -1, keepdims=True))
    yield
    p = e * (valid4 * (1.0 / jnp.sum(e, -1, keepdims=True)))
    yield
    o = _dot(p.astype(BF16), vc.astype(BF16))
    psum = p[0:tq]
    for g in range(1, NSA_GROUP):
        psum = psum + p[g * tq:(g + 1) * tq]
    return o, psum


def _select_blocks(psum, ovl, qpos, n_sel):
    imp = sum(_dot(t, ovl) for t in _split_bf16(psum, 3))
    yield
    blk = lax.broadcasted_iota(jnp.int32, psum.shape, 1)
    ss = blk * SEL_LEN
    cur = qpos - (qpos & (SEL_LEN - 1))
    forced = (ss == 0) | (ss == cur) | (ss == cur - SEL_LEN)
    score = jnp.where(ss <= qpos, jnp.where(forced, 1e9, imp), -1e9)
    nb = -(-n_sel // SUBLANES) * SUBLANES
    st = score.T[0:nb]
    yield
    sub = lax.broadcasted_iota(jnp.int32, st.shape, 0)
    rank = jnp.zeros(st.shape, F32)
    for s in range(n_sel):
        other = st[s:s + 1, :]
        beats = (other > st) | ((other == st) & (sub > s))
        rank = rank + beats.astype(F32)
        if s % SUBLANES == SUBLANES - 1:
            yield
    sel_t = ((rank < SEL_TOPN) & (st > -1e8)).astype(F32)
    sel_t = jnp.concatenate([sel_t, jnp.zeros((NBLK - nb, st.shape[1]), F32)], axis=0)
    yield
    return sel_t.T


def _emit_branch(out_ref, o, gate, h, tq, first, denom_in_other_half=False):
    lane = lax.broadcasted_iota(jnp.int32, (tq, LANES), 1)
    for gp in range(NSA_GROUP // 2):
        y0 = o[(2 * gp) * tq:(2 * gp + 1) * tq]
        y1 = o[(2 * gp + 1) * tq:(2 * gp + 2) * tq]
        if h % 2:
            y0 = pltpu.roll(y0, NSA_HD, 1)
        else:
            y1 = pltpu.roll(y1, NSA_HD, 1)
        m = 2 * h + gp
        sl = slice(m * LANES, (m + 1) * LANES)
        val = jnp.where(lane < NSA_HD, y0, y1)
        if denom_in_other_half:
            val = val / pltpu.roll(jnp.where(lane < NSA_HD, y1, y0), NSA_HD, 1)
        val = val * gate[:, sl]
        if first:
            out_ref[:, sl] = val
        else:
            out_ref[:, sl] += val


def _expand_gates(tail, gexp_ref):
    g_hi, g_lo = _split_bf16(jax.nn.sigmoid(tail), 2)
    return [_dot(g_hi, gexp_ref[c]) + _dot(g_lo, gexp_ref[c]) for c in range(3)]


def _own_half(h, shape):
    lane = lax.broadcasted_iota(jnp.int32, shape, len(shape) - 1)
    return (lane >= NSA_HD) if (h % 2) else (lane < NSA_HD)


def _nsa_prompt_kernel(q_ref, slc_ref, win_ref, tail_ref, kc_ref, vc_ref, ovl_ref, bexp_ref, gexp_ref,
                       o_ref, qs_ref, mexp_ref, m_ref, acc_ref, out_ref, *, n_cmp, n_sel):
    i = pl.program_id(1)
    q0 = i * TQ
    G = NSA_GROUP
    R = G * TQ
    _build_q(q_ref, qs_ref, TQ)
    gates = _expand_gates(tail_ref[:, 0:LANES], gexp_ref)
    qpos = q0 + lax.broadcasted_iota(jnp.int32, (TQ, NBLK), 0)
    qpos_k = q0 + lax.broadcasted_iota(jnp.int32, (TQ, FKT), 0)
    lane_k = lax.broadcasted_iota(jnp.int32, (TQ, FKT), 1)
    n_kt = mexp_ref.shape[1]

    def head_chain(h):
        o_c, psum = yield from _cmp_attention(qs_ref[h * R:(h + 1) * R, :], kc_ref[0, :, _pair_slab(h)],
                                              vc_ref[0, :, _pair_slab(h)], qpos, n_cmp)
        yield
        sel = yield from _select_blocks(psum, ovl_ref[...], qpos, n_sel)
        _emit_branch(out_ref, o_c, gates[0], h, TQ, True)
        yield
        mexp = _dot(sel.astype(BF16), bexp_ref[...])
        for kt in range(n_kt):
            mexp_ref[h, kt] = mexp[:, kt * FKT:(kt + 1) * FKT].astype(BF16)

    _lockstep([head_chain(h) for h in range(NSA_KV)])

    def flash(src_ref, lo, hi, bias_fn):
        m_ref[...] = jnp.full(m_ref.shape, NEG, F32)
        acc_ref[...] = jnp.zeros(acc_ref.shape, F32)

        def body(kt, carry):
            k0 = pl.multiple_of(kt * FKT, FKT)
            kpos = k0 + lane_k
            kslab = [src_ref[pl.ds(k0, FKT), m * LANES:(m + 1) * LANES].astype(BF16) for m in range(2)]
            vslab = [src_ref[pl.ds(k0, FKT), NSA_KVW + m * LANES:NSA_KVW + (m + 1) * LANES] for m in range(2)]
            for h in range(NSA_KV):
                vv = jnp.where(_own_half(h, (FKT, LANES)), vslab[h // 2], 1.0).astype(BF16)
                s = _dot_nt(qs_ref[h * R:(h + 1) * R, :], kslab[h // 2]) + _tile_rows(bias_fn(h, kt, kpos), G)
                m_prev = m_ref[h]
                smax = s[:, 0:LANES]
                for c in range(1, FKT // LANES):
                    smax = jnp.maximum(smax, s[:, c * LANES:(c + 1) * LANES])
                m_new = jnp.maximum(m_prev, jnp.max(smax, -1, keepdims=True))
                alpha = jnp.exp(m_prev - m_new)
                p = jnp.exp(s - jnp.concatenate([m_new] * (FKT // LANES), axis=1))
                acc_ref[h] = alpha * acc_ref[h] + _dot(p.astype(BF16), vv)
                m_ref[h] = m_new
            return carry

        lax.fori_loop(lo, hi, body, 0)

    def sel_bias(h, kt, kpos):
        return jnp.where((mexp_ref[h, kt].astype(F32) > 0.5) & (kpos <= qpos_k), 0.0, NEG)

    def win_bias(h, kt, kpos):
        dist = qpos_k - kpos
        return jnp.where((dist >= 0) & (dist < WINDOW), 0.0, NEG)

    hi = (q0 + TQ + FKT - 1) // FKT
    flash(slc_ref, 0, hi, sel_bias)
    for h in range(NSA_KV):
        _emit_branch(out_ref, acc_ref[h], gates[1], h, TQ, False, denom_in_other_half=True)
    flash(win_ref, jnp.maximum(q0 - WINDOW, 0) // FKT, hi, win_bias)
    for h in range(NSA_KV):
        _emit_branch(out_ref, acc_ref[h], gates[2], h, TQ, False, denom_in_other_half=True)
    o_ref[...] = out_ref[...].astype(o_ref.dtype)


def _nsa_prompt(proj, kc, vc, ovl, bexp, gexp, n_seq, t_seq):
    assert t_seq % FKT == 0 and t_seq % TQ == 0 and TQ % LANES == 0 and NBLK == LANES
    nq = t_seq // TQ
    n_cmp = (t_seq - CMP_LEN) // CMP_STRIDE + 1
    n_sel = -(-t_seq // SEL_LEN)
    G = NSA_GROUP
    return pl.pallas_call(
        functools.partial(_nsa_prompt_kernel, n_cmp=n_cmp, n_sel=n_sel),
        grid=(n_seq, nq),
        in_specs=[
            pl.BlockSpec((TQ, NSA_QW), lambda b, i: (b * nq + i, C_NQ // NSA_QW)),
            pl.BlockSpec((t_seq, ROWW), lambda b, i: (b, C_SLC // ROWW)),
            pl.BlockSpec((t_seq, ROWW), lambda b, i: (b, C_WIN // ROWW)),
            pl.BlockSpec((TQ, TAILW), lambda b, i: (b * nq + i, C_TAIL // TAILW)),
            pl.BlockSpec((1, NBLK, NSA_KVW), lambda b, i: (b, 0, 0)),
            pl.BlockSpec((1, NBLK, NSA_KVW), lambda b, i: (b, 0, 0)),
            pl.BlockSpec(ovl.shape, lambda b, i: (0, 0)),
            pl.BlockSpec(bexp.shape, lambda b, i: (0, 0)),
            pl.BlockSpec(gexp.shape, lambda b, i: (0, 0, 0)),
        ],
        out_specs=pl.BlockSpec((TQ, NSA_QW), lambda b, i: (b * nq + i, 0)),
        out_shape=jax.ShapeDtypeStruct((n_seq * t_seq, NSA_QW), BF16),
        scratch_shapes=[
            pltpu.VMEM((NSA_HEADS * TQ, LANES), BF16),
            pltpu.VMEM((NSA_KV, t_seq // FKT, TQ, FKT), BF16),
            pltpu.VMEM((NSA_KV, G * TQ, LANES), F32),
            pltpu.VMEM((NSA_KV, G * TQ, LANES), F32),
            pltpu.VMEM((TQ, NSA_QW), F32),
        ],
        compiler_params=_cparams(("parallel", "arbitrary")),
        name="nsa_prompt",
    )(proj, proj, proj, proj, kc, vc, ovl, bexp, gexp)


def _nsa_sample_kernel(*refs, n_sb, n_pages, t_new, **kw):
    q_ref, snew_ref, wnew_ref, tail_ref, kc_ref, vc_ref, wcache_ref = refs[1:8]
    pages = refs[8:8 + n_sb * n_pages]
    ovl_ref, bexp_ref, gexp_ref, o_ref, qs_ref, out_ref = refs[8 + n_sb * n_pages:]
    chains = []
    for sb in range(n_sb):
        rows = pl.ds(sb * t_new, t_new)
        chains += _nsa_sample_chains(
            q_ref.at[rows], snew_ref.at[rows], wnew_ref.at[rows], tail_ref.at[rows], kc_ref.at[pl.ds(sb, 1)],
            vc_ref.at[pl.ds(sb, 1)], wcache_ref.at[pl.ds(sb * ROWW, ROWW)], pages[sb * n_pages:(sb + 1) * n_pages],
            ovl_ref, bexp_ref, gexp_ref, o_ref.at[rows],
            qs_ref.at[pl.ds(sb * NSA_HEADS * t_new, NSA_HEADS * t_new)], out_ref.at[rows], t_new=t_new, **kw)
    _lockstep(chains)


def _nsa_sample_chains(q_ref, snew_ref, wnew_ref, tail_ref, kc_ref, vc_ref, wcache_ref, pages, ovl_ref, bexp_ref,
                       gexp_ref, o_ref, qs_ref, out_ref, *, n_wt, t_new, past_len, n_cmp, n_sel):
    n_pages = len(pages)
    G = NSA_GROUP
    R = NSA_HEADS * t_new
    half = R // 2
    _build_q(q_ref, qs_ref, t_new)
    gates = _expand_gates(tail_ref[:, 0:LANES], gexp_ref)
    zeros = jnp.zeros((half, LANES), F32)
    q_all = jnp.concatenate([jnp.concatenate([qs_ref[0:half, :], zeros], axis=1),
                             jnp.concatenate([zeros, qs_ref[half:R, :]], axis=1)], axis=0).astype(BF16)
    row_t = lax.broadcasted_iota(jnp.int32, (SROWS, NBLK), 0) & (t_new - 1)
    qpos_blk = past_len + row_t
    pad_new = lambda x: jnp.concatenate([x, jnp.zeros((KT - t_new, x.shape[1]), x.dtype)], axis=0)
    window_out = []

    def attend(k_t_tiles, v_t_tiles, k_new, v_new, bias):
        s = [_dot(q_all, kt.astype(BF16)) for kt in k_t_tiles] + [_dot_nt(q_all, pad_new(k_new).astype(BF16))]
        yield
        s = jnp.concatenate(s, axis=1) + bias
        e = jnp.exp(s - jnp.max(s, -1, keepdims=True))
        yield
        p = (e / jnp.sum(e, -1, keepdims=True)).astype(BF16)
        yield
        n = len(k_t_tiles)
        o = _dot(p[:, n * KT:(n + 1) * KT], pad_new(v_new).astype(BF16))
        for c, vt in enumerate(v_t_tiles):
            o = o + _dot_nt(p[:, c * KT:(c + 1) * KT], vt.astype(BF16))
        return o

    def emit(o_all, gate, first):
        for h in range(NSA_KV):
            _emit_branch(out_ref, o_all[h * G * t_new:(h + 1) * G * t_new, _pair_slab(h)], gate, h, t_new, first)

    def window_chain():
        n_keys = (n_wt + 1) * KT
        kpos = (past_len - n_wt * KT) + lax.broadcasted_iota(jnp.int32, (R, n_keys), 1)
        dist = past_len + (lax.broadcasted_iota(jnp.int32, (R, n_keys), 0) & (t_new - 1)) - kpos
        bias = jnp.where((dist >= 0) & (dist < WINDOW), 0.0, NEG)
        o_w = yield from attend([wcache_ref[0:NSA_KVW, c * KT:(c + 1) * KT] for c in range(n_wt)],
                                [wcache_ref[NSA_KVW:ROWW, c * KT:(c + 1) * KT] for c in range(n_wt)],
                                wnew_ref[:, 0:NSA_KVW], wnew_ref[:, NSA_KVW:ROWW], bias)
        window_out.append(o_w)

    def main_chain():
        qpos_r = qpos_blk[0:R]
        blk = lax.broadcasted_iota(jnp.int32, (R, NBLK), 1)
        valid = ((blk * CMP_STRIDE + (CMP_LEN - 1) <= qpos_r) & (blk < n_cmp)).astype(F32)
        s = jnp.where(valid > 0.5, _dot_nt(q_all, kc_ref[0].astype(BF16)), NEG)
        yield
        e = jnp.exp(s - jnp.max(s, -1, keepdims=True))
        p = e / jnp.sum(e, -1, keepdims=True) * valid
        yield
        emit(_dot(p.astype(BF16), vc_ref[0].astype(BF16)), gates[0], True)

        psum = []
        for h in range(NSA_KV):
            acc = p[(h * G) * t_new:(h * G + 1) * t_new]
            for g in range(1, G):
                acc = acc + p[(h * G + g) * t_new:(h * G + g + 1) * t_new]
            psum.append(acc)
        psum = jnp.concatenate(psum + [jnp.zeros((SROWS - NSA_KV * t_new, NBLK), F32)], axis=0)
        sel = yield from _select_blocks(psum, ovl_ref[...], qpos_blk, n_sel)
        mexp = _dot(sel.astype(BF16), bexp_ref[...])
        mexp = jnp.concatenate([_tile_rows(mexp[h * t_new:(h + 1) * t_new], G) for h in range(NSA_KV)], axis=0)
        n_keys = (n_pages + 1) * KT
        kpos = lax.broadcasted_iota(jnp.int32, (R, n_keys), 1)
        qpos_k = past_len + (lax.broadcasted_iota(jnp.int32, (R, n_keys), 0) & (t_new - 1))
        bias = jnp.where((mexp > 0.5) & (kpos <= qpos_k), 0.0, NEG)
        yield
        o_s = yield from attend([pg[0:NSA_KVW, :] for pg in pages], [pg[NSA_KVW:ROWW, :] for pg in pages],
                                snew_ref[:, 0:NSA_KVW], snew_ref[:, NSA_KVW:ROWW], bias)
        emit(o_s, gates[1], False)
        while not window_out:
            yield
        emit(window_out[0], gates[2], False)
        o_ref[...] = out_ref[...].astype(o_ref.dtype)

    return [window_chain(), main_chain()]


def _nsa_sample(proj, kc, vc, slc_pool_t, win_cache_t, page_table, ovl, bexp, gexp, n_seq, t_new, past_len):
    n_pages = page_table.shape[1]
    w_buf = win_cache_t.shape[1]
    assert w_buf % KT == 0 and t_new == SUBLANES and NSA_HEADS * t_new <= SROWS and PAGE_SIZE == KT
    n_wt = w_buf // KT
    n_cmp = (past_len + t_new - CMP_LEN) // CMP_STRIDE + 1
    n_sel = -(-(past_len + t_new) // SEL_LEN)
    n_sb = math.gcd(n_seq, 4)
    rows = n_sb * t_new
    page_spec = lambda sb, p: pl.BlockSpec((ROWW, PAGE_SIZE), lambda b, pt: (pt[n_sb * b + sb, p], 0))
    in_specs = [
        pl.BlockSpec((rows, NSA_QW), lambda b, pt: (b, C_NQ // NSA_QW)),
        pl.BlockSpec((rows, ROWW), lambda b, pt: (b, C_SLC // ROWW)),
        pl.BlockSpec((rows, ROWW), lambda b, pt: (b, C_WIN // ROWW)),
        pl.BlockSpec((rows, TAILW), lambda b, pt: (b, C_TAIL // TAILW)),
        pl.BlockSpec((n_sb, NBLK, NSA_KVW), lambda b, pt: (b, 0, 0)),
        pl.BlockSpec((n_sb, NBLK, NSA_KVW), lambda b, pt: (b, 0, 0)),
        pl.BlockSpec((n_sb * ROWW, w_buf), lambda b, pt: (b, 0)),
    ] + [page_spec(sb, p) for sb in range(n_sb) for p in range(n_pages)] + [
        pl.BlockSpec(ovl.shape, lambda b, pt: (0, 0)),
        pl.BlockSpec(bexp.shape, lambda b, pt: (0, 0)),
        pl.BlockSpec(gexp.shape, lambda b, pt: (0, 0, 0)),
    ]
    return pl.pallas_call(
        functools.partial(_nsa_sample_kernel, n_sb=n_sb, n_pages=n_pages, n_wt=n_wt, t_new=t_new,
                          past_len=past_len, n_cmp=n_cmp, n_sel=n_sel),
        grid_spec=pltpu.PrefetchScalarGridSpec(
            num_scalar_prefetch=1, grid=(n_seq // n_sb,), in_specs=in_specs,
            out_specs=pl.BlockSpec((rows, NSA_QW), lambda b, pt: (b, 0)),
            scratch_shapes=[pltpu.VMEM((n_sb * NSA_HEADS * t_new, LANES), F32),
                            pltpu.VMEM((rows, NSA_QW), F32)]),
        out_shape=jax.ShapeDtypeStruct((n_seq * t_new, NSA_QW), F32),
        compiler_params=_cparams(("parallel",)),
        name="nsa_sample",
    )(page_table, proj, proj, proj, proj, kc, vc, win_cache_t, *([slc_pool_t] * (n_sb * n_pages)), ovl, bexp, gexp)


def _out_ln_kernel(og_ref, on_ref, wg_ref, wn_ref, x_ref, g_ref, b_ref, y_ref, yb_ref, *, alpha):
    tm = x_ref.shape[0]
    rows = math.gcd(tm, 256)
    for r0 in range(0, tm, rows):
        sl = slice(r0, r0 + rows)
        m = _dot(og_ref[sl, :].astype(BF16), wg_ref[...]) + _dot(on_ref[sl, :].astype(BF16), wn_ref[...])
        y = _layer_norm(alpha * x_ref[sl, :] + m, g_ref[...], b_ref[...])
        y_ref[sl, :] = y
        yb_ref[sl, :] = y.astype(BF16)


def _out_ln(og, on, wg, wn, x, g, b, alpha, tm):
    n, d = x.shape
    assert n % tm == 0
    return pl.pallas_call(
        functools.partial(_out_ln_kernel, alpha=alpha),
        grid=(n // tm,),
        in_specs=[
            pl.BlockSpec((tm, og.shape[1]), lambda i: (i, 0)),
            pl.BlockSpec((tm, on.shape[1]), lambda i: (i, 0)),
            pl.BlockSpec(wg.shape, lambda i: (0, 0), pipeline_mode=pl.Buffered(1)),
            pl.BlockSpec(wn.shape, lambda i: (0, 0), pipeline_mode=pl.Buffered(1)),
            pl.BlockSpec((tm, d), lambda i: (i, 0)),
            pl.BlockSpec((1, d), lambda i: (0, 0)),
            pl.BlockSpec((1, d), lambda i: (0, 0)),
        ],
        out_specs=[pl.BlockSpec((tm, d), lambda i: (i, 0)), pl.BlockSpec((tm, d), lambda i: (i, 0))],
        out_shape=[jax.ShapeDtypeStruct((n, d), F32), jax.ShapeDtypeStruct((n, d), BF16)],
        compiler_params=_cparams(("parallel",)),
        name="out_ln",
    )(og, on, wg, wn, x, g, b)


def _rope_tables(pos):
    half = NSA_HD // 2
    freq = ROPE_THETA ** (-jnp.arange(half, dtype=F32) / half)
    ang = pos.astype(F32)[:, None] * freq[None, :]
    cos, sin = jnp.cos(ang), jnp.sin(ang)
    cos_h = jnp.concatenate([cos, cos], -1)
    sin_h = jnp.concatenate([-sin, sin], -1)
    reps = LANES // NSA_HD
    return jnp.tile(cos_h, (1, reps)), jnp.tile(sin_h, (1, reps))


def _reorder_w_in(w):
    d = w.shape[0]
    sizes = (GLA_KW, GLA_KW, GLA_VW, GLA_GATE_RANK, GLA_VW, NSA_QW) + (NSA_KVW,) * 6 + (NSA_HEADS * 3,)
    offs = np.concatenate([[0], np.cumsum(sizes)])
    gq, gk, gv, ga, gr, nq, kc, vc, ks, vs, kw, vw, ng = [w[:, offs[i]:offs[i + 1]] for i in range(len(sizes))]
    pad = jnp.zeros((d, TAILW - GLA_GATE_RANK - NSA_HEADS * 3), w.dtype)
    return jnp.concatenate([gq, gk, gv, gr, nq, kc, vc, ks, vs, kw, vw, ga, ng, pad], axis=1)


def _block_diag2(w):
    z = jnp.zeros_like(w)
    return jnp.concatenate([jnp.concatenate([w, z], -1), jnp.concatenate([z, w], -1)], -2)


def _compress_params(pos, w1):
    half = CMP_LEN // 2
    hidden = w1.shape[-1]
    w1r = w1.reshape(2, 2, half, NSA_HD, hidden)
    eye = jnp.eye(2, dtype=w1.dtype)
    w1p = jnp.einsum("hg,palDn->plhDgan", eye, w1r).reshape(2, half * 2 * NSA_HD, 2 * 2 * hidden)
    posr = jnp.tile(pos.reshape(2, 2, half, 1, NSA_HD), (1, 1, 1, 2, 1)).reshape(2, 2, 1, half * 2 * NSA_HD)
    pos16 = jnp.tile(posr, (1, 1, SUBLANES, 1)).reshape(2, 2 * SUBLANES, half * 2 * NSA_HD)
    return pos16.astype(BF16), w1p.astype(BF16)


def _tile_size(n, pref):
    t = math.gcd(n, pref)
    assert t % SUBLANES == 0
    return t


def _const_tables(n_keys):
    cs = np.arange(NBLK)[:, None] * CMP_STRIDE
    ss = np.arange(NBLK)[None, :] * SEL_LEN
    ovl = ((cs < ss + SEL_LEN) & (cs + CMP_LEN > ss)).astype(np.float32)
    bexp = (np.arange(NBLK)[:, None] == (np.arange(n_keys)[None, :] // SEL_LEN)).astype(np.float32)
    gexp = np.zeros((3, LANES, NSA_QW), np.float32)
    for c in range(3):
        for j in range(NSA_HEADS):
            gexp[c, GATE_LANE0 + 3 * j + c, j * NSA_HD:(j + 1) * NSA_HD] = 1.0
    return jnp.asarray(ovl, BF16), jnp.asarray(bexp, BF16), jnp.asarray(gexp, BF16)


def kernel(x_prompt, x_sample, cache_cmp_kv, cache_slc_kv, cache_win_kv, state_gla, page_table,
           w_ffn1_in, w_ffn1_out, w_in, w_gla_gate2, b_gla_gate2, gla_norm_g,
           cmp_pos, cmp_w1, cmp_w2, w_out, w_ffn2_in, w_ffn2_out, ln_g, ln_b):
    B, T, D = x_prompt.shape
    DB, TS, _ = x_sample.shape
    depth = w_in.shape[0]
    n_pages = page_table.shape[1]
    past_len = n_pages * PAGE_SIZE
    w_buf = cache_win_kv.shape[2]
    alpha = (2.0 * depth) ** 0.25
    assert w_buf == WINDOW and T % TQ == 0 and past_len % KT == 0

    cos_p, sin_p = _rope_tables(jnp.arange(T, dtype=jnp.int32))
    cos_s, sin_s = _rope_tables(past_len + jnp.arange(TS, dtype=jnp.int32))
    ovl, bexp_p, gexp = _const_tables(T)
    _, bexp_s, _ = _const_tables(past_len + KT)

    hp = x_prompt.reshape(B * T, D)
    hs = x_sample.reshape(DB * TS, D)
    tm_s = _tile_size(DB * TS, PROJ_TM)
    cos_s, sin_s = (jnp.tile(t, (tm_s // TS, 1)) for t in (cos_s, sin_s))
    outs = {k: [] for k in ("cmp_p", "slc_p", "win_p", "gla_p", "cmp_s", "slc_s", "win_s", "gla_s")}

    for l in range(depth):
        wf1i, wf1o = w_ffn1_in[l].astype(BF16), w_ffn1_out[l].astype(BF16)
        wf2i, wf2o = w_ffn2_in[l].astype(BF16), w_ffn2_out[l].astype(BF16)
        wi = _reorder_w_in(w_in[l].astype(BF16))
        wo_g, wo_n = w_out[l][:GLA_VW].astype(BF16), w_out[l][GLA_VW:].astype(BF16)
        lg, lb = ln_g[l][:, None, :], ln_b[l][:, None, :]
        w2h = jnp.zeros((LANES, GLA_KW), F32).at[:GLA_GATE_RANK].set(w_gla_gate2[l]).astype(BF16)
        b2h = b_gla_gate2[l][None, :]
        gn = gla_norm_g[l][None, :]
        pos2, w1bd = _compress_params(cmp_pos[l], cmp_w1[l])
        w2bd = _block_diag2(cmp_w2[l]).astype(BF16)

        h1, h1b = _ffn_ln(hp, wf1i, wf1o, lg[0], lb[0], alpha, _tile_size(B * T, FFN_TM), FFN_TF)
        proj, cmp_t, slc_t, win_t = _in_proj(h1b, wi, cos_p, sin_p, _tile_size(T, PROJ_TM), PROJ_TN, True)
        og, gla_state = _gla(proj, w2h, b2h, gn, jnp.zeros((B, GLA_HEADS, GLA_DK, GLA_DV), F32), B, T)
        n_grp = T // CMP_STRIDE
        n_cmp = (T - CMP_LEN) // CMP_STRIDE + 1
        kc, vc = _compress([proj], [pl.BlockSpec((T, ROWW), lambda b: (b, C_CMP // ROWW))],
                           pos2, w1bd, w2bd, B, n_grp, n_cmp)
        kc, vc = (jnp.pad(a, ((0, 0), (0, NBLK - n_grp), (0, 0))) for a in (kc, vc))
        on = _nsa_prompt(proj, kc, vc, ovl, bexp_p, gexp, B, T)
        h2, h2b = _out_ln(og, on, wo_g, wo_n, h1, lg[1], lb[1], alpha, _tile_size(B * T, OUT_TM))
        hp, _ = _ffn_ln(h2, wf2i, wf2o, lg[2], lb[2], alpha, _tile_size(B * T, FFN_TM), FFN_TF)
        row6 = lambda a, n, t: a.reshape(n, t, 2, NSA_KV, NSA_HD)
        row6_t = lambda a: a.reshape(a.shape[0], 2, NSA_KV, NSA_HD, a.shape[2]).transpose(0, 4, 1, 2, 3)
        outs["cmp_p"].append(row6_t(cmp_t))
        outs["slc_p"].append(row6_t(slc_t))
        outs["win_p"].append(row6_t(win_t[:, :, T - w_buf:]))
        outs["gla_p"].append(gla_state)

        s1, s1b = _ffn_ln(hs, wf1i, wf1o, lg[0], lb[0], alpha, _tile_size(DB * TS, FFN_TM), FFN_TF)
        (sproj,) = _in_proj(s1b, wi, cos_s, sin_s, tm_s, PROJ_TN, False)
        sog, sgla_state = _gla(sproj, w2h, b2h, gn, state_gla[l], DB, TS)
        feature_major = lambda c: c.transpose(0, 2, 3, 4, 1).reshape(c.shape[0] * ROWW, c.shape[1])
        cmp_pool_t = feature_major(cache_cmp_kv[l])
        slc_pool_t = feature_major(cache_slc_kv[l])
        win_cache_t = feature_major(cache_win_kv[l])
        n_grp_s = past_len // CMP_STRIDE
        n_cmp_s = (past_len + TS - CMP_LEN) // CMP_STRIDE + 1
        page_specs = [pl.BlockSpec((ROWW, PAGE_SIZE), lambda b, pt, p=p: (pt[b, p], 0)) for p in range(n_pages)]
        skc, svc = _compress([cmp_pool_t] * n_pages, page_specs, pos2, w1bd, w2bd, DB, n_grp_s, n_cmp_s,
                             scalar_args=(page_table,), feature_major=True)
        skc, svc = (jnp.pad(a, ((0, 0), (0, NBLK - n_grp_s), (0, 0))) for a in (skc, svc))
        son = _nsa_sample(sproj, skc, svc, slc_pool_t, win_cache_t, page_table, ovl, bexp_s, gexp, DB, TS, past_len)
        s2, s2b = _out_ln(sog, son, wo_g, wo_n, s1, lg[1], lb[1], alpha, _tile_size(DB * TS, OUT_TM))
        hs, _ = _ffn_ln(s2, wf2i, wf2o, lg[2], lb[2], alpha, _tile_size(DB * TS, FFN_TM), FFN_TF)
        new_win = row6(sproj[:, C_WIN:C_WIN + ROWW], DB, TS)
        outs["cmp_s"].append(row6(sproj[:, C_CMP:C_CMP + ROWW], DB, TS))
        outs["slc_s"].append(row6(sproj[:, C_SLC:C_SLC + ROWW], DB, TS))
        outs["win_s"].append(jnp.concatenate([cache_win_kv[l], new_win], axis=1)[:, -w_buf:])
        outs["gla_s"].append(sgla_state)

    st = lambda k: jnp.stack(outs[k])
    return (hp.reshape(B, T, D), hs.reshape(DB, TS, D),
            st("cmp_p"), st("slc_p"), st("win_p"), st("gla_p"),
            st("cmp_s"), st("slc_s"), st("win_s"), st("gla_s"))
```
